```python
import functools
import jax
import jax.numpy as jnp
from jax import lax
import numpy as np

D_MODEL = 2048
BATCH = 2
SEQ = 4096
DEPTH = 2
DEC_BATCH = 128
DEC_SEQ = 1
PAST_LEN = 8192
PAGE_SIZE = 128

N_MLA_LAYERS = (DEPTH + 1) // 2
N_SSM_LAYERS = DEPTH // 2
EPS = 1e-6
N_MOD = 6

POOL_WIDTH = D_MODEL // 2
POOL_WINDOWS = (2, 4, 8, 16)
POOL_GROUP_DIM = POOL_WIDTH // len(POOL_WINDOWS)
POOL_BUF = max(POOL_WINDOWS) - 1

MLA_HEADS = 8
MLA_NOPE = 128
MLA_ROPE = 64
MLA_V = 128
MLA_QK = MLA_NOPE + MLA_ROPE
MLA_Q_LORA = D_MODEL // 4
MLA_KV_LORA = D_MODEL // 8
MLA_KV_DIM = MLA_KV_LORA + MLA_ROPE
MLA_OUT = MLA_HEADS * MLA_V
ROPE_THETA = 10000.0
Q_BLOCK = 128

EVEN_IN = POOL_WIDTH + MLA_Q_LORA + MLA_KV_LORA + MLA_ROPE
EVEN_MIX = POOL_WIDTH + MLA_OUT

SSD_D_INNER = D_MODEL
SSD_HEAD_DIM = 64
SSD_HEADS = SSD_D_INNER // SSD_HEAD_DIM
SSD_GROUPS = 4
SSD_STATE = 128
SSD_CHUNK = 128
SSD_CONV_DIM = SSD_D_INNER + 2 * SSD_GROUPS * SSD_STATE
CONV_WIDTH = 4

LRU_WIDTH = D_MODEL // 2
LRU_BLOCKS = 8
LRU_BLOCK_DIM = LRU_WIDTH // LRU_BLOCKS
LRU_C = 8.0

ODD_IN = SSD_D_INNER + SSD_CONV_DIM + SSD_HEADS + 2 * LRU_WIDTH
ODD_MIX = SSD_D_INNER + LRU_WIDTH

MLP_HIDDEN = 4 * D_MODEL

kernel_name = 'hybrid_pool_mla_ssd_rglru_decoder_step'


def rmsnorm(x, g):
    xf = x.astype(jnp.float32)
    y = xf * lax.rsqrt(jnp.mean(xf * xf, axis=-1, keepdims=True) + EPS)
    return (y * g.astype(jnp.float32)).astype(x.dtype)


def causal_dwconv(u, buf, w, b):
    ext = jnp.concatenate([buf.astype(u.dtype), u], axis=1)
    y = lax.conv_general_dilated(ext, w[:, None, :].astype(u.dtype), (1,), 'VALID',
                                 dimension_numbers=('NWC', 'WIO', 'NWC'),
                                 feature_group_count=u.shape[-1])
    return y + b.astype(u.dtype), ext[:, -(CONV_WIDTH - 1):]


def rope_tail(x, pos):
    half = MLA_ROPE // 2
    inv = ROPE_THETA ** (-jnp.arange(half, dtype=jnp.float32) * (2.0 / MLA_ROPE))
    ang = pos[:, None] * inv[None, :]
    cos = jnp.cos(ang)[None, :, None, :]
    sin = jnp.sin(ang)[None, :, None, :]
    r = x[..., MLA_NOPE:].astype(jnp.float32)
    r1, r2 = r[..., :half], r[..., half:]
    rot = jnp.concatenate([r1 * cos - r2 * sin, r1 * sin + r2 * cos], axis=-1).astype(x.dtype)
    return jnp.concatenate([x[..., :MLA_NOPE], rot], axis=-1)


def mla_keys(lat, kr, w_kvb, g_k, pos):
    b, t, _ = lat.shape
    kv = (lat @ w_kvb).reshape(b, t, MLA_HEADS, MLA_NOPE + MLA_V)
    k_nope, v = kv[..., :MLA_NOPE], kv[..., MLA_NOPE:]
    k_rope = jnp.broadcast_to(kr[:, :, None, :], (b, t, MLA_HEADS, MLA_ROPE)).astype(k_nope.dtype)
    k = rope_tail(rmsnorm(jnp.concatenate([k_nope, k_rope], axis=-1), g_k), pos)
    return k, v


def mla_prompt(q, lat, kr, w_kvb, g_k):
    b, t = q.shape[:2]
    k, v = mla_keys(lat, kr, w_kvb, g_k, jnp.arange(t, dtype=jnp.float32))
    scale = MLA_QK ** -0.5
    kpos = jnp.arange(t)

    def block(i):
        qb = lax.dynamic_slice_in_dim(q, i * Q_BLOCK, Q_BLOCK, axis=1)
        s = jnp.einsum('bqhd,bkhd->bhqk', qb, k, preferred_element_type=jnp.float32) * scale
        qpos = i * Q_BLOCK + jnp.arange(Q_BLOCK)
        s = jnp.where(kpos[None, :] <= qpos[:, None], s, -jnp.inf)
        p = jax.nn.softmax(s, axis=-1).astype(v.dtype)
        return jnp.einsum('bhqk,bkhd->bqhd', p, v)

    o = lax.map(block, jnp.arange(t // Q_BLOCK))
    return jnp.moveaxis(o, 0, 1).reshape(b, t, MLA_OUT)


def mla_decode(q, lat, kr, cache_l, page_table, pos0, w_kvb, g_k):
    b, t = q.shape[:2]
    past_len = page_table.shape[1] * PAGE_SIZE
    total = past_len + t
    pos_all = jnp.arange(total, dtype=jnp.float32)
    scale = MLA_QK ** -0.5
    mask = jnp.arange(total)[None, :] <= (pos0 + jnp.arange(t))[:, None]

    def one(args):
        q_s, pt, lat_s, kr_s = args
        past = cache_l[pt].reshape(past_len, MLA_KV_DIM).astype(lat_s.dtype)
        lat_all = jnp.concatenate([past[:, :MLA_KV_LORA], lat_s], axis=0)
        kr_all = jnp.concatenate([past[:, MLA_KV_LORA:], kr_s], axis=0)
        k, v = mla_keys(lat_all[None], kr_all[None], w_kvb, g_k, pos_all)
        s = jnp.einsum('thd,khd->htk', q_s, k[0], preferred_element_type=jnp.float32) * scale
        p = jax.nn.softmax(jnp.where(mask[None], s, -jnp.inf), axis=-1).astype(v.dtype)
        return jnp.einsum('htk,khd->thd', p, v[0])

    o = lax.map(one, (q, page_table, lat, kr))
    return o.reshape(b, t, MLA_OUT)


def pool_mix(u, buf, pos0, w_pool, s_pool):
    b, t, _ = u.shape
    ext = jnp.concatenate([buf.astype(u.dtype), u], axis=1)
    n_ext = ext.shape[1]
    ext32 = ext.astype(jnp.float32)
    cs0 = jnp.concatenate([jnp.zeros((b, 1, POOL_WIDTH), jnp.float32), jnp.cumsum(ext32, axis=1)], axis=1)
    pos = pos0 + jnp.arange(t, dtype=jnp.float32)
    outs = []
    for g, w in enumerate(POOL_WINDOWS):
        sl = slice(g * POOL_GROUP_DIM, (g + 1) * POOL_GROUP_DIM)
        c = cs0[..., sl]
        lag = jnp.concatenate([jnp.zeros((b, w, POOL_GROUP_DIM), jnp.float32), c[:, :n_ext + 1 - w]], axis=1)
        wsum = (c - lag)[:, 1 + POOL_BUF:]
        cnt = jnp.minimum(pos + 1.0, float(w))
        outs.append(wsum / cnt[None, :, None] - ext32[:, POOL_BUF:, sl])
    d = jnp.stack(outs, axis=2).astype(u.dtype)
    y = jnp.einsum('btgc,gcd->btgd', d, w_pool).reshape(b, t, POOL_WIDTH) * s_pool
    return y, ext[:, -POOL_BUF:]


def ssd_chunked(x, dt, a, bm, cm, h0):
    b, t, nh, hp = x.shape
    ng, ns = bm.shape[2:]
    hg = nh // ng
    l = SSD_CHUNK if t % SSD_CHUNK == 0 else t
    nc = t // l
    f32 = jnp.float32
    xr = x.astype(f32).reshape(b, nc, l, ng, hg, hp)
    dtr = dt.reshape(b, nc, l, ng, hg)
    br = bm.astype(f32).reshape(b, nc, l, ng, ns)
    cr = cm.astype(f32).reshape(b, nc, l, ng, ns)
    acum = jnp.cumsum(jnp.moveaxis(dtr * a.reshape(ng, hg), 2, -1), axis=-1)
    xdt = xr * dtr[..., None]
    causal = jnp.tril(jnp.ones((l, l), bool))
    decay = jnp.exp(jnp.where(causal, acum[..., :, None] - acum[..., None, :], -jnp.inf))
    cb = jnp.einsum('bctgn,bcsgn->bcgts', cr, br)
    y_diag = jnp.einsum('bcgts,bcghts,bcsghp->bctghp', cb, decay, xdt)
    decay_end = jnp.exp(acum[..., -1:] - acum)
    s_chunk = jnp.einsum('bcsgn,bcghs,bcsghp->bcghpn', br, decay_end, xdt)
    chunk_decay = jnp.exp(acum[..., -1])

    def step(h, inp):
        dec, s = inp
        return dec[..., None, None] * h + s, h

    h_last, h_in = lax.scan(step, h0.astype(f32).reshape(b, ng, hg, hp, ns),
                            (jnp.moveaxis(chunk_decay, 1, 0), jnp.moveaxis(s_chunk, 1, 0)))
    h_in = jnp.moveaxis(h_in, 0, 1)
    y_off = jnp.einsum('bctgn,bcght,bcghpn->bctghp', cr, jnp.exp(acum), h_in)
    return (y_diag + y_off).reshape(b, t, nh, hp), h_last.reshape(b, nh, hp, ns)


def ssd_mix(u_z, u_xbc, u_dt, conv_buf, h0, w_conv, b_conv, dt_bias, a_log, d_skip, g_norm):
    b, t, _ = u_z.shape
    f32 = jnp.float32
    xbc, new_buf = causal_dwconv(u_xbc, conv_buf, w_conv, b_conv)
    xbc = jax.nn.silu(xbc)
    xs, bs, cs = jnp.split(xbc, [SSD_D_INNER, SSD_D_INNER + SSD_GROUPS * SSD_STATE], axis=-1)
    xs = xs.reshape(b, t, SSD_HEADS, SSD_HEAD_DIM)
    bs = bs.reshape(b, t, SSD_GROUPS, SSD_STATE)
    cs = cs.reshape(b, t, SSD_GROUPS, SSD_STATE)
    dt = jax.nn.softplus(u_dt.astype(f32) + dt_bias.astype(f32))
    a = -jnp.exp(a_log.astype(f32))
    y, h_last = ssd_chunked(xs, dt, a, bs, cs, h0)
    y = y + d_skip.astype(f32)[:, None] * xs.astype(f32)
    y = (y.reshape(b, t, SSD_D_INNER) * jax.nn.silu(u_z.astype(f32))).reshape(b, t, SSD_GROUPS, -1)
    y = rmsnorm(y, g_norm.reshape(SSD_GROUPS, -1)).reshape(b, t, SSD_D_INNER).astype(u_z.dtype)
    return y, new_buf, h_last.astype(h0.dtype)


def lru_mix(u_x, u_gate, conv_buf, h0, pos0, w_conv, b_conv, w_r, b_r, w_i, b_i, lam):
    b, t, _ = u_x.shape
    f32 = jnp.float32
    xc, new_buf = causal_dwconv(u_x, conv_buf, w_conv, b_conv)
    xb = xc.reshape(b, t, LRU_BLOCKS, LRU_BLOCK_DIM)
    r = jax.nn.sigmoid(jnp.einsum('btnc,ncd->btnd', xb, w_r).reshape(b, t, LRU_WIDTH).astype(f32) + b_r.astype(f32))
    i = jax.nn.sigmoid(jnp.einsum('btnc,ncd->btnd', xb, w_i).reshape(b, t, LRU_WIDTH).astype(f32) + b_i.astype(f32))
    log_a = -LRU_C * r * jax.nn.softplus(-lam.astype(f32))
    a = jnp.exp(log_a)
    pos = pos0 + jnp.arange(t)
    mult = jnp.where((pos == 0)[None, :, None], 1.0, jnp.sqrt(-jnp.expm1(2.0 * log_a)))
    bt = mult * i * xc.astype(f32)
    bt = bt.at[:, 0].add(a[:, 0] * h0.astype(f32))

    def combine(lhs, rhs):
        return lhs[0] * rhs[0], rhs[0] * lhs[1] + rhs[1]

    _, h = lax.associative_scan(combine, (a, bt), axis=1)
    y = (h * jax.nn.gelu(u_gate.astype(f32))).astype(u_x.dtype)
    return y, new_buf, h[:, -1].astype(h0.dtype)


def run_trunk(x, c, pos0, pool_buf, ssd_conv_buf, ssd_state, lru_conv_buf, lru_state, mla_cache, page_table, *,
              g_norm1, g_norm2, w_mod, b_mod, w_mlp1, w_mlp2,
              w_in_e, w_pool, s_pool, g_qlat, w_qb, g_kvlat, w_kvb, g_q, g_k, w_out_e,
              w_in_o, w_conv_ssd, b_conv_ssd, dt_bias, a_log, d_skip, g_ssd_norm,
              w_conv_lru, b_conv_lru, w_lru_r, b_lru_r, w_lru_i, b_lru_i, lru_lambda, w_out_o):
    b, t, _ = x.shape
    pos = pos0 + jnp.arange(t, dtype=jnp.float32)
    mla_rows, pool_new, sconv_new, ssd_new, lconv_new, lru_new = [], [], [], [], [], []
    for layer in range(DEPTH):
        mod = jax.nn.silu(c) @ w_mod[layer] + b_mod[layer]
        sh1, sc1, gt1, sh2, sc2, gt2 = [m[:, None, :] for m in jnp.split(mod, N_MOD, axis=-1)]
        hn = rmsnorm(x, g_norm1[layer]) * (1.0 + sc1) + sh1
        if layer % 2 == 0:
            e = layer // 2
            u = hn @ w_in_e[e]
            u_pool, u_q, u_kv, u_kr = jnp.split(
                u, [POOL_WIDTH, POOL_WIDTH + MLA_Q_LORA, POOL_WIDTH + MLA_Q_LORA + MLA_KV_LORA], axis=-1)
            y_pool, pb = pool_mix(u_pool, pool_buf[e], pos0, w_pool[e], s_pool[e])
            q = (rmsnorm(u_q, g_qlat[e]) @ w_qb[e]).reshape(b, t, MLA_HEADS, MLA_QK)
            q = rope_tail(rmsnorm(q, g_q[e]), pos)
            lat = rmsnorm(u_kv, g_kvlat[e])
            if mla_cache is None:
                y_att = mla_prompt(q, lat, u_kr, w_kvb[e], g_k[e])
            else:
                y_att = mla_decode(q, lat, u_kr, mla_cache[e], page_table, pos0, w_kvb[e], g_k[e])
            mix = jnp.concatenate([y_pool, y_att.astype(y_pool.dtype)], axis=-1) @ w_out_e[e]
            mla_rows.append(jnp.concatenate([lat, u_kr], axis=-1))
            pool_new.append(pb)
        else:
            o = layer // 2
            u = hn @ w_in_o[o]
            j1 = SSD_D_INNER
            j2 = j1 + SSD_CONV_DIM
            j3 = j2 + SSD_HEADS
            j4 = j3 + LRU_WIDTH
            u_z, u_xbc, u_dt, u_lx, u_lg = jnp.split(u, [j1, j2, j3, j4], axis=-1)
            y_ssd, scb, hs = ssd_mix(u_z, u_xbc, u_dt, ssd_conv_buf[o], ssd_state[o], w_conv_ssd[o], b_conv_ssd[o],
                                     dt_bias[o], a_log[o], d_skip[o], g_ssd_norm[o])
            y_lru, lcb, hl = lru_mix(u_lx, u_lg, lru_conv_buf[o], lru_state[o], pos0, w_conv_lru[o], b_conv_lru[o],
                                     w_lru_r[o], b_lru_r[o], w_lru_i[o], b_lru_i[o], lru_lambda[o])
            mix = jnp.concatenate([y_ssd, y_lru], axis=-1) @ w_out_o[o]
            sconv_new.append(scb)
            ssd_new.append(hs)
            lconv_new.append(lcb)
            lru_new.append(hl)
        x = x + gt1 * mix
        hn2 = rmsnorm(x, g_norm2[layer]) * (1.0 + sc2) + sh2
        x = x + gt2 * (jnp.square(jax.nn.relu(hn2 @ w_mlp1[layer])) @ w_mlp2[layer])
    return (x, jnp.stack(mla_rows), jnp.stack(pool_new), jnp.stack(sconv_new), jnp.stack(ssd_new),
            jnp.stack(lconv_new), jnp.stack(lru_new))


def setup_inputs(seed: int = 0) -> dict:
    key = jax.random.key(seed)
    ks = iter(jax.random.split(key, 64))
    f32 = jnp.float32

    def nrm(shape, scale=1.0):
        return jax.random.normal(next(ks), shape, f32) * scale

    def gain(shape):
        return 1.0 + 0.1 * jax.random.normal(next(ks), shape, f32)

    def unif(shape, lo, hi):
        return jax.random.uniform(next(ks), shape, f32, lo, hi)

    n_pages = PAST_LEN // PAGE_SIZE
    n_used = DEC_BATCH * n_pages
    n_phys = n_used + (n_used + 3) // 4
    ne, no = N_MLA_LAYERS, N_SSM_LAYERS

    x_prompt = nrm((BATCH, SEQ, D_MODEL))
    x_sample = nrm((DEC_BATCH, DEC_SEQ, D_MODEL))
    cache_mla = nrm((ne, n_phys, PAGE_SIZE, MLA_KV_DIM))
    state_pool = nrm((ne, DEC_BATCH, POOL_BUF, POOL_WIDTH))
    state_ssd_conv = nrm((no, DEC_BATCH, CONV_WIDTH - 1, SSD_CONV_DIM))
    state_ssd = nrm((no, DEC_BATCH, SSD_HEADS, SSD_HEAD_DIM, SSD_STATE), 0.1)
    state_lru_conv = nrm((no, DEC_BATCH, CONV_WIDTH - 1, LRU_WIDTH))
    state_lru = nrm((no, DEC_BATCH, LRU_WIDTH))
    page_table = jax.random.permutation(next(ks), n_phys)[:n_used].reshape(DEC_BATCH, n_pages).astype(jnp.int32)
    c_prompt = nrm((BATCH, D_MODEL))
    c_sample = nrm((DEC_BATCH, D_MODEL))

    g_norm1 = gain((DEPTH, D_MODEL))
    g_norm2 = gain((DEPTH, D_MODEL))
    w_mod = nrm((DEPTH, D_MODEL, N_MOD * D_MODEL), 0.5 * D_MODEL ** -0.5)
    b_mod = nrm((DEPTH, N_MOD * D_MODEL), 0.01)
    w_mlp1 = nrm((DEPTH, D_MODEL, MLP_HIDDEN), D_MODEL ** -0.5)
    w_mlp2 = nrm((DEPTH, MLP_HIDDEN, D_MODEL), MLP_HIDDEN ** -0.5)

    w_in_e = nrm((ne, D_MODEL, EVEN_IN), D_MODEL ** -0.5)
    w_pool = nrm((ne, len(POOL_WINDOWS), POOL_GROUP_DIM, POOL_GROUP_DIM), POOL_GROUP_DIM ** -0.5)
    s_pool = gain((ne, POOL_WIDTH))
    g_qlat = gain((ne, MLA_Q_LORA))
    w_qb = nrm((ne, MLA_Q_LORA, MLA_HEADS * MLA_QK), MLA_Q_LORA ** -0.5)
    g_kvlat = gain((ne, MLA_KV_LORA))
    w_kvb = nrm((ne, MLA_KV_LORA, MLA_HEADS * (MLA_NOPE + MLA_V)), MLA_KV_LORA ** -0.5)
    g_q = gain((ne, MLA_QK))
    g_k = gain((ne, MLA_QK))
    w_out_e = nrm((ne, EVEN_MIX, D_MODEL), EVEN_MIX ** -0.5)

    w_in_o = nrm((no, D_MODEL, ODD_IN), D_MODEL ** -0.5)
    w_conv_ssd = nrm((no, CONV_WIDTH, SSD_CONV_DIM), CONV_WIDTH ** -0.5)
    b_conv_ssd = nrm((no, SSD_CONV_DIM), 0.01)
    dt0 = jnp.exp(unif((no, SSD_HEADS), float(np.log(1e-3)), float(np.log(1e-1))))
    dt_bias = dt0 + jnp.log(-jnp.expm1(-dt0))
    a_log = jnp.log(unif((no, SSD_HEADS), 1.0, 16.0))
    d_skip = gain((no, SSD_HEADS))
    g_ssd_norm = gain((no, SSD_D_INNER))
    w_conv_lru = nrm((no, CONV_WIDTH, LRU_WIDTH), CONV_WIDTH ** -0.5)
    b_conv_lru = nrm((no, LRU_WIDTH), 0.01)
    w_lru_r = nrm((no, LRU_BLOCKS, LRU_BLOCK_DIM, LRU_BLOCK_DIM), LRU_BLOCK_DIM ** -0.5)
    b_lru_r = nrm((no, LRU_WIDTH), 0.01)
    w_lru_i = nrm((no, LRU_BLOCKS, LRU_BLOCK_DIM, LRU_BLOCK_DIM), LRU_BLOCK_DIM ** -0.5)
    b_lru_i = nrm((no, LRU_WIDTH), 0.01)
    a_pow = unif((no, LRU_WIDTH), 0.9, 0.999)
    p_sig = a_pow ** (1.0 / LRU_C)
    lru_lambda = jnp.log(p_sig) - jnp.log1p(-p_sig)
    w_out_o = nrm((no, ODD_MIX, D_MODEL), ODD_MIX ** -0.5)

    return {'x_prompt': x_prompt, 'x_sample': x_sample, 'cache_mla': cache_mla, 'state_pool': state_pool,
            'state_ssd_conv': state_ssd_conv, 'state_ssd': state_ssd, 'state_lru_conv': state_lru_conv,
            'state_lru': state_lru, 'page_table': page_table, 'c_prompt': c_prompt, 'c_sample': c_sample,
            'g_norm1': g_norm1, 'g_norm2': g_norm2, 'w_mod': w_mod, 'b_mod': b_mod, 'w_mlp1': w_mlp1,
            'w_mlp2': w_mlp2, 'w_in_e': w_in_e, 'w_pool': w_pool, 's_pool': s_pool, 'g_qlat': g_qlat,
            'w_qb': w_qb, 'g_kvlat': g_kvlat, 'w_kvb': w_kvb, 'g_q': g_q, 'g_k': g_k, 'w_out_e': w_out_e,
            'w_in_o': w_in_o, 'w_conv_ssd': w_conv_ssd, 'b_conv_ssd': b_conv_ssd, 'dt_bias': dt_bias,
            'a_log': a_log, 'd_skip': d_skip, 'g_ssd_norm': g_ssd_norm, 'w_conv_lru': w_conv_lru,
            'b_conv_lru': b_conv_lru, 'w_lru_r': w_lru_r, 'b_lru_r': b_lru_r, 'w_lru_i': w_lru_i,
            'b_lru_i': b_lru_i, 'lru_lambda': lru_lambda, 'w_out_o': w_out_o}


def reference(x_prompt, x_sample, cache_mla, state_pool, state_ssd_conv, state_ssd, state_lru_conv, state_lru,
              page_table, c_prompt, c_sample, g_norm1, g_norm2, w_mod, b_mod, w_mlp1, w_mlp2,
              w_in_e, w_pool, s_pool, g_qlat, w_qb, g_kvlat, w_kvb, g_q, g_k, w_out_e,
              w_in_o, w_conv_ssd, b_conv_ssd, dt_bias, a_log, d_skip, g_ssd_norm,
              w_conv_lru, b_conv_lru, w_lru_r, b_lru_r, w_lru_i, b_lru_i, lru_lambda, w_out_o):
    trunk = functools.partial(
        run_trunk, g_norm1=g_norm1, g_norm2=g_norm2, w_mod=w_mod, b_mod=b_mod, w_mlp1=w_mlp1, w_mlp2=w_mlp2,
        w_in_e=w_in_e, w_pool=w_pool, s_pool=s_pool, g_qlat=g_qlat, w_qb=w_qb, g_kvlat=g_kvlat, w_kvb=w_kvb,
        g_q=g_q, g_k=g_k, w_out_e=w_out_e, w_in_o=w_in_o, w_conv_ssd=w_conv_ssd, b_conv_ssd=b_conv_ssd,
        dt_bias=dt_bias, a_log=a_log, d_skip=d_skip, g_ssd_norm=g_ssd_norm, w_conv_lru=w_conv_lru,
        b_conv_lru=b_conv_lru, w_lru_r=w_lru_r, b_lru_r=b_lru_r, w_lru_i=w_lru_i, b_lru_i=b_lru_i,
        lru_lambda=lru_lambda, w_out_o=w_out_o)
    bp = x_prompt.shape[0]
    dtp = x_prompt.dtype
    zero_pool = jnp.zeros((N_MLA_LAYERS, bp, POOL_BUF, POOL_WIDTH), dtp)
    zero_sconv = jnp.zeros((N_SSM_LAYERS, bp, CONV_WIDTH - 1, SSD_CONV_DIM), dtp)
    zero_ssd = jnp.zeros((N_SSM_LAYERS, bp, SSD_HEADS, SSD_HEAD_DIM, SSD_STATE), jnp.float32)
    zero_lconv = jnp.zeros((N_SSM_LAYERS, bp, CONV_WIDTH - 1, LRU_WIDTH), dtp)
    zero_lru = jnp.zeros((N_SSM_LAYERS, bp, LRU_WIDTH), jnp.float32)
    y_prompt, mla_p, pool_p, sconv_p, ssd_p, lconv_p, lru_p = trunk(
        x_prompt, c_prompt, 0, zero_pool, zero_sconv, zero_ssd, zero_lconv, zero_lru, None, None)
    past_len = page_table.shape[1] * PAGE_SIZE
    y_sample, mla_s, pool_s, sconv_s, ssd_s, lconv_s, lru_s = trunk(
        x_sample, c_sample, past_len, state_pool, state_ssd_conv, state_ssd, state_lru_conv, state_lru,
        cache_mla, page_table)
    return (y_prompt, y_sample, mla_p, mla_s, pool_p, pool_s, sconv_p, sconv_s, ssd_p, ssd_s,
            lconv_p, lconv_s, lru_p, lru_s)
```

```python
import functools

import jax
import jax.numpy as jnp
from jax import lax
from jax.experimental import pallas as pl
from jax.experimental.pallas import tpu as pltpu

F32 = jnp.float32
BF16 = jnp.bfloat16

VMEM_LIMIT_BYTES = 48 * 1024 * 1024
LANES = 128

D_MODEL = 2048
EPS = 1e-6
N_MOD = 6
DEPTH = 2

POOL_WIDTH = 1024
POOL_WINDOWS = (2, 4, 8, 16)
POOL_GROUP_DIM = 256
POOL_BUF = 15

MLA_HEADS = 8
MLA_NOPE = 128
MLA_ROPE = 64
MLA_V = 128
MLA_QK = 192
MLA_QKP = 256
MLA_Q_LORA = 512
MLA_KV_LORA = 256
ROPE_THETA = 10000.0
PAGE_SIZE = 128
EVEN_IN_PAD = 1920

SSD_D_INNER = 2048
SSD_HEAD_DIM = 64
SSD_HEADS = 32
SSD_GROUPS = 4
SSD_STATE = 128
SSD_CHUNK = 128
SSD_BC = 1024
CONV_WIDTH = 4

LRU_WIDTH = 1024
LRU_BLOCKS = 8
LRU_BLOCK_DIM = 128
LRU_C = 8.0

MLP_HIDDEN = 8192


def _cparams(*sem):
    return pltpu.CompilerParams(dimension_semantics=sem, vmem_limit_bytes=VMEM_LIMIT_BYTES)


def _sigmoid(x):
    return 1.0 / (1.0 + jnp.exp(-x))


def _silu(x):
    return x * _sigmoid(x)


def _softplus(x):
    return jnp.maximum(x, 0.0) + jnp.log1p(jnp.exp(-jnp.abs(x)))


def _gelu_tanh(x):
    return 0.5 * x * (1.0 + jnp.tanh(0.7978845608028654 * (x + 0.044715 * (x * x * x))))


def _dot(a, b):
    return jnp.dot(a, b, preferred_element_type=F32)


def _dot_nt(a, b):
    return lax.dot_general(a, b, (((1,), (1,)), ((), ())), preferred_element_type=F32)


def _split3(x):
    hi = x.astype(BF16)
    r = x - hi.astype(F32)
    mid = r.astype(BF16)
    lo = (r - mid.astype(F32)).astype(BF16)
    return hi, mid, lo


def _dot_sel(x, sel):
    hi, mid, lo = _split3(x)
    return _dot(hi, sel) + _dot(mid, sel) + _dot(lo, sel)


def _mod_kernel(c_ref, w_ref, b_ref, o_ref):
    c = c_ref[...]
    o_ref[...] = _dot(_silu(c).astype(BF16), w_ref[...].astype(BF16)) + b_ref[...]


def _modulation(c_all, w_mod, b_mod):
    mp = c_all.shape[0]
    n = N_MOD * D_MODEL
    tn = 1024
    return pl.pallas_call(
        _mod_kernel,
        grid=(DEPTH, n // tn),
        in_specs=[pl.BlockSpec((mp, D_MODEL), lambda l, j: (0, 0)),
                  pl.BlockSpec((None, D_MODEL, tn), lambda l, j: (l, 0, j)),
                  pl.BlockSpec((None, 1, tn), lambda l, j: (l, 0, j))],
        out_specs=pl.BlockSpec((None, mp, tn), lambda l, j: (l, 0, j)),
        out_shape=jax.ShapeDtypeStruct((DEPTH, mp, n), F32),
        compiler_params=_cparams("parallel", "parallel"),
    )(c_all, w_mod, b_mod.reshape(DEPTH, 1, n))


class _Rows:
    def __init__(self, nb, t, tm):
        self.nb, self.t, self.tm = nb, t, tm
        self.m = nb * t
        self.per_seq = t > 1
        if self.per_seq:
            assert t % tm == 0
            self.tiles_per_seq = t // tm

    def mod_array(self, mod_rows):
        return mod_rows.reshape(self.nb, 1, -1) if self.per_seq else mod_rows

    def mod_spec(self, chunk, width, ncol, col_of):
        per_chunk = D_MODEL // width
        if self.per_seq:
            tps = self.tiles_per_seq
            return pl.BlockSpec((None, 1, width),
                                lambda *g: (g[0] // tps, 0, chunk * per_chunk + col_of(*g)))
        return pl.BlockSpec((self.tm, width), lambda *g: (g[0], chunk * per_chunk + col_of(*g)))


def _norm_mod_kernel(x_ref, g_ref, sc_ref, sh_ref, o_ref):
    x = x_ref[...]
    y = x * lax.rsqrt(jnp.mean(x * x, axis=-1, keepdims=True) + EPS) * g_ref[...]
    o_ref[...] = (y * (1.0 + sc_ref[...]) + sh_ref[...]).astype(o_ref.dtype)


def _norm_mod(rows, x, g, mod, sc_chunk, sh_chunk):
    tm = min(rows.tm, 512)
    r = _Rows(rows.nb, rows.t, tm)
    return pl.pallas_call(
        _norm_mod_kernel,
        grid=(r.m // tm,),
        in_specs=[pl.BlockSpec((tm, D_MODEL), lambda i: (i, 0)),
                  pl.BlockSpec((1, D_MODEL), lambda i: (0, 0)),
                  r.mod_spec(sc_chunk, D_MODEL, 1, lambda i: 0),
                  r.mod_spec(sh_chunk, D_MODEL, 1, lambda i: 0)],
        out_specs=pl.BlockSpec((tm, D_MODEL), lambda i: (i, 0)),
        out_shape=jax.ShapeDtypeStruct((r.m, D_MODEL), BF16),
        compiler_params=_cparams("parallel"),
    )(x, g.reshape(1, D_MODEL), mod, mod)


def _mm_kernel(*refs, n_a, nk, act, has_res):
    a_refs = refs[:n_a]
    w_refs = refs[n_a:2 * n_a]
    pos = 2 * n_a
    if has_res:
        x_ref, gt_ref = refs[pos], refs[pos + 1]
        pos += 2
    o_ref = refs[pos]
    acc_ref = refs[pos + 1] if nk > 1 else None

    part = _dot(a_refs[0][...], w_refs[0][...])
    for a_ref, w_ref in zip(a_refs[1:], w_refs[1:]):
        part = part + _dot(a_ref[...], w_ref[...])

    def finish(acc):
        if act == "relu2":
            acc = jnp.square(jnp.maximum(acc, 0.0))
        if has_res:
            acc = x_ref[...] + gt_ref[...] * acc
        o_ref[...] = acc.astype(o_ref.dtype)

    if nk == 1:
        finish(part)
    else:
        k = pl.program_id(2)

        @pl.when(k == 0)
        def _():
            acc_ref[...] = part

        @pl.when(k > 0)
        def _():
            acc_ref[...] += part

        @pl.when(k == nk - 1)
        def _():
            finish(acc_ref[...])


def _matmul(rows, a_list, w_list, *, tn, tk=None, act=None, res=None, mod=None, gate_chunk=None,
            out_dtype=F32):
    tm, m = rows.tm, rows.m
    n = w_list[0].shape[1]
    n_a = len(a_list)
    if tk is None or n_a > 1:
        nk = 1
    else:
        assert a_list[0].shape[1] % tk == 0
        nk = a_list[0].shape[1] // tk
    assert n % tn == 0 and m % tm == 0
    in_specs, args = [], []
    for a in a_list:
        kk = a.shape[1] if nk == 1 else tk
        in_specs.append(pl.BlockSpec((tm, kk), lambda i, j, k: (i, k)))
        args.append(a)
    for w in w_list:
        kk = w.shape[0] if nk == 1 else tk
        in_specs.append(pl.BlockSpec((kk, tn), lambda i, j, k: (k, j)))
        args.append(w)
    has_res = res is not None
    if has_res:
        in_specs.append(pl.BlockSpec((tm, tn), lambda i, j, k: (i, j)))
        in_specs.append(rows.mod_spec(gate_chunk, tn, n // tn, lambda i, j, k: j))
        args += [res, mod]
    scratch = [pltpu.VMEM((tm, tn), F32)] if nk > 1 else []
    return pl.pallas_call(
        functools.partial(_mm_kernel, n_a=n_a, nk=nk, act=act, has_res=has_res),
        grid=(m // tm, n // tn, nk),
        in_specs=in_specs,
        out_specs=pl.BlockSpec((tm, tn), lambda i, j, k: (i, j)),
        out_shape=jax.ShapeDtypeStruct((m, n), out_dtype),
        scratch_shapes=scratch,
        compiler_params=_cparams("parallel", "parallel", "arbitrary"),
    )(*args)


def _bmm_kernel(a_ref, w_ref, o_ref):
    o_ref[...] = _dot(a_ref[...], w_ref[...]).astype(o_ref.dtype)


def _bmm_heads(a, w, out_dtype):
    h, m, k = a.shape
    n = w.shape[2]
    return pl.pallas_call(
        _bmm_kernel,
        grid=(h,),
        in_specs=[pl.BlockSpec((None, m, k), lambda i: (i, 0, 0)),
                  pl.BlockSpec((None, k, n), lambda i: (i, 0, 0))],
        out_specs=pl.BlockSpec((None, m, n), lambda i: (i, 0, 0)),
        out_shape=jax.ShapeDtypeStruct((h, m, n), out_dtype),
        compiler_params=_cparams("parallel"),
    )(a, w)


CONV_HALO = 8


def _conv_chunk(ext_ref, u, buf_ref, w_ref, b_ref, first):
    tc = u.shape[0]
    k1 = CONV_WIDTH - 1

    @pl.when(first)
    def _():
        ext_ref[CONV_HALO - k1:CONV_HALO, :] = buf_ref[...]

    ext_ref[CONV_HALO:CONV_HALO + tc, :] = u
    y = b_ref[...] + w_ref[k1:k1 + 1, :] * u
    for k in range(k1):
        y = y + w_ref[k:k + 1, :] * ext_ref[CONV_HALO - k1 + k:CONV_HALO - k1 + k + tc, :]
    ext_ref[CONV_HALO - k1:CONV_HALO, :] = ext_ref[CONV_HALO + tc - k1:CONV_HALO + tc, :]
    return y


def _conv_step(buf_ref, u, w_ref, b_ref):
    y = b_ref[...] + w_ref[CONV_WIDTH - 1:CONV_WIDTH, :] * u
    for k in range(CONV_WIDTH - 1):
        y = y + w_ref[k:k + 1, :] * buf_ref[k]
    return y


def _pool_prompt_kernel(u_ref, buf_ref, w_ref, s_ref, o_ref, ext_ref, *, tc, pos0):
    c = pl.program_id(1)
    halo = 16

    @pl.when(c == 0)
    def _():
        ext_ref[0:1, :] = jnp.zeros((1, POOL_WIDTH), F32)
        ext_ref[1:halo, :] = buf_ref[...]

    u = u_ref[...]
    ext_ref[halo:halo + tc, :] = u
    pos = (pos0 + c * tc + lax.broadcasted_iota(jnp.int32, (tc, 1), 0)).astype(F32)
    outs = []
    for g, w in enumerate(POOL_WINDOWS):
        lo, hi = g * POOL_GROUP_DIM, (g + 1) * POOL_GROUP_DIM
        x = u[:, lo:hi]
        wsum = x
        for j in range(1, w):
            wsum = wsum + ext_ref[halo - j:halo - j + tc, lo:hi]
        cnt = jnp.minimum(pos + 1.0, float(w))
        d = (wsum / cnt - x).astype(BF16)
        outs.append(_dot(d, w_ref[g]))
    y = jnp.concatenate(outs, axis=1) * s_ref[...]
    o_ref[...] = y.astype(o_ref.dtype)
    ext_ref[0:halo, :] = ext_ref[tc:tc + halo, :]


def _pool_prompt(u, buf, w_pool, s_pool, nb, t, pos0):
    tc = 256
    nc = t // tc
    return pl.pallas_call(
        functools.partial(_pool_prompt_kernel, tc=tc, pos0=pos0),
        grid=(nb, nc),
        in_specs=[pl.BlockSpec((tc, POOL_WIDTH), lambda b, c: (b * nc + c, 0)),
                  pl.BlockSpec((None, POOL_BUF, POOL_WIDTH), lambda b, c: (b, 0, 0)),
                  pl.BlockSpec((len(POOL_WINDOWS), POOL_GROUP_DIM, POOL_GROUP_DIM), lambda b, c: (0, 0, 0)),
                  pl.BlockSpec((1, POOL_WIDTH), lambda b, c: (0, 0))],
        out_specs=pl.BlockSpec((tc, POOL_WIDTH), lambda b, c: (b * nc + c, 0)),
        out_shape=jax.ShapeDtypeStruct((nb * t, POOL_WIDTH), BF16),
        scratch_shapes=[pltpu.VMEM((16 + tc, POOL_WIDTH), F32)],
        compiler_params=_cparams("parallel", "arbitrary"),
    )(u, buf, w_pool, s_pool)


def _pool_step_kernel(u_ref, buf_ref, w_ref, s_ref, o_ref, *, pos0):
    outs = []
    for g, w in enumerate(POOL_WINDOWS):
        lo, hi = g * POOL_GROUP_DIM, (g + 1) * POOL_GROUP_DIM
        x = u_ref[:, lo:hi]
        wsum = x
        for j in range(1, w):
            wsum = wsum + buf_ref[POOL_BUF - j, :, lo:hi]
        cnt = min(pos0 + 1.0, float(w))
        d = (wsum / cnt - x).astype(BF16)
        outs.append(_dot(d, w_ref[g]))
    o_ref[...] = (jnp.concatenate(outs, axis=1) * s_ref[...]).astype(o_ref.dtype)


def _pool_step(u, buf_t, w_pool, s_pool, pos0):
    m = buf_t.shape[1]
    return pl.pallas_call(
        functools.partial(_pool_step_kernel, pos0=pos0),
        grid=(1,),
        in_specs=[pl.BlockSpec((m, POOL_WIDTH), lambda i: (0, 0)),
                  pl.BlockSpec((POOL_BUF, m, POOL_WIDTH), lambda i: (0, 0, 0)),
                  pl.BlockSpec((len(POOL_WINDOWS), POOL_GROUP_DIM, POOL_GROUP_DIM), lambda i: (0, 0, 0)),
                  pl.BlockSpec((1, POOL_WIDTH), lambda i: (0, 0))],
        out_specs=pl.BlockSpec((m, POOL_WIDTH), lambda i: (0, 0)),
        out_shape=jax.ShapeDtypeStruct((m, POOL_WIDTH), BF16),
        compiler_params=_cparams("arbitrary"),
    )(u, buf_t, w_pool, s_pool)


def _rope128(x, cc, ss):
    lane = lax.broadcasted_iota(jnp.int32, x.shape, 1)
    swapped = jnp.where(lane < MLA_ROPE // 2, pltpu.roll(x, LANES - MLA_ROPE // 2, 1),
                        pltpu.roll(x, MLA_ROPE // 2, 1))
    return x * cc + swapped * ss


def _mla_prep_kernel(uq_ref, ukv_ref, ukr_ref, gql_ref, gkvl_ref, wq_ref, wkv_ref, gq_ref, gk_ref,
                     cc_ref, ss_ref, *out_refs, emit_qg):
    if emit_qg:
        q_ref, k_ref, v_ref, lat_ref, qg_ref = out_refs
    else:
        q_ref, k_ref, v_ref, lat_ref = out_refs
    cc = cc_ref[...]
    ss = ss_ref[...]
    scale = MLA_QK ** -0.5

    uq = uq_ref[...]
    qn = uq * lax.rsqrt(jnp.mean(uq * uq, axis=-1, keepdims=True) + EPS) * gql_ref[...]
    qall = _dot(qn.astype(BF16), wq_ref[...])
    ukv = ukv_ref[...]
    lat = ukv * lax.rsqrt(jnp.mean(ukv * ukv, axis=-1, keepdims=True) + EPS) * gkvl_ref[...]
    lat_ref[...] = lat
    kvall = _dot(lat.astype(BF16), wkv_ref[...])
    kr = ukr_ref[...]
    kr_ss = jnp.sum(kr * kr, axis=-1, keepdims=True)

    gq_n, gq_r = gq_ref[:, :MLA_NOPE], gq_ref[:, MLA_NOPE:]
    gk_n, gk_r = gk_ref[:, :MLA_NOPE], gk_ref[:, MLA_NOPE:]
    nh = MLA_HEADS
    for h in range(nh):
        qnope = qall[:, h * 128:(h + 1) * 128]
        qrope = qall[:, (nh + h) * 128:(nh + h + 1) * 128]
        ssq = jnp.sum(qnope * qnope, axis=-1, keepdims=True) + jnp.sum(qrope * qrope, axis=-1, keepdims=True)
        rs = lax.rsqrt(ssq * (1.0 / MLA_QK) + EPS) * scale
        qn_h = qnope * rs * gq_n
        qr_h = _rope128(qrope * rs * gq_r, cc, ss)
        q_ref[h] = jnp.concatenate([qn_h, qr_h], axis=1).astype(q_ref.dtype)
        if emit_qg:
            qg_ref[h] = (qn_h * gk_n).astype(qg_ref.dtype)

        knope = kvall[:, h * 256:h * 256 + 128]
        ssk = jnp.sum(knope * knope, axis=-1, keepdims=True) + kr_ss
        rsk = lax.rsqrt(ssk * (1.0 / MLA_QK) + EPS)
        kr_h = _rope128(kr * rsk * gk_r, cc, ss)
        k_ref[h] = jnp.concatenate([knope * rsk * gk_n, kr_h], axis=1).astype(k_ref.dtype)
        v_ref[h] = kvall[:, h * 256 + 128:(h + 1) * 256].astype(v_ref.dtype)


def _mla_prep(u, m, tm, pos_of_tile, tabs, prm, emit_qg):
    cc_tab, ss_tab = tabs
    nh = MLA_HEADS
    out_shape = [jax.ShapeDtypeStruct((nh, m, MLA_QKP), BF16),
                 jax.ShapeDtypeStruct((nh, m, MLA_QKP), BF16),
                 jax.ShapeDtypeStruct((nh, m, MLA_V), BF16),
                 jax.ShapeDtypeStruct((m, MLA_KV_LORA), F32)]
    out_specs = [pl.BlockSpec((nh, tm, MLA_QKP), lambda i: (0, i, 0)),
                 pl.BlockSpec((nh, tm, MLA_QKP), lambda i: (0, i, 0)),
                 pl.BlockSpec((nh, tm, MLA_V), lambda i: (0, i, 0)),
                 pl.BlockSpec((tm, MLA_KV_LORA), lambda i: (i, 0))]
    if emit_qg:
        out_shape.append(jax.ShapeDtypeStruct((nh, m, MLA_NOPE), BF16))
        out_specs.append(pl.BlockSpec((nh, tm, MLA_NOPE), lambda i: (0, i, 0)))
    full = lambda a: pl.BlockSpec(a.shape, lambda i: (0,) * a.ndim)
    return pl.pallas_call(
        functools.partial(_mla_prep_kernel, emit_qg=emit_qg),
        grid=(m // tm,),
        in_specs=[pl.BlockSpec((tm, MLA_Q_LORA), lambda i: (i, POOL_WIDTH // MLA_Q_LORA)),
                  pl.BlockSpec((tm, MLA_KV_LORA), lambda i: (i, (POOL_WIDTH + MLA_Q_LORA) // MLA_KV_LORA)),
                  pl.BlockSpec((tm, LANES), lambda i: (i, (POOL_WIDTH + MLA_Q_LORA + MLA_KV_LORA) // LANES)),
                  full(prm["g_qlat"]), full(prm["g_kvlat"]), full(prm["w_qb"]), full(prm["w_kvb"]),
                  full(prm["g_q"]), full(prm["g_k"]),
                  pl.BlockSpec((tm, LANES), lambda i: (pos_of_tile(i), 0)),
                  pl.BlockSpec((tm, LANES), lambda i: (pos_of_tile(i), 0))],
        out_specs=out_specs,
        out_shape=out_shape,
        compiler_params=_cparams("parallel"),
    )(u, u, u, prm["g_qlat"], prm["g_kvlat"], prm["w_qb"], prm["w_kvb"], prm["g_q"], prm["g_k"],
      cc_tab, ss_tab)


def _flash_kernel(q_ref, k_ref, v_ref, o_ref, *, tq):
    i = pl.program_id(2)
    q = q_ref[...]

    def step(j, carry, masked):
        m, l, acc = carry
        start = pl.multiple_of(j * tq, tq)
        kb = k_ref[pl.ds(start, tq), :]
        vb = v_ref[pl.ds(start, tq), :]
        s = _dot_nt(q, kb)
        if masked:
            r = lax.broadcasted_iota(jnp.int32, (tq, tq), 0)
            c = lax.broadcasted_iota(jnp.int32, (tq, tq), 1)
            s = jnp.where(c <= r, s, -jnp.inf)
        m_new = jnp.maximum(m, jnp.max(s, axis=1, keepdims=True))
        alpha = jnp.exp(m - m_new)
        p = jnp.exp(s - m_new)
        l = alpha * l + jnp.sum(p, axis=1, keepdims=True)
        acc = alpha * acc + _dot(p.astype(BF16), vb)
        return m_new, l, acc

    init = (jnp.full((tq, 1), -jnp.inf, F32), jnp.zeros((tq, 1), F32), jnp.zeros((tq, MLA_V), F32))
    carry = lax.fori_loop(0, i, lambda j, c: step(j, c, False), init)
    _, l, acc = step(i, carry, True)
    o_ref[...] = (acc / l).astype(o_ref.dtype)


def _flash_attention(q, k, v, nb, t):
    tq = 512
    nq = t // tq
    nh = MLA_HEADS
    return pl.pallas_call(
        functools.partial(_flash_kernel, tq=tq),
        grid=(nb, nh, nq),
        in_specs=[pl.BlockSpec((None, tq, MLA_QKP), lambda b, h, i: (h, b * nq + i, 0)),
                  pl.BlockSpec((None, t, MLA_QKP), lambda b, h, i: (h, b, 0)),
                  pl.BlockSpec((None, t, MLA_V), lambda b, h, i: (h, b, 0))],
        out_specs=pl.BlockSpec((tq, MLA_V), lambda b, h, i: (b * nq + i, h)),
        out_shape=jax.ShapeDtypeStruct((nb * t, nh * MLA_V), BF16),
        compiler_params=_cparams("parallel", "parallel", "arbitrary"),
    )(q, k, v)


DEC_PAGES_PER_STEP = 16
DEC_SUB = 256


def _decode_kernel(pt_ref, *refs, nj):
    npg = DEC_PAGES_PER_STEP
    pages = refs[:npg]
    (wt_ref, qp_ref, qr_ref, gr_ref, cc_ref, ss_ref, qf_ref, kn_ref, latn_ref,
     o_ref, m_ref, l_ref, acc_ref, s_ref, latb_ref) = refs[npg:]
    j = pl.program_id(1)
    nh = MLA_HEADS

    @pl.when(j == 0)
    def _():
        m_ref[...] = jnp.full(m_ref.shape, -jnp.inf, F32)
        l_ref[...] = jnp.zeros(l_ref.shape, F32)
        acc_ref[...] = jnp.zeros(acc_ref.shape, F32)

    wt = wt_ref[...]
    qp = qp_ref[...]
    qr = qr_ref[...]
    gr = gr_ref[...]
    half = MLA_ROPE // 2
    for t in range(npg * PAGE_SIZE // DEC_SUB):
        r0 = t * DEC_SUB
        lat = jnp.concatenate([pages[2 * t][:, 0:MLA_KV_LORA], pages[2 * t + 1][:, 0:MLA_KV_LORA]], axis=0)
        kr = jnp.concatenate([pages[2 * t][:, MLA_KV_LORA:MLA_KV_LORA + MLA_ROPE],
                              pages[2 * t + 1][:, MLA_KV_LORA:MLA_KV_LORA + MLA_ROPE]], axis=0)
        latb = lat.astype(BF16)
        latb_ref[r0:r0 + DEC_SUB, :] = latb
        kt = _dot_nt(wt, latb)
        ssn = jnp.concatenate(
            [jnp.sum(jnp.square(kt[h * 128:(h + 1) * 128, :]), axis=0, keepdims=True) for h in range(nh)], axis=0)
        sn = _dot_nt(qp, latb)[:nh]
        krg = kr * gr
        swapped = jnp.concatenate([krg[:, half:], krg[:, :half]], axis=1)
        rot = krg * cc_ref[r0:r0 + DEC_SUB, :] + swapped * ss_ref[r0:r0 + DEC_SUB, :]
        feat = jnp.concatenate([rot, kr * kr], axis=1).astype(BF16)
        rr = _dot_nt(qr, feat)
        rs = lax.rsqrt((ssn + rr[nh:nh + 1]) * (1.0 / MLA_QK) + EPS)
        s_ref[:, r0:r0 + DEC_SUB] = rs * (sn + rr[:nh])

    s = s_ref[...]
    m_old = m_ref[...]
    m_new = jnp.maximum(m_old, jnp.max(s, axis=1, keepdims=True))
    alpha = jnp.exp(m_old - m_new)
    p = jnp.exp(s - m_new)
    l_new = alpha * l_ref[...] + jnp.sum(p, axis=1, keepdims=True)
    acc_new = alpha * acc_ref[...] + _dot(p.astype(BF16), latb_ref[...])
    m_ref[...] = m_new
    l_ref[...] = l_new
    acc_ref[...] = acc_new

    @pl.when(j == nj - 1)
    def _():
        s_new = jnp.sum(qf_ref[...] * kn_ref[...], axis=1, keepdims=True)
        m_fin = jnp.maximum(m_new, s_new)
        a_fin = jnp.exp(m_new - m_fin)
        p_new = jnp.exp(s_new - m_fin)
        l_fin = a_fin * l_new + p_new
        lat_new = latn_ref[...].astype(BF16).astype(F32)
        acc_fin = a_fin * acc_new + p_new.astype(BF16).astype(F32) * lat_new
        o_ref[...] = acc_fin / l_fin


def _mla_decode(cache, layer, page_table, wt_nope, qp, qr, g_rope, cc_keys, ss_keys, qf, kn, lat_new):
    nsamp, n_pages = page_table.shape
    npg = DEC_PAGES_PER_STEP
    nj = n_pages // npg
    assert n_pages % npg == 0
    step_keys = npg * PAGE_SIZE
    kvd = cache.shape[-1]
    page_specs = [pl.BlockSpec((None, None, PAGE_SIZE, kvd),
                               lambda b, j, pt, i=i: (layer, pt[b, j * npg + i], 0, 0)) for i in range(npg)]
    nh = MLA_HEADS
    grid_spec = pltpu.PrefetchScalarGridSpec(
        num_scalar_prefetch=1,
        grid=(nsamp, nj),
        in_specs=page_specs + [
            pl.BlockSpec(wt_nope.shape, lambda b, j, pt: (0, 0)),
            pl.BlockSpec((None, 16, MLA_KV_LORA), lambda b, j, pt: (b, 0, 0)),
            pl.BlockSpec((None, 16, LANES), lambda b, j, pt: (b, 0, 0)),
            pl.BlockSpec((1, MLA_ROPE), lambda b, j, pt: (0, 0)),
            pl.BlockSpec((step_keys, MLA_ROPE), lambda b, j, pt: (j, 0)),
            pl.BlockSpec((step_keys, MLA_ROPE), lambda b, j, pt: (j, 0)),
            pl.BlockSpec((None, nh, MLA_QKP), lambda b, j, pt: (b, 0, 0)),
            pl.BlockSpec((None, nh, MLA_QKP), lambda b, j, pt: (b, 0, 0)),
            pl.BlockSpec((None, 1, MLA_KV_LORA), lambda b, j, pt: (b, 0, 0)),
        ],
        out_specs=pl.BlockSpec((None, nh, MLA_KV_LORA), lambda b, j, pt: (b, 0, 0)),
        scratch_shapes=[pltpu.VMEM((nh, 1), F32), pltpu.VMEM((nh, 1), F32),
                        pltpu.VMEM((nh, MLA_KV_LORA), F32),
                        pltpu.VMEM((nh, step_keys), F32),
                        pltpu.VMEM((step_keys, MLA_KV_LORA), BF16)],
    )
    return pl.pallas_call(
        functools.partial(_decode_kernel, nj=nj),
        grid_spec=grid_spec,
        out_shape=jax.ShapeDtypeStruct((nsamp, nh, MLA_KV_LORA), F32),
        compiler_params=_cparams("parallel", "arbitrary"),
    )(page_table, *([cache] * npg), wt_nope, qp, qr, g_rope, cc_keys, ss_keys, qf, kn, lat_new)


def _ssd_gate_norm(y, xs, z, dskip, gnorm):
    y = (y + dskip * xs) * _silu(z)
    gw = SSD_D_INNER // SSD_GROUPS
    outs = []
    for g in range(SSD_GROUPS):
        yg = y[:, g * gw:(g + 1) * gw]
        outs.append(yg * lax.rsqrt(jnp.mean(yg * yg, axis=-1, keepdims=True) + EPS))
    return jnp.concatenate(outs, axis=1) * gnorm


def _ssd_prompt_kernel(z_ref, x_ref, bc_ref, dt_ref, bufx_ref, bufbc_ref, wx_ref, bx_ref, wbc_ref, bbc_ref,
                       dtb_ref, alog_ref, dskip_ref, gnorm_ref, h0_ref, tri_ref, exp_ref,
                       y_ref, hout_ref, extx_ref, extbc_ref, ht_ref, *, nc):
    c = pl.program_id(1)
    first = c == 0
    l = SSD_CHUNK
    xs = _silu(_conv_chunk(extx_ref, x_ref[...], bufx_ref, wx_ref, bx_ref, first))
    bcs = _silu(_conv_chunk(extbc_ref, bc_ref[...], bufbc_ref, wbc_ref, bbc_ref, first))
    gs = SSD_GROUPS * SSD_STATE
    bm, cm = bcs[:, :gs], bcs[:, gs:]

    @pl.when(first)
    def _():
        for r in range(SSD_D_INNER // LANES):
            ht_ref[:, r * LANES:(r + 1) * LANES] = h0_ref[r * LANES:(r + 1) * LANES, :].T

    dt = _softplus(dt_ref[...] + dtb_ref[...])
    a = -jnp.exp(alog_ref[...])
    acum = _dot_sel_left(tri_ref[...], dt * a)
    acum_t = acum.T
    dt_t = dt.T
    a_last = acum[l - 1:l, :]
    dend = jnp.exp(a_last - acum)
    sel = exp_ref[...]
    w_state = _dot_sel(dt * dend, sel)
    e_acc = _dot_sel(jnp.exp(acum), sel)
    e_last = _dot_sel(jnp.exp(a_last), sel)

    rows = lax.broadcasted_iota(jnp.int32, (l, l), 0)
    cols = lax.broadcasted_iota(jnp.int32, (l, l), 1)
    causal = cols <= rows
    lane = lax.broadcasted_iota(jnp.int32, (l, LANES), 1)
    xs_b = xs.astype(BF16)
    hpg = SSD_HEADS // SSD_GROUPS
    gw = SSD_D_INNER // SSD_GROUPS
    y_groups = []
    for g in range(SSD_GROUPS):
        gcols = slice(g * gw, (g + 1) * gw)
        bg = bm[:, g * SSD_STATE:(g + 1) * SSD_STATE]
        cg_b = cm[:, g * SSD_STATE:(g + 1) * SSD_STATE].astype(BF16)
        cb = _dot_nt(cg_b, bg.astype(BF16))
        ht_g = ht_ref[:, gcols]
        y_off = _dot(cg_b, ht_g.astype(BF16))
        y_pairs = []
        for pr in range(hpg // 2):
            col = (g * hpg + pr * 2) * SSD_HEAD_DIM
            res = []
            for hh in range(2):
                h = g * hpg + pr * 2 + hh
                diff = acum[:, h:h + 1] - acum_t[h:h + 1, :]
                lmat = jnp.exp(jnp.where(causal, diff, -jnp.inf)) * dt_t[h:h + 1, :]
                res.append(_dot((cb * lmat).astype(BF16), xs_b[:, col:col + LANES]))
            y_pairs.append(jnp.where(lane < SSD_HEAD_DIM, res[0], res[1]))
        y_groups.append(jnp.concatenate(y_pairs, axis=1) + e_acc[:, gcols] * y_off)
        xw = (xs[:, gcols] * w_state[:, gcols]).astype(BF16)
        ht_ref[:, gcols] = e_last[:, gcols] * ht_g + _dot(bg.T.astype(BF16), xw)
    y = jnp.concatenate(y_groups, axis=1)
    y_ref[...] = _ssd_gate_norm(y, xs, z_ref[...], dskip_ref[...], gnorm_ref[...]).astype(y_ref.dtype)

    @pl.when(c == nc - 1)
    def _():
        for r in range(SSD_D_INNER // LANES):
            hout_ref[r * LANES:(r + 1) * LANES, :] = ht_ref[:, r * LANES:(r + 1) * LANES].T


def _dot_sel_left(sel, x):
    hi, mid, lo = _split3(x)
    return _dot(sel, hi) + _dot(sel, mid) + _dot(sel, lo)


def _ssd_consts():
    l = SSD_CHUNK
    tri = (jnp.arange(l)[:, None] >= jnp.arange(l)[None, :]).astype(BF16)
    sel = (jnp.arange(LANES)[:, None] == (jnp.arange(SSD_D_INNER)[None, :] // SSD_HEAD_DIM)).astype(BF16)
    return tri, sel


def _ssd_prompt(u, u_dt, bufx, bufbc, h0, prm, nb, t):
    l = SSD_CHUNK
    nc = t // l
    tri, sel = _ssd_consts()
    full = lambda a: pl.BlockSpec(a.shape, lambda b, c: (0,) * a.ndim)
    row = lambda width, colblk: pl.BlockSpec((l, width), lambda b, c: (b * nc + c, colblk))
    y, hout = pl.pallas_call(
        functools.partial(_ssd_prompt_kernel, nc=nc),
        grid=(nb, nc),
        in_specs=[row(SSD_D_INNER, 0), row(SSD_D_INNER, 1), row(SSD_BC, 4), row(LANES, 0),
                  pl.BlockSpec((None, CONV_WIDTH - 1, SSD_D_INNER), lambda b, c: (b, 0, 0)),
                  pl.BlockSpec((None, CONV_WIDTH - 1, SSD_BC), lambda b, c: (b, 0, 0)),
                  full(prm["w_conv_x"]), full(prm["b_conv_x"]), full(prm["w_conv_bc"]), full(prm["b_conv_bc"]),
                  full(prm["dt_bias"]), full(prm["a_log"]), full(prm["d_skip"]), full(prm["g_ssd_norm"]),
                  pl.BlockSpec((None, SSD_D_INNER, SSD_STATE), lambda b, c: (b, 0, 0)),
                  full(tri), full(sel)],
        out_specs=[pl.BlockSpec((l, SSD_D_INNER), lambda b, c: (b * nc + c, 0)),
                   pl.BlockSpec((None, SSD_D_INNER, SSD_STATE), lambda b, c: (b, 0, 0))],
        out_shape=[jax.ShapeDtypeStruct((nb * t, SSD_D_INNER), BF16),
                   jax.ShapeDtypeStruct((nb, SSD_D_INNER, SSD_STATE), F32)],
        scratch_shapes=[pltpu.VMEM((CONV_HALO + l, SSD_D_INNER), F32),
                        pltpu.VMEM((CONV_HALO + l, SSD_BC), F32),
                        pltpu.VMEM((SSD_STATE, SSD_D_INNER), F32)],
        compiler_params=_cparams("parallel", "arbitrary"),
    )(u, u, u, u_dt, bufx, bufbc, prm["w_conv_x"], prm["b_conv_x"], prm["w_conv_bc"], prm["b_conv_bc"],
      prm["dt_bias"], prm["a_log"], prm["d_skip"], prm["g_ssd_norm"], h0, tri, sel)
    return y, hout


def _ssd_step_prep_kernel(x_ref, bc_ref, dt_ref, bufx_ref, bufbc_ref, wx_ref, bx_ref, wbc_ref, bbc_ref,
                          dtb_ref, alog_ref, exp_ref, xs_ref, bcs_ref, dtx_ref, dae_ref):
    xs = _silu(_conv_step(bufx_ref, x_ref[...], wx_ref, bx_ref))
    xs_ref[...] = xs
    bcs_ref[...] = _silu(_conv_step(bufbc_ref, bc_ref[...], wbc_ref, bbc_ref))
    dt = _softplus(dt_ref[...] + dtb_ref[...])
    da = jnp.exp(dt * (-jnp.exp(alog_ref[...])))
    sel = exp_ref[...]
    dtx_ref[...] = _dot_sel(dt, sel) * xs
    dae_ref[...] = _dot_sel(da, sel)


def _ssd_step_prep(u, u_dt, bufx_t, bufbc_t, prm):
    m = u_dt.shape[0]
    _, sel = _ssd_consts()
    full = lambda a: pl.BlockSpec(a.shape, lambda i: (0,) * a.ndim)
    return pl.pallas_call(
        _ssd_step_prep_kernel,
        grid=(1,),
        in_specs=[pl.BlockSpec((m, SSD_D_INNER), lambda i: (0, 1)),
                  pl.BlockSpec((m, SSD_BC), lambda i: (0, 4)),
                  full(u_dt), full(bufx_t), full(bufbc_t),
                  full(prm["w_conv_x"]), full(prm["b_conv_x"]), full(prm["w_conv_bc"]), full(prm["b_conv_bc"]),
                  full(prm["dt_bias"]), full(prm["a_log"]), full(sel)],
        out_specs=[pl.BlockSpec((m, SSD_D_INNER), lambda i: (0, 0)),
                   pl.BlockSpec((m, SSD_BC), lambda i: (0, 0)),
                   pl.BlockSpec((m, SSD_D_INNER), lambda i: (0, 0)),
                   pl.BlockSpec((m, SSD_D_INNER), lambda i: (0, 0))],
        out_shape=[jax.ShapeDtypeStruct((m, SSD_D_INNER), F32),
                   jax.ShapeDtypeStruct((m, SSD_BC), F32),
                   jax.ShapeDtypeStruct((m, SSD_D_INNER), F32),
                   jax.ShapeDtypeStruct((m, SSD_D_INNER), F32)],
        compiler_params=_cparams("arbitrary"),
    )(u, u, u_dt, bufx_t, bufbc_t, prm["w_conv_x"], prm["b_conv_x"], prm["w_conv_bc"], prm["b_conv_bc"],
      prm["dt_bias"], prm["a_log"], sel)


def _ssd_step_state_kernel(h0_ref, cols_ref, b_ref, c_ref, hn_ref, y_ref):
    nr = SSD_D_INNER // LANES
    pad = jnp.concatenate([cols_ref[...], jnp.zeros((LANES - 2 * nr, LANES), F32)], axis=0)
    ct = pad.T
    rows_per_group = SSD_D_INNER // SSD_GROUPS // LANES
    for r in range(nr):
        g = r // rows_per_group
        hn_ref[r * LANES:(r + 1) * LANES, :] = (ct[:, nr + r:nr + r + 1] * h0_ref[r * LANES:(r + 1) * LANES, :]
                                                + ct[:, r:r + 1] * b_ref[g:g + 1, :])
    res = _dot_nt(c_ref[...].astype(BF16), hn_ref[...].astype(BF16))
    gw = SSD_D_INNER // SSD_GROUPS
    y_ref[...] = jnp.concatenate([res[g:g + 1, g * gw:(g + 1) * gw] for g in range(SSD_GROUPS)], axis=1)


def _ssd_step_state(h0, cols, bmat, cmat):
    m = h0.shape[0]
    per = lambda shape: pl.BlockSpec((None,) + shape, lambda i: (i, 0, 0))
    return pl.pallas_call(
        _ssd_step_state_kernel,
        grid=(m,),
        in_specs=[per((SSD_D_INNER, SSD_STATE)), per((2 * SSD_D_INNER // LANES, LANES)),
                  per((8, SSD_STATE)), per((8, SSD_STATE))],
        out_specs=[per((SSD_D_INNER, SSD_STATE)), per((1, SSD_D_INNER))],
        out_shape=[jax.ShapeDtypeStruct((m, SSD_D_INNER, SSD_STATE), F32),
                   jax.ShapeDtypeStruct((m, 1, SSD_D_INNER), F32)],
        compiler_params=_cparams("parallel"),
    )(h0, cols, bmat, cmat)


def _ssd_step_post_kernel(y_ref, xs_ref, z_ref, dskip_ref, gnorm_ref, o_ref):
    o_ref[...] = _ssd_gate_norm(y_ref[...], xs_ref[...], z_ref[...], dskip_ref[...],
                                gnorm_ref[...]).astype(o_ref.dtype)


def _ssd_step_post(y, xs, u, prm):
    m = y.shape[0]
    full = lambda a: pl.BlockSpec(a.shape, lambda i: (0,) * a.ndim)
    return pl.pallas_call(
        _ssd_step_post_kernel,
        grid=(1,),
        in_specs=[full(y), full(xs), pl.BlockSpec((m, SSD_D_INNER), lambda i: (0, 0)),
                  full(prm["d_skip"]), full(prm["g_ssd_norm"])],
        out_specs=pl.BlockSpec((m, SSD_D_INNER), lambda i: (0, 0)),
        out_shape=jax.ShapeDtypeStruct((m, SSD_D_INNER), BF16),
        compiler_params=_cparams("arbitrary"),
    )(y, xs, u, prm["d_skip"], prm["g_ssd_norm"])


def _lru_gates(xc, wr_ref, br_ref, wi_ref, bi_ref, lam_ref, is_pos0):
    xb = xc.astype(BF16)
    rs, is_ = [], []
    for n in range(LRU_BLOCKS):
        blk = xb[:, n * LRU_BLOCK_DIM:(n + 1) * LRU_BLOCK_DIM]
        rs.append(_dot(blk, wr_ref[n]))
        is_.append(_dot(blk, wi_ref[n]))
    r = _sigmoid(jnp.concatenate(rs, axis=1) + br_ref[...])
    i = _sigmoid(jnp.concatenate(is_, axis=1) + bi_ref[...])
    log_a = -LRU_C * r * _softplus(-lam_ref[...])
    a = jnp.exp(log_a)
    th = jnp.tanh(log_a)
    mult = jnp.sqrt(-2.0 * th / (1.0 - th))
    if is_pos0 is not False:
        mult = jnp.where(is_pos0, 1.0, mult)
    return a, mult * i * xc


def _shift_rows(x, d, fill):
    rolled = pltpu.roll(x, d, 0)
    row = lax.broadcasted_iota(jnp.int32, x.shape, 0)
    return jnp.where(row >= d, rolled, fill)


def _lru_prompt_kernel(lx_ref, lg_ref, buf_ref, wc_ref, bc_ref, wr_ref, br_ref, wi_ref, bi_ref, lam_ref, h0_ref,
                       y_ref, hout_ref, ext_ref, h_ref, *, tc, nc, pos0):
    c = pl.program_id(1)
    first = c == 0

    @pl.when(first)
    def _():
        h_ref[...] = h0_ref[...]

    xc = _conv_chunk(ext_ref, lx_ref[...], buf_ref, wc_ref, bc_ref, first)
    pos = pos0 + c * tc + lax.broadcasted_iota(jnp.int32, (tc, 1), 0)
    a, b = _lru_gates(xc, wr_ref, br_ref, wi_ref, bi_ref, lam_ref, pos == 0)
    d = 1
    while d < tc:
        b = b + a * _shift_rows(b, d, 0.0)
        a = a * _shift_rows(a, d, 1.0)
        d *= 2
    h = b + a * h_ref[...]
    h_ref[...] = h[tc - 1:tc, :]
    y_ref[...] = (h * _gelu_tanh(lg_ref[...])).astype(y_ref.dtype)

    @pl.when(c == nc - 1)
    def _():
        hout_ref[...] = h[tc - 1:tc, :]


def _lru_prompt(u, buf, h0, prm, nb, t, pos0):
    tc = 256
    nc = t // tc
    full = lambda a: pl.BlockSpec(a.shape, lambda b, c: (0,) * a.ndim)
    y, hout = pl.pallas_call(
        functools.partial(_lru_prompt_kernel, tc=tc, nc=nc, pos0=pos0),
        grid=(nb, nc),
        in_specs=[pl.BlockSpec((tc, LRU_WIDTH), lambda b, c: (b * nc + c, 5)),
                  pl.BlockSpec((tc, LRU_WIDTH), lambda b, c: (b * nc + c, 6)),
                  pl.BlockSpec((None, CONV_WIDTH - 1, LRU_WIDTH), lambda b, c: (b, 0, 0)),
                  full(prm["w_conv_lru"]), full(prm["b_conv_lru"]), full(prm["w_lru_r"]), full(prm["b_lru_r"]),
                  full(prm["w_lru_i"]), full(prm["b_lru_i"]), full(prm["lru_lambda"]),
                  pl.BlockSpec((None, 1, LRU_WIDTH), lambda b, c: (b, 0, 0))],
        out_specs=[pl.BlockSpec((tc, LRU_WIDTH), lambda b, c: (b * nc + c, 0)),
                   pl.BlockSpec((None, 1, LRU_WIDTH), lambda b, c: (b, 0, 0))],
        out_shape=[jax.ShapeDtypeStruct((nb * t, LRU_WIDTH), BF16),
                   jax.ShapeDtypeStruct((nb, 1, LRU_WIDTH), F32)],
        scratch_shapes=[pltpu.VMEM((CONV_HALO + tc, LRU_WIDTH), F32), pltpu.VMEM((1, LRU_WIDTH), F32)],
        compiler_params=_cparams("parallel", "arbitrary"),
    )(u, u, buf, prm["w_conv_lru"], prm["b_conv_lru"], prm["w_lru_r"], prm["b_lru_r"], prm["w_lru_i"],
      prm["b_lru_i"], prm["lru_lambda"], h0)
    return y, hout


def _lru_step_kernel(lx_ref, lg_ref, buf_ref, wc_ref, bc_ref, wr_ref, br_ref, wi_ref, bi_ref, lam_ref, h0_ref,
                     y_ref, hout_ref, *, pos0):
    xc = _conv_step(buf_ref, lx_ref[...], wc_ref, bc_ref)
    a, b = _lru_gates(xc, wr_ref, br_ref, wi_ref, bi_ref, lam_ref, pos0 == 0)
    h = b + a * h0_ref[...]
    hout_ref[...] = h
    y_ref[...] = (h * _gelu_tanh(lg_ref[...])).astype(y_ref.dtype)


def _lru_step(u, buf_t, h0, prm, pos0):
    m = h0.shape[0]
    full = lambda a: pl.BlockSpec(a.shape, lambda i: (0,) * a.ndim)
    return pl.pallas_call(
        functools.partial(_lru_step_kernel, pos0=pos0),
        grid=(1,),
        in_specs=[pl.BlockSpec((m, LRU_WIDTH), lambda i: (0, 5)),
                  pl.BlockSpec((m, LRU_WIDTH), lambda i: (0, 6)),
                  full(buf_t), full(prm["w_conv_lru"]), full(prm["b_conv_lru"]), full(prm["w_lru_r"]),
                  full(prm["b_lru_r"]), full(prm["w_lru_i"]), full(prm["b_lru_i"]), full(prm["lru_lambda"]),
                  full(h0)],
        out_specs=[pl.BlockSpec((m, LRU_WIDTH), lambda i: (0, 0)),
                   pl.BlockSpec((m, LRU_WIDTH), lambda i: (0, 0))],
        out_shape=[jax.ShapeDtypeStruct((m, LRU_WIDTH), BF16),
                   jax.ShapeDtypeStruct((m, LRU_WIDTH), F32)],
        compiler_params=_cparams("arbitrary"),
    )(u, u, buf_t, prm["w_conv_lru"], prm["b_conv_lru"], prm["w_lru_r"], prm["b_lru_r"], prm["w_lru_i"],
      prm["b_lru_i"], prm["lru_lambda"], h0)


def _rope_tables(n_pos):
    half = MLA_ROPE // 2
    inv = ROPE_THETA ** (-jnp.arange(half, dtype=F32) * (2.0 / MLA_ROPE))
    ang = jnp.arange(n_pos, dtype=F32)[:, None] * inv[None, :]
    cos, sin = jnp.cos(ang), jnp.sin(ang)
    cc = jnp.concatenate([cos, cos], axis=1)
    ss = jnp.concatenate([-sin, sin], axis=1)
    return cc, ss


def _even_params(e, w_in_e, w_pool, s_pool, g_qlat, w_qb, g_kvlat, w_kvb, g_q, g_k, w_out_e):
    nh = MLA_HEADS
    w_in = jnp.pad(w_in_e[e], ((0, 0), (0, EVEN_IN_PAD - w_in_e.shape[2]))).astype(BF16)
    wq = w_qb[e].reshape(MLA_Q_LORA, nh, MLA_QK)
    wq_nope = wq[:, :, :MLA_NOPE].reshape(MLA_Q_LORA, nh * MLA_NOPE)
    wq_rope = jnp.pad(wq[:, :, MLA_NOPE:], ((0, 0), (0, 0), (0, LANES - MLA_ROPE))).reshape(MLA_Q_LORA, nh * LANES)
    wkv = w_kvb[e].reshape(MLA_KV_LORA, nh, MLA_NOPE + MLA_V)
    w_nope_t = jnp.transpose(wkv[:, :, :MLA_NOPE], (1, 2, 0))
    pad_g = lambda g: jnp.pad(g, (0, MLA_QKP - MLA_QK)).reshape(1, MLA_QKP)
    return dict(
        w_in=w_in,
        w_pool=w_pool[e].astype(BF16),
        s_pool=s_pool[e].reshape(1, POOL_WIDTH),
        g_qlat=g_qlat[e].reshape(1, MLA_Q_LORA),
        g_kvlat=g_kvlat[e].reshape(1, MLA_KV_LORA),
        w_qb=jnp.concatenate([wq_nope, wq_rope], axis=1).astype(BF16),
        w_kvb=w_kvb[e].astype(BF16),
        g_q=pad_g(g_q[e]), g_k=pad_g(g_k[e]),
        g_k_rope=g_k[e][MLA_NOPE:].reshape(1, MLA_ROPE),
        w_nope_t=w_nope_t.astype(BF16),
        w_v=jnp.transpose(wkv[:, :, MLA_NOPE:], (1, 0, 2)).astype(BF16),
        w_out_pool=w_out_e[e][:POOL_WIDTH].astype(BF16),
        w_out_att=w_out_e[e][POOL_WIDTH:].astype(BF16),
    )


def _odd_params(o, w_in_o, w_conv_ssd, b_conv_ssd, dt_bias, a_log, d_skip, g_ssd_norm, w_conv_lru, b_conv_lru,
                w_lru_r, b_lru_r, w_lru_i, b_lru_i, lru_lambda, w_out_o):
    j2 = SSD_D_INNER + SSD_D_INNER + SSD_BC
    j3 = j2 + SSD_HEADS
    w = w_in_o[o]
    pad_heads = lambda v: jnp.pad(v, (0, LANES - SSD_HEADS)).reshape(1, LANES)
    return dict(
        w_in=jnp.concatenate([w[:, :j2], w[:, j3:]], axis=1).astype(BF16),
        w_in_dt=jnp.pad(w[:, j2:j3], ((0, 0), (0, LANES - SSD_HEADS))).astype(BF16),
        w_conv_x=w_conv_ssd[o][:, :SSD_D_INNER], w_conv_bc=w_conv_ssd[o][:, SSD_D_INNER:],
        b_conv_x=b_conv_ssd[o][:SSD_D_INNER].reshape(1, -1), b_conv_bc=b_conv_ssd[o][SSD_D_INNER:].reshape(1, -1),
        dt_bias=pad_heads(dt_bias[o]), a_log=pad_heads(a_log[o]),
        d_skip=jnp.repeat(d_skip[o], SSD_HEAD_DIM).reshape(1, SSD_D_INNER),
        g_ssd_norm=g_ssd_norm[o].reshape(1, SSD_D_INNER),
        w_conv_lru=w_conv_lru[o], b_conv_lru=b_conv_lru[o].reshape(1, LRU_WIDTH),
        w_lru_r=w_lru_r[o].astype(BF16), b_lru_r=b_lru_r[o].reshape(1, LRU_WIDTH),
        w_lru_i=w_lru_i[o].astype(BF16), b_lru_i=b_lru_i[o].reshape(1, LRU_WIDTH),
        lru_lambda=lru_lambda[o].reshape(1, LRU_WIDTH),
        w_out_ssd=w_out_o[o][:SSD_D_INNER].astype(BF16),
        w_out_lru=w_out_o[o][SSD_D_INNER:].astype(BF16),
    )


def _pad_tab(tab):
    return jnp.pad(tab, ((0, 0), (0, LANES - tab.shape[1])))


def _even_layer_prompt(rows, hn, prm, nb, t):
    u = _matmul(rows, [hn], [prm["w_in"]], tn=640)
    zero_buf = jnp.zeros((nb, POOL_BUF, POOL_WIDTH), F32)
    y_pool = _pool_prompt(u, zero_buf, prm["w_pool"], prm["s_pool"], nb, t, 0)
    cc, ss = _rope_tables(t)
    tm = 256
    tiles = t // tm
    q, k, v, lat = _mla_prep(u, rows.m, tm, lambda i: i % tiles, (_pad_tab(cc), _pad_tab(ss)), prm, False)
    y_att = _flash_attention(q, k, v, nb, t)
    u3 = u.reshape(nb, t, EVEN_IN_PAD)
    kr = u3[:, :, POOL_WIDTH + MLA_Q_LORA + MLA_KV_LORA:POOL_WIDTH + MLA_Q_LORA + MLA_KV_LORA + MLA_ROPE]
    mla_rows = jnp.concatenate([lat.reshape(nb, t, MLA_KV_LORA), kr], axis=-1)
    pool_new = u3[:, t - POOL_BUF:, :POOL_WIDTH]
    return [y_pool, y_att], [prm["w_out_pool"], prm["w_out_att"]], mla_rows, pool_new


def _even_layer_sample(rows, hn, prm, pool_buf, cache, layer, page_table, pos0):
    m = rows.m
    nh = MLA_HEADS
    u = _matmul(rows, [hn], [prm["w_in"]], tn=640)
    y_pool = _pool_step(u, jnp.transpose(pool_buf, (1, 0, 2)), prm["w_pool"], prm["s_pool"], pos0)
    cc, ss = _rope_tables(pos0 + 1)
    cc_new = jnp.broadcast_to(_pad_tab(cc[pos0:]), (m, LANES))
    ss_new = jnp.broadcast_to(_pad_tab(ss[pos0:]), (m, LANES))
    q, k, v, lat, qg = _mla_prep(u, m, m, lambda i: 0, (cc_new, ss_new), prm, True)
    qp = _bmm_heads(qg, prm["w_nope_t"], BF16)
    qp = jnp.pad(jnp.transpose(qp, (1, 0, 2)), ((0, 0), (0, 16 - nh), (0, 0)))
    q_m = jnp.transpose(q, (1, 0, 2))
    q_rope = q_m[:, :, MLA_NOPE:MLA_NOPE + MLA_ROPE]
    qr = jnp.concatenate([q_rope, jnp.zeros((m, nh, LANES - MLA_ROPE), BF16)], axis=2)
    ones_row = jnp.concatenate([jnp.zeros((m, 1, MLA_ROPE), BF16), jnp.ones((m, 1, LANES - MLA_ROPE), BF16)], axis=2)
    qr = jnp.concatenate([qr, ones_row, jnp.zeros((m, 16 - nh - 1, LANES), BF16)], axis=1)
    o_lat = _mla_decode(cache, layer, page_table, prm["w_nope_t"].reshape(nh * MLA_NOPE, MLA_KV_LORA), qp, qr,
                        prm["g_k_rope"], cc[:pos0], ss[:pos0], q_m.astype(F32),
                        jnp.transpose(k, (1, 0, 2)).astype(F32), lat.reshape(m, 1, MLA_KV_LORA))
    y_att = _bmm_heads(jnp.transpose(o_lat, (1, 0, 2)).astype(BF16), prm["w_v"], BF16)
    y_att = jnp.transpose(y_att, (1, 0, 2)).reshape(m, nh * MLA_V)
    kr = u[:, POOL_WIDTH + MLA_Q_LORA + MLA_KV_LORA:POOL_WIDTH + MLA_Q_LORA + MLA_KV_LORA + MLA_ROPE]
    mla_rows = jnp.concatenate([lat, kr], axis=-1).reshape(m, 1, MLA_KV_LORA + MLA_ROPE)
    pool_new = jnp.concatenate([pool_buf[:, 1:], u[:, None, :POOL_WIDTH]], axis=1)
    return [y_pool, y_att], [prm["w_out_pool"], prm["w_out_att"]], mla_rows, pool_new


def _odd_layer_prompt(rows, hn, prm, nb, t):
    u = _matmul(rows, [hn], [prm["w_in"]], tn=1024)
    u_dt = _matmul(rows, [hn], [prm["w_in_dt"]], tn=LANES)
    k1 = CONV_WIDTH - 1
    y_ssd, h_ssd = _ssd_prompt(u, u_dt, jnp.zeros((nb, k1, SSD_D_INNER), F32), jnp.zeros((nb, k1, SSD_BC), F32),
                               jnp.zeros((nb, SSD_D_INNER, SSD_STATE), F32), prm, nb, t)
    y_lru, h_lru = _lru_prompt(u, jnp.zeros((nb, k1, LRU_WIDTH), F32), jnp.zeros((nb, 1, LRU_WIDTH), F32),
                               prm, nb, t, 0)
    u3 = u.reshape(nb, t, -1)
    sconv = u3[:, t - k1:, SSD_D_INNER:2 * SSD_D_INNER + SSD_BC]
    lconv = u3[:, t - k1:, 2 * SSD_D_INNER + SSD_BC:2 * SSD_D_INNER + SSD_BC + LRU_WIDTH]
    return ([y_ssd, y_lru], [prm["w_out_ssd"], prm["w_out_lru"]], sconv,
            h_ssd.reshape(nb, SSD_HEADS, SSD_HEAD_DIM, SSD_STATE), lconv, h_lru.reshape(nb, LRU_WIDTH))


def _odd_layer_sample(rows, hn, prm, sconv_buf, ssd_state, lconv_buf, lru_state, pos0):
    m = rows.m
    u = _matmul(rows, [hn], [prm["w_in"]], tn=1024)
    u_dt = _matmul(rows, [hn], [prm["w_in_dt"]], tn=LANES)
    sconv_t = jnp.transpose(sconv_buf, (1, 0, 2))
    xs, bcs, dtx, dae = _ssd_step_prep(u, u_dt, sconv_t[:, :, :SSD_D_INNER], sconv_t[:, :, SSD_D_INNER:], prm)
    nr = SSD_D_INNER // LANES
    cols = jnp.concatenate([dtx.reshape(m, nr, LANES), dae.reshape(m, nr, LANES)], axis=1)
    gs = SSD_GROUPS * SSD_STATE
    pad8 = lambda a: jnp.pad(a.reshape(m, SSD_GROUPS, SSD_STATE), ((0, 0), (0, 8 - SSD_GROUPS), (0, 0)))
    h_new, y = _ssd_step_state(ssd_state.reshape(m, SSD_D_INNER, SSD_STATE), cols,
                               pad8(bcs[:, :gs]), pad8(bcs[:, gs:]))
    y_ssd = _ssd_step_post(y.reshape(m, SSD_D_INNER), xs, u, prm)
    y_lru, h_lru = _lru_step(u, jnp.transpose(lconv_buf, (1, 0, 2)), lru_state, prm, pos0)
    u_xbc = u[:, SSD_D_INNER:2 * SSD_D_INNER + SSD_BC]
    u_lx = u[:, 2 * SSD_D_INNER + SSD_BC:2 * SSD_D_INNER + SSD_BC + LRU_WIDTH]
    sconv = jnp.concatenate([sconv_buf[:, 1:], u_xbc[:, None]], axis=1)
    lconv = jnp.concatenate([lconv_buf[:, 1:], u_lx[:, None]], axis=1)
    return ([y_ssd, y_lru], [prm["w_out_ssd"], prm["w_out_lru"]], sconv,
            h_new.reshape(m, SSD_HEADS, SSD_HEAD_DIM, SSD_STATE), lconv, h_lru)


def kernel(x_prompt, x_sample, cache_mla, state_pool, state_ssd_conv, state_ssd, state_lru_conv, state_lru, page_table, c_prompt, c_sample, g_norm1, g_norm2, w_mod, b_mod, w_mlp1, w_mlp2, w_in_e, w_pool, s_pool, g_qlat, w_qb, g_kvlat, w_kvb, g_q, g_k, w_out_e, w_in_o, w_conv_ssd, b_conv_ssd, dt_bias, a_log, d_skip, g_ssd_norm, w_conv_lru, b_conv_lru, w_lru_r, b_lru_r, w_lru_i, b_lru_i, lru_lambda, w_out_o):
    nb, t, _ = x_prompt.shape
    ns = x_sample.shape[0]
    assert x_sample.shape[1] == 1 and t >= POOL_BUF
    pos0_s = page_table.shape[1] * PAGE_SIZE

    pad_rows = (-(ns + nb)) % 8
    c_all = jnp.concatenate([c_sample, c_prompt, jnp.zeros((pad_rows, D_MODEL), F32)], axis=0)
    mod_all = _modulation(c_all, w_mod, b_mod)

    rows_p = _Rows(nb, t, 512)
    rows_s = _Rows(ns, 1, ns)
    xp = x_prompt.reshape(nb * t, D_MODEL)
    xs = x_sample.reshape(ns, D_MODEL)
    w1 = w_mlp1.astype(BF16)
    w2 = w_mlp2.astype(BF16)

    outs_p, outs_s = {}, {}
    for layer in range(DEPTH):
        mod_p = rows_p.mod_array(mod_all[layer, ns:ns + nb])
        mod_s = rows_s.mod_array(mod_all[layer, :ns])
        hn_p = _norm_mod(rows_p, xp, g_norm1[layer], mod_p, 1, 0)
        hn_s = _norm_mod(rows_s, xs, g_norm1[layer], mod_s, 1, 0)
        if layer % 2 == 0:
            e = layer // 2
            prm = _even_params(e, w_in_e, w_pool, s_pool, g_qlat, w_qb, g_kvlat, w_kvb, g_q, g_k, w_out_e)
            a_p, w_o, mla_p, pool_p = _even_layer_prompt(rows_p, hn_p, prm, nb, t)
            a_s, _, mla_s, pool_s = _even_layer_sample(rows_s, hn_s, prm, state_pool[e], cache_mla, e,
                                                       page_table, pos0_s)
            outs_p.setdefault("mla", []).append(mla_p)
            outs_p.setdefault("pool", []).append(pool_p)
            outs_s.setdefault("mla", []).append(mla_s)
            outs_s.setdefault("pool", []).append(pool_s)
        else:
            o = layer // 2
            prm = _odd_params(o, w_in_o, w_conv_ssd, b_conv_ssd, dt_bias, a_log, d_skip, g_ssd_norm, w_conv_lru,
                              b_conv_lru, w_lru_r, b_lru_r, w_lru_i, b_lru_i, lru_lambda, w_out_o)
            a_p, w_o, sconv_p, ssd_p, lconv_p, lru_p = _odd_layer_prompt(rows_p, hn_p, prm, nb, t)
            a_s, _, sconv_s, ssd_s, lconv_s, lru_s = _odd_layer_sample(
                rows_s, hn_s, prm, state_ssd_conv[o], state_ssd[o], state_lru_conv[o], state_lru[o], pos0_s)
            for d, vals in ((outs_p, (sconv_p, ssd_p, lconv_p, lru_p)), (outs_s, (sconv_s, ssd_s, lconv_s, lru_s))):
                for name, val in zip(("sconv", "ssd", "lconv", "lru"), vals):
                    d.setdefault(name, []).append(val)
        xp = _matmul(rows_p, a_p, w_o, tn=1024, res=xp, mod=mod_p, gate_chunk=2)
        xs = _matmul(rows_s, a_s, w_o, tn=1024, res=xs, mod=mod_s, gate_chunk=2)
        hn2_p = _norm_mod(rows_p, xp, g_norm2[layer], mod_p, 4, 3)
        hn2_s = _norm_mod(rows_s, xs, g_norm2[layer], mod_s, 4, 3)
        act_p = _matmul(rows_p, [hn2_p], [w1[layer]], tn=1024, act="relu2", out_dtype=BF16)
        act_s = _matmul(rows_s, [hn2_s], [w1[layer]], tn=1024, act="relu2", out_dtype=BF16)
        xp = _matmul(rows_p, [act_p], [w2[layer]], tn=1024, tk=2048, res=xp, mod=mod_p, gate_chunk=5)
        xs = _matmul(rows_s, [act_s], [w2[layer]], tn=1024, tk=2048, res=xs, mod=mod_s, gate_chunk=5)

    st = lambda d, name: jnp.stack(d[name])
    return (xp.reshape(nb, t, D_MODEL), xs.reshape(ns, 1, D_MODEL),
            st(outs_p, "mla"), st(outs_s, "mla"), st(outs_p, "pool"), st(outs_s, "pool"),
            st(outs_p, "sconv"), st(outs_s, "sconv"), st(outs_p, "ssd"), st(outs_s, "ssd"),
            st(outs_p, "lconv"), st(outs_s, "lconv"), st(outs_p, "lru"), st(outs_s, "lru"))
```

```python
import functools

import jax
import jax.numpy as jnp
from jax import lax
from jax.experimental import pallas as pl
from jax.experimental.pallas import tpu as pltpu

F32 = jnp.float32
BF16 = jnp.bfloat16

VMEM_LIMIT_BYTES = 48 * 1024 * 1024
LANES = 128

D_MODEL = 2048
EPS = 1e-6
N_MOD = 6
DEPTH = 2

POOL_WIDTH = 1024
POOL_WINDOWS = (2, 4, 8, 16)
POOL_GROUP_DIM = 256
POOL_BUF = 15

MLA_HEADS = 8
MLA_NOPE = 128
MLA_ROPE = 64
MLA_V = 128
MLA_QK = 192
MLA_QKP = 256
MLA_Q_LORA = 512
MLA_KV_LORA = 256
ROPE_THETA = 10000.0
PAGE_SIZE = 128
EVEN_IN_PAD = 1920

SSD_D_INNER = 2048
SSD_HEAD_DIM = 64
SSD_HEADS = 32
SSD_GROUPS = 4
SSD_STATE = 128
SSD_CHUNK = 128
SSD_BC = 1024
CONV_WIDTH = 4

LRU_WIDTH = 1024
LRU_BLOCKS = 8
LRU_BLOCK_DIM = 128
LRU_C = 8.0

MLP_HIDDEN = 8192


def _cparams(*sem):
    return pltpu.CompilerParams(dimension_semantics=sem, vmem_limit_bytes=VMEM_LIMIT_BYTES)


def _sigmoid(x):
    return 1.0 / (1.0 + jnp.exp(-x))


def _silu(x):
    return x * _sigmoid(x)


def _softplus(x):
    return jnp.maximum(x, 0.0) + jnp.log1p(jnp.exp(-jnp.abs(x)))


def _gelu_tanh(x):
    return 0.5 * x * (1.0 + jnp.tanh(0.7978845608028654 * (x + 0.044715 * (x * x * x))))


def _dot(a, b):
    return jnp.dot(a, b, preferred_element_type=F32)


def _dot_nt(a, b):
    return lax.dot_general(a, b, (((1,), (1,)), ((), ())), preferred_element_type=F32)


def _split3(x):
    hi = x.astype(BF16)
    r = x - hi.astype(F32)
    mid = r.astype(BF16)
    lo = (r - mid.astype(F32)).astype(BF16)
    return hi, mid, lo


def _dot_sel(x, sel):
    hi, mid, lo = _split3(x)
    return _dot(hi, sel) + _dot(mid, sel) + _dot(lo, sel)


def _mod_kernel(c_ref, w_ref, b_ref, o_ref):
    c = c_ref[...]
    o_ref[...] = _dot(_silu(c).astype(BF16), w_ref[...].astype(BF16)) + b_ref[...]


def _modulation(c_all, w_mod, b_mod):
    mp = c_all.shape[0]
    n = N_MOD * D_MODEL
    tn = 1024
    return pl.pallas_call(
        _mod_kernel, name="modulation",
        grid=(DEPTH, n // tn),
        in_specs=[pl.BlockSpec((mp, D_MODEL), lambda l, j: (0, 0)),
                  pl.BlockSpec((None, D_MODEL, tn), lambda l, j: (l, 0, j)),
                  pl.BlockSpec((None, 1, tn), lambda l, j: (l, 0, j))],
        out_specs=pl.BlockSpec((None, mp, tn), lambda l, j: (l, 0, j)),
        out_shape=jax.ShapeDtypeStruct((DEPTH, mp, n), F32),
        compiler_params=_cparams("parallel", "parallel"),
    )(c_all, w_mod, b_mod.reshape(DEPTH, 1, n))


class _Rows:
    def __init__(self, nb, t, tm):
        self.nb, self.t, self.tm = nb, t, tm
        self.m = nb * t
        self.per_seq = t > 1
        if self.per_seq:
            assert t % tm == 0
            self.tiles_per_seq = t // tm

    def mod_array(self, mod_rows):
        return mod_rows.reshape(self.nb, 1, -1) if self.per_seq else mod_rows

    def mod_spec(self, chunk, width, ncol, col_of):
        per_chunk = D_MODEL // width
        if self.per_seq:
            tps = self.tiles_per_seq
            return pl.BlockSpec((None, 1, width),
                                lambda *g: (g[0] // tps, 0, chunk * per_chunk + col_of(*g)))
        return pl.BlockSpec((self.tm, width), lambda *g: (g[0], chunk * per_chunk + col_of(*g)))


def _norm_mod_kernel(x_ref, g_ref, sc_ref, sh_ref, o_ref):
    x = x_ref[...]
    y = x * lax.rsqrt(jnp.mean(x * x, axis=-1, keepdims=True) + EPS) * g_ref[...]
    o_ref[...] = (y * (1.0 + sc_ref[...]) + sh_ref[...]).astype(o_ref.dtype)


def _norm_mod(rows, x, g, mod, sc_chunk, sh_chunk):
    tm = min(rows.tm, 512)
    r = _Rows(rows.nb, rows.t, tm)
    return pl.pallas_call(
        _norm_mod_kernel, name="norm_mod",
        grid=(r.m // tm,),
        in_specs=[pl.BlockSpec((tm, D_MODEL), lambda i: (i, 0)),
                  pl.BlockSpec((1, D_MODEL), lambda i: (0, 0)),
                  r.mod_spec(sc_chunk, D_MODEL, 1, lambda i: 0),
                  r.mod_spec(sh_chunk, D_MODEL, 1, lambda i: 0)],
        out_specs=pl.BlockSpec((tm, D_MODEL), lambda i: (i, 0)),
        out_shape=jax.ShapeDtypeStruct((r.m, D_MODEL), BF16),
        compiler_params=_cparams("parallel"),
    )(x, g.reshape(1, D_MODEL), mod, mod)


def _mm_kernel(*refs, n_a, nk, act, has_res):
    a_refs = refs[:n_a]
    w_refs = refs[n_a:2 * n_a]
    pos = 2 * n_a
    if has_res:
        x_ref, gt_ref = refs[pos], refs[pos + 1]
        pos += 2
    o_ref = refs[pos]
    acc_ref = refs[pos + 1] if nk > 1 else None

    part = _dot(a_refs[0][...], w_refs[0][...].astype(BF16))
    for a_ref, w_ref in zip(a_refs[1:], w_refs[1:]):
        part = part + _dot(a_ref[...], w_ref[...].astype(BF16))

    def finish(acc):
        if act == "relu2":
            acc = jnp.square(jnp.maximum(acc, 0.0))
        if has_res:
            acc = x_ref[...] + gt_ref[...] * acc
        o_ref[...] = acc.astype(o_ref.dtype)

    if nk == 1:
        finish(part)
    else:
        k = pl.program_id(2)

        @pl.when(k == 0)
        def _():
            acc_ref[...] = part

        @pl.when(k > 0)
        def _():
            acc_ref[...] += part

        @pl.when(k == nk - 1)
        def _():
            finish(acc_ref[...])


def _matmul(rows, a_list, w_list, *, name, tn, tk=None, act=None, res=None, mod=None, gate_chunk=None,
            out_dtype=F32, alias_res=False):
    tm, m = rows.tm, rows.m
    n = (w_list[0][0] if isinstance(w_list[0], tuple) else w_list[0]).shape[-1]
    n_a = len(a_list)
    if tk is None or n_a > 1:
        nk = 1
    else:
        assert a_list[0].shape[1] % tk == 0
        nk = a_list[0].shape[1] // tk
    assert n % tn == 0 and m % tm == 0
    in_specs, args = [], []
    for a in a_list:
        kk = a.shape[1] if nk == 1 else tk
        in_specs.append(pl.BlockSpec((tm, kk), lambda i, j, k: (i, k)))
        args.append(a)
    for w in w_list:
        if isinstance(w, tuple):
            w, layer = w
            kk = w.shape[1] if nk == 1 else tk
            in_specs.append(pl.BlockSpec((None, kk, tn), lambda i, j, k, layer=layer: (layer, k, j)))
        else:
            kk = w.shape[0] if nk == 1 else tk
            in_specs.append(pl.BlockSpec((kk, tn), lambda i, j, k: (k, j)))
        args.append(w)
    has_res = res is not None
    if has_res:
        in_specs.append(pl.BlockSpec((tm, tn), lambda i, j, k: (i, j)))
        in_specs.append(rows.mod_spec(gate_chunk, tn, n // tn, lambda i, j, k: j))
        args += [res, mod]
    scratch = [pltpu.VMEM((tm, tn), F32)] if nk > 1 else []
    aliases = {2 * n_a: 0} if (has_res and alias_res) else {}
    return pl.pallas_call(
        functools.partial(_mm_kernel, n_a=n_a, nk=nk, act=act, has_res=has_res), name=name,
        input_output_aliases=aliases,
        grid=(m // tm, n // tn, nk),
        in_specs=in_specs,
        out_specs=pl.BlockSpec((tm, tn), lambda i, j, k: (i, j)),
        out_shape=jax.ShapeDtypeStruct((m, n), out_dtype),
        scratch_shapes=scratch,
        compiler_params=_cparams("parallel", "parallel", "arbitrary"),
    )(*args)


def _bmm_kernel(a_ref, w_ref, o_ref):
    o_ref[...] = _dot(a_ref[...], w_ref[...]).astype(o_ref.dtype)


def _bmm_heads(a, w, out_dtype):
    h, m, k = a.shape
    n = w.shape[2]
    return pl.pallas_call(
        _bmm_kernel, name="bmm_heads",
        grid=(h,),
        in_specs=[pl.BlockSpec((None, m, k), lambda i: (i, 0, 0)),
                  pl.BlockSpec((None, k, n), lambda i: (i, 0, 0))],
        out_specs=pl.BlockSpec((None, m, n), lambda i: (i, 0, 0)),
        out_shape=jax.ShapeDtypeStruct((h, m, n), out_dtype),
        compiler_params=_cparams("parallel"),
    )(a, w)


CONV_HALO = 8


def _conv_chunk(ext_ref, u, buf_ref, w_ref, b_ref, first):
    tc = u.shape[0]
    k1 = CONV_WIDTH - 1

    @pl.when(first)
    def _():
        ext_ref[CONV_HALO - k1:CONV_HALO, :] = buf_ref[...]

    ext_ref[CONV_HALO:CONV_HALO + tc, :] = u
    y = b_ref[...] + w_ref[k1:k1 + 1, :] * u
    for k in range(k1):
        y = y + w_ref[k:k + 1, :] * ext_ref[CONV_HALO - k1 + k:CONV_HALO - k1 + k + tc, :]
    ext_ref[CONV_HALO - k1:CONV_HALO, :] = ext_ref[CONV_HALO + tc - k1:CONV_HALO + tc, :]
    return y


def _conv_step(buf_ref, u, w_ref, b_ref):
    y = b_ref[...] + w_ref[CONV_WIDTH - 1:CONV_WIDTH, :] * u
    for k in range(CONV_WIDTH - 1):
        y = y + w_ref[k:k + 1, :] * buf_ref[k]
    return y


def _pool_prompt_kernel(u_ref, buf_ref, w_ref, s_ref, o_ref, ext_ref, *, tc, pos0):
    c = pl.program_id(1)
    halo = 16

    @pl.when(c == 0)
    def _():
        ext_ref[0:1, :] = jnp.zeros((1, POOL_WIDTH), F32)
        ext_ref[1:halo, :] = buf_ref[...]

    u = u_ref[...]
    ext_ref[halo:halo + tc, :] = u
    pos = (pos0 + c * tc + lax.broadcasted_iota(jnp.int32, (tc, 1), 0)).astype(F32)
    outs = []
    for g, w in enumerate(POOL_WINDOWS):
        lo, hi = g * POOL_GROUP_DIM, (g + 1) * POOL_GROUP_DIM
        x = u[:, lo:hi]
        wsum = x
        for j in range(1, w):
            wsum = wsum + ext_ref[halo - j:halo - j + tc, lo:hi]
        cnt = jnp.minimum(pos + 1.0, float(w))
        d = (wsum / cnt - x).astype(BF16)
        outs.append(_dot(d, w_ref[g]))
    y = jnp.concatenate(outs, axis=1) * s_ref[...]
    o_ref[...] = y.astype(o_ref.dtype)
    ext_ref[0:halo, :] = ext_ref[tc:tc + halo, :]


def _pool_prompt(u, buf, w_pool, s_pool, nb, t, pos0):
    tc = 256
    nc = t // tc
    return pl.pallas_call(
        functools.partial(_pool_prompt_kernel, tc=tc, pos0=pos0), name="pool_prompt",
        grid=(nb, nc),
        in_specs=[pl.BlockSpec((tc, POOL_WIDTH), lambda b, c: (b * nc + c, 0)),
                  pl.BlockSpec((None, POOL_BUF, POOL_WIDTH), lambda b, c: (b, 0, 0)),
                  pl.BlockSpec((len(POOL_WINDOWS), POOL_GROUP_DIM, POOL_GROUP_DIM), lambda b, c: (0, 0, 0)),
                  pl.BlockSpec((1, POOL_WIDTH), lambda b, c: (0, 0))],
        out_specs=pl.BlockSpec((tc, POOL_WIDTH), lambda b, c: (b * nc + c, 0)),
        out_shape=jax.ShapeDtypeStruct((nb * t, POOL_WIDTH), BF16),
        scratch_shapes=[pltpu.VMEM((16 + tc, POOL_WIDTH), F32)],
        compiler_params=_cparams("parallel", "arbitrary"),
    )(u, buf, w_pool, s_pool)


def _pool_step_kernel(u_ref, buf_ref, w_ref, s_ref, o_ref, *, pos0):
    outs = []
    for g, w in enumerate(POOL_WINDOWS):
        lo, hi = g * POOL_GROUP_DIM, (g + 1) * POOL_GROUP_DIM
        x = u_ref[:, lo:hi]
        wsum = x
        for j in range(1, w):
            wsum = wsum + buf_ref[POOL_BUF - j, :, lo:hi]
        cnt = min(pos0 + 1.0, float(w))
        d = (wsum / cnt - x).astype(BF16)
        outs.append(_dot(d, w_ref[g]))
    o_ref[...] = (jnp.concatenate(outs, axis=1) * s_ref[...]).astype(o_ref.dtype)


def _pool_step(u, buf_t, w_pool, s_pool, pos0):
    m = buf_t.shape[1]
    return pl.pallas_call(
        functools.partial(_pool_step_kernel, pos0=pos0), name="pool_step",
        grid=(1,),
        in_specs=[pl.BlockSpec((m, POOL_WIDTH), lambda i: (0, 0)),
                  pl.BlockSpec((POOL_BUF, m, POOL_WIDTH), lambda i: (0, 0, 0)),
                  pl.BlockSpec((len(POOL_WINDOWS), POOL_GROUP_DIM, POOL_GROUP_DIM), lambda i: (0, 0, 0)),
                  pl.BlockSpec((1, POOL_WIDTH), lambda i: (0, 0))],
        out_specs=pl.BlockSpec((m, POOL_WIDTH), lambda i: (0, 0)),
        out_shape=jax.ShapeDtypeStruct((m, POOL_WIDTH), BF16),
        compiler_params=_cparams("arbitrary"),
    )(u, buf_t, w_pool, s_pool)


def _rope128(x, cc, ss):
    lane = lax.broadcasted_iota(jnp.int32, x.shape, 1)
    swapped = jnp.where(lane < MLA_ROPE // 2, pltpu.roll(x, LANES - MLA_ROPE // 2, 1),
                        pltpu.roll(x, MLA_ROPE // 2, 1))
    return x * cc + swapped * ss


def _mla_prep_kernel(uq_ref, ukv_ref, ukr_ref, gql_ref, gkvl_ref, wq_ref, wkv_ref, gq_ref, gk_ref,
                     cc_ref, ss_ref, *out_refs, emit_qg):
    if emit_qg:
        q_ref, k_ref, v_ref, lat_ref, qg_ref = out_refs
    else:
        q_ref, k_ref, v_ref, lat_ref = out_refs
    cc = cc_ref[...]
    ss = ss_ref[...]
    scale = MLA_QK ** -0.5

    uq = uq_ref[...]
    qn = uq * lax.rsqrt(jnp.mean(uq * uq, axis=-1, keepdims=True) + EPS) * gql_ref[...]
    qall = _dot(qn.astype(BF16), wq_ref[...])
    ukv = ukv_ref[...]
    lat = ukv * lax.rsqrt(jnp.mean(ukv * ukv, axis=-1, keepdims=True) + EPS) * gkvl_ref[...]
    lat_ref[...] = lat
    kvall = _dot(lat.astype(BF16), wkv_ref[...])
    kr = ukr_ref[...]
    kr_ss = jnp.sum(kr * kr, axis=-1, keepdims=True)

    gq_n, gq_r = gq_ref[:, :MLA_NOPE], gq_ref[:, MLA_NOPE:]
    gk_n, gk_r = gk_ref[:, :MLA_NOPE], gk_ref[:, MLA_NOPE:]
    nh = MLA_HEADS
    for h in range(nh):
        qnope = qall[:, h * 128:(h + 1) * 128]
        qrope = qall[:, (nh + h) * 128:(nh + h + 1) * 128]
        ssq = jnp.sum(qnope * qnope, axis=-1, keepdims=True) + jnp.sum(qrope * qrope, axis=-1, keepdims=True)
        rs = lax.rsqrt(ssq * (1.0 / MLA_QK) + EPS) * scale
        qn_h = qnope * rs * gq_n
        qr_h = _rope128(qrope * rs * gq_r, cc, ss)
        q_ref[h] = jnp.concatenate([qn_h, qr_h], axis=1).astype(q_ref.dtype)
        if emit_qg:
            qg_ref[h] = (qn_h * gk_n).astype(qg_ref.dtype)

        knope = kvall[:, h * 256:h * 256 + 128]
        ssk = jnp.sum(knope * knope, axis=-1, keepdims=True) + kr_ss
        rsk = lax.rsqrt(ssk * (1.0 / MLA_QK) + EPS)
        kr_h = _rope128(kr * rsk * gk_r, cc, ss)
        k_ref[h] = jnp.concatenate([knope * rsk * gk_n, kr_h], axis=1).astype(k_ref.dtype)
        v_ref[h] = kvall[:, h * 256 + 128:(h + 1) * 256].astype(v_ref.dtype)


def _mla_prep(u, m, tm, pos_of_tile, tabs, prm, emit_qg):
    cc_tab, ss_tab = tabs
    nh = MLA_HEADS
    out_shape = [jax.ShapeDtypeStruct((nh, m, MLA_QKP), BF16),
                 jax.ShapeDtypeStruct((nh, m, MLA_QKP), BF16),
                 jax.ShapeDtypeStruct((nh, m, MLA_V), BF16),
                 jax.ShapeDtypeStruct((m, MLA_KV_LORA), F32)]
    out_specs = [pl.BlockSpec((nh, tm, MLA_QKP), lambda i: (0, i, 0)),
                 pl.BlockSpec((nh, tm, MLA_QKP), lambda i: (0, i, 0)),
                 pl.BlockSpec((nh, tm, MLA_V), lambda i: (0, i, 0)),
                 pl.BlockSpec((tm, MLA_KV_LORA), lambda i: (i, 0))]
    if emit_qg:
        out_shape.append(jax.ShapeDtypeStruct((nh, m, MLA_NOPE), BF16))
        out_specs.append(pl.BlockSpec((nh, tm, MLA_NOPE), lambda i: (0, i, 0)))
    full = lambda a: pl.BlockSpec(a.shape, lambda i: (0,) * a.ndim)
    return pl.pallas_call(
        functools.partial(_mla_prep_kernel, emit_qg=emit_qg), name="mla_prep",
        grid=(m // tm,),
        in_specs=[pl.BlockSpec((tm, MLA_Q_LORA), lambda i: (i, POOL_WIDTH // MLA_Q_LORA)),
                  pl.BlockSpec((tm, MLA_KV_LORA), lambda i: (i, (POOL_WIDTH + MLA_Q_LORA) // MLA_KV_LORA)),
                  pl.BlockSpec((tm, LANES), lambda i: (i, (POOL_WIDTH + MLA_Q_LORA + MLA_KV_LORA) // LANES)),
                  full(prm["g_qlat"]), full(prm["g_kvlat"]), full(prm["w_qb"]), full(prm["w_kvb"]),
                  full(prm["g_q"]), full(prm["g_k"]),
                  pl.BlockSpec((tm, LANES), lambda i: (pos_of_tile(i), 0)),
                  pl.BlockSpec((tm, LANES), lambda i: (pos_of_tile(i), 0))],
        out_specs=out_specs,
        out_shape=out_shape,
        compiler_params=_cparams("parallel"),
    )(u, u, u, prm["g_qlat"], prm["g_kvlat"], prm["w_qb"], prm["w_kvb"], prm["g_q"], prm["g_k"],
      cc_tab, ss_tab)


def _flash_kernel(q_ref, k_ref, v_ref, o_ref, *, tq):
    i = pl.program_id(2)
    q = q_ref[...]

    def step(j, carry, masked):
        m, l, acc = carry
        start = pl.multiple_of(j * tq, tq)
        kb = k_ref[pl.ds(start, tq), :]
        vb = v_ref[pl.ds(start, tq), :]
        s = _dot_nt(q, kb)
        if masked:
            r = lax.broadcasted_iota(jnp.int32, (tq, tq), 0)
            c = lax.broadcasted_iota(jnp.int32, (tq, tq), 1)
            s = jnp.where(c <= r, s, -jnp.inf)
        m_new = jnp.maximum(m, jnp.max(s, axis=1, keepdims=True))
        alpha = jnp.exp(m - m_new)
        p = jnp.exp(s - m_new)
        l = alpha * l + jnp.sum(p, axis=1, keepdims=True)
        acc = alpha * acc + _dot(p.astype(BF16), vb)
        return m_new, l, acc

    init = (jnp.full((tq, 1), -jnp.inf, F32), jnp.zeros((tq, 1), F32), jnp.zeros((tq, MLA_V), F32))
    carry = lax.fori_loop(0, i, lambda j, c: step(j, c, False), init)
    _, l, acc = step(i, carry, True)
    o_ref[...] = (acc / l).astype(o_ref.dtype)


def _flash_attention(q, k, v, nb, t):
    tq = 512
    nq = t // tq
    nh = MLA_HEADS
    return pl.pallas_call(
        functools.partial(_flash_kernel, tq=tq), name="flash_prompt",
        grid=(nb, nh, nq),
        in_specs=[pl.BlockSpec((None, tq, MLA_QKP), lambda b, h, i: (h, b * nq + i, 0)),
                  pl.BlockSpec((None, t, MLA_QKP), lambda b, h, i: (h, b, 0)),
                  pl.BlockSpec((None, t, MLA_V), lambda b, h, i: (h, b, 0))],
        out_specs=pl.BlockSpec((tq, MLA_V), lambda b, h, i: (b * nq + i, h)),
        out_shape=jax.ShapeDtypeStruct((nb * t, nh * MLA_V), BF16),
        compiler_params=_cparams("parallel", "parallel", "arbitrary"),
    )(q, k, v)


DEC_PAGES_PER_STEP = 16
DEC_SUB_PAGES = 4


def _page_copy(pt_ref, cache_ref, buf_ref, sem_ref, layer, b, j, i, slot):
    page = pt_ref[b, j * DEC_PAGES_PER_STEP + i]
    return pltpu.make_async_copy(cache_ref.at[layer, page], buf_ref.at[slot, i], sem_ref.at[slot])


def _decode_kernel(pt_ref, cache_ref, wt_ref, qp_ref, qr_ref, gr_ref, cc_ref, ss_ref, qf_ref, kn_ref, latn_ref,
                   o_ref, buf_ref, sem_ref, m_ref, l_ref, acc_ref, s_ref, latb_ref, *, nj, nsamp, layer):
    npg = DEC_PAGES_PER_STEP
    b = pl.program_id(0)
    j = pl.program_id(1)
    nh = MLA_HEADS
    step = b * nj + j
    slot = lax.rem(step, 2)

    def fetch(bb, jj, sl):
        for i in range(npg):
            _page_copy(pt_ref, cache_ref, buf_ref, sem_ref, layer, bb, jj, i, sl).start()

    @pl.when(step == 0)
    def _():
        fetch(b, j, slot)

    last_j = j == nj - 1
    b_next = jnp.where(last_j, b + 1, b)
    j_next = jnp.where(last_j, 0, j + 1)

    @pl.when(step + 1 < nsamp * nj)
    def _():
        fetch(b_next, j_next, 1 - slot)

    for i in range(npg):
        _page_copy(pt_ref, cache_ref, buf_ref, sem_ref, layer, b, j, i, slot).wait()
    pages = [buf_ref.at[slot, i] for i in range(npg)]

    @pl.when(j == 0)
    def _():
        m_ref[...] = jnp.full(m_ref.shape, -jnp.inf, F32)
        l_ref[...] = jnp.zeros(l_ref.shape, F32)
        acc_ref[...] = jnp.zeros(acc_ref.shape, F32)

    wt = wt_ref[...]
    qp = qp_ref[...]
    qr = qr_ref[...]
    gr = gr_ref[...]
    half = MLA_ROPE // 2
    sub = DEC_SUB_PAGES * PAGE_SIZE
    for t in range(npg // DEC_SUB_PAGES):
        r0 = t * sub
        pg = pages[t * DEC_SUB_PAGES:(t + 1) * DEC_SUB_PAGES]
        latb = jnp.concatenate([p[0:MLA_KV_LORA, :] for p in pg], axis=1).astype(BF16)
        kr = jnp.concatenate([p[MLA_KV_LORA:MLA_KV_LORA + MLA_ROPE, :] for p in pg], axis=1)
        latb_ref[:, r0:r0 + sub] = latb
        kt = _dot(wt, latb)
        ssn = jnp.concatenate(
            [jnp.sum(jnp.square(kt[h * 128:(h + 1) * 128, :]), axis=0, keepdims=True) for h in range(nh)], axis=0)
        sn = _dot(qp, latb)[:nh]
        krg = kr * gr
        swapped = jnp.concatenate([krg[half:, :], krg[:half, :]], axis=0)
        rot = krg * cc_ref[:, r0:r0 + sub] + swapped * ss_ref[:, r0:r0 + sub]
        feat = jnp.concatenate([rot, kr * kr], axis=0).astype(BF16)
        rr = _dot(qr, feat)
        rs = lax.rsqrt((ssn + rr[nh:nh + 1]) * (1.0 / MLA_QK) + EPS)
        s_ref[:, r0:r0 + sub] = rs * (sn + rr[:nh])

    s = s_ref[...]
    m_old = m_ref[...]
    m_new = jnp.maximum(m_old, jnp.max(s, axis=1, keepdims=True))
    alpha = jnp.exp(m_old - m_new)
    p = jnp.exp(s - m_new)
    l_new = alpha * l_ref[...] + jnp.sum(p, axis=1, keepdims=True)
    acc_new = alpha * acc_ref[...] + _dot_nt(p.astype(BF16), latb_ref[...])
    m_ref[...] = m_new
    l_ref[...] = l_new
    acc_ref[...] = acc_new

    @pl.when(j == nj - 1)
    def _():
        s_new = jnp.sum(qf_ref[...] * kn_ref[...], axis=1, keepdims=True)
        m_fin = jnp.maximum(m_new, s_new)
        a_fin = jnp.exp(m_new - m_fin)
        p_new = jnp.exp(s_new - m_fin)
        l_fin = a_fin * l_new + p_new
        lat_new = latn_ref[...].astype(BF16).astype(F32)
        acc_fin = a_fin * acc_new + p_new.astype(BF16).astype(F32) * lat_new
        o_ref[...] = acc_fin / l_fin


def _mla_decode(cache, layer, page_table, wt_nope, qp, qr, g_rope, cc_keys, ss_keys, qf, kn, lat_new):
    nsamp, n_pages = page_table.shape
    npg = DEC_PAGES_PER_STEP
    nj = n_pages // npg
    assert n_pages % npg == 0 and npg % DEC_SUB_PAGES == 0
    step_keys = npg * PAGE_SIZE
    kvd = cache.shape[-1]
    cache = jnp.transpose(cache, (0, 1, 3, 2))
    nh = MLA_HEADS
    grid_spec = pltpu.PrefetchScalarGridSpec(
        num_scalar_prefetch=1,
        grid=(nsamp, nj),
        in_specs=[
            pl.BlockSpec(memory_space=pl.ANY),
            pl.BlockSpec(wt_nope.shape, lambda b, j, pt: (0, 0)),
            pl.BlockSpec((None, 16, MLA_KV_LORA), lambda b, j, pt: (b, 0, 0)),
            pl.BlockSpec((None, 16, LANES), lambda b, j, pt: (b, 0, 0)),
            pl.BlockSpec((MLA_ROPE, 1), lambda b, j, pt: (0, 0)),
            pl.BlockSpec((MLA_ROPE, step_keys), lambda b, j, pt: (0, j)),
            pl.BlockSpec((MLA_ROPE, step_keys), lambda b, j, pt: (0, j)),
            pl.BlockSpec((None, nh, MLA_QKP), lambda b, j, pt: (b, 0, 0)),
            pl.BlockSpec((None, nh, MLA_QKP), lambda b, j, pt: (b, 0, 0)),
            pl.BlockSpec((None, 1, MLA_KV_LORA), lambda b, j, pt: (b, 0, 0)),
        ],
        out_specs=pl.BlockSpec((None, nh, MLA_KV_LORA), lambda b, j, pt: (b, 0, 0)),
        scratch_shapes=[pltpu.VMEM((2, npg, kvd, PAGE_SIZE), F32), pltpu.SemaphoreType.DMA((2,)),
                        pltpu.VMEM((nh, 1), F32), pltpu.VMEM((nh, 1), F32),
                        pltpu.VMEM((nh, MLA_KV_LORA), F32),
                        pltpu.VMEM((nh, step_keys), F32),
                        pltpu.VMEM((MLA_KV_LORA, step_keys), BF16)],
    )
    return pl.pallas_call(
        functools.partial(_decode_kernel, nj=nj, nsamp=nsamp, layer=layer), name="mla_decode",
        grid_spec=grid_spec,
        out_shape=jax.ShapeDtypeStruct((nsamp, nh, MLA_KV_LORA), F32),
        compiler_params=_cparams("arbitrary", "arbitrary"),
    )(page_table, cache, wt_nope, qp, qr, g_rope, cc_keys, ss_keys, qf, kn, lat_new)


def _ssd_gate_norm(y, xs, z, dskip, gnorm):
    y = (y + dskip * xs) * _silu(z)
    gw = SSD_D_INNER // SSD_GROUPS
    outs = []
    for g in range(SSD_GROUPS):
        yg = y[:, g * gw:(g + 1) * gw]
        outs.append(yg * lax.rsqrt(jnp.mean(yg * yg, axis=-1, keepdims=True) + EPS))
    return jnp.concatenate(outs, axis=1) * gnorm


def _ssd_prompt_kernel(z_ref, x_ref, bc_ref, dt_ref, bufx_ref, bufbc_ref, wx_ref, bx_ref, wbc_ref, bbc_ref,
                       dtb_ref, alog_ref, dskip_ref, gnorm_ref, h0_ref, tri_ref, exp_ref,
                       y_ref, hout_ref, extx_ref, extbc_ref, ht_ref, *, nc):
    c = pl.program_id(1)
    first = c == 0
    l = SSD_CHUNK
    xs = _silu(_conv_chunk(extx_ref, x_ref[...], bufx_ref, wx_ref, bx_ref, first))
    bcs = _silu(_conv_chunk(extbc_ref, bc_ref[...], bufbc_ref, wbc_ref, bbc_ref, first))
    gs = SSD_GROUPS * SSD_STATE
    bm, cm = bcs[:, :gs], bcs[:, gs:]

    @pl.when(first)
    def _():
        for r in range(SSD_D_INNER // LANES):
            ht_ref[:, r * LANES:(r + 1) * LANES] = h0_ref[r * LANES:(r + 1) * LANES, :].T

    dt = _softplus(dt_ref[...] + dtb_ref[...])
    a = -jnp.exp(alog_ref[...])
    acum = _dot_sel_left(tri_ref[...], dt * a)
    acum_t = acum.T
    dt_t = dt.T
    a_last = acum[l - 1:l, :]
    dend = jnp.exp(a_last - acum)
    sel = exp_ref[...]
    w_state = _dot_sel(dt * dend, sel)
    e_acc = _dot_sel(jnp.exp(acum), sel)
    e_last = _dot_sel(jnp.exp(a_last), sel)

    rows = lax.broadcasted_iota(jnp.int32, (l, l), 0)
    cols = lax.broadcasted_iota(jnp.int32, (l, l), 1)
    causal = cols <= rows
    lane = lax.broadcasted_iota(jnp.int32, (l, LANES), 1)
    xs_b = xs.astype(BF16)
    hpg = SSD_HEADS // SSD_GROUPS
    gw = SSD_D_INNER // SSD_GROUPS
    y_groups = []
    for g in range(SSD_GROUPS):
        gcols = slice(g * gw, (g + 1) * gw)
        bg = bm[:, g * SSD_STATE:(g + 1) * SSD_STATE]
        cg_b = cm[:, g * SSD_STATE:(g + 1) * SSD_STATE].astype(BF16)
        cb = _dot_nt(cg_b, bg.astype(BF16))
        ht_g = ht_ref[:, gcols]
        y_off = _dot(cg_b, ht_g.astype(BF16))
        y_pairs = []
        for pr in range(hpg // 2):
            col = (g * hpg + pr * 2) * SSD_HEAD_DIM
            res = []
            for hh in range(2):
                h = g * hpg + pr * 2 + hh
                diff = acum[:, h:h + 1] - acum_t[h:h + 1, :]
                lmat = jnp.exp(jnp.where(causal, diff, -jnp.inf)) * dt_t[h:h + 1, :]
                res.append(_dot((cb * lmat).astype(BF16), xs_b[:, col:col + LANES]))
            y_pairs.append(jnp.where(lane < SSD_HEAD_DIM, res[0], res[1]))
        y_groups.append(jnp.concatenate(y_pairs, axis=1) + e_acc[:, gcols] * y_off)
        xw = (xs[:, gcols] * w_state[:, gcols]).astype(BF16)
        ht_ref[:, gcols] = e_last[:, gcols] * ht_g + _dot(bg.T.astype(BF16), xw)
    y = jnp.concatenate(y_groups, axis=1)
    y_ref[...] = _ssd_gate_norm(y, xs, z_ref[...], dskip_ref[...], gnorm_ref[...]).astype(y_ref.dtype)

    @pl.when(c == nc - 1)
    def _():
        for r in range(SSD_D_INNER // LANES):
            hout_ref[r * LANES:(r + 1) * LANES, :] = ht_ref[:, r * LANES:(r + 1) * LANES].T


def _dot_sel_left(sel, x):
    hi, mid, lo = _split3(x)
    return _dot(sel, hi) + _dot(sel, mid) + _dot(sel, lo)


def _ssd_consts():
    l = SSD_CHUNK
    tri = (jnp.arange(l)[:, None] >= jnp.arange(l)[None, :]).astype(BF16)
    sel = (jnp.arange(LANES)[:, None] == (jnp.arange(SSD_D_INNER)[None, :] // SSD_HEAD_DIM)).astype(BF16)
    return tri, sel


def _ssd_prompt(u, u_dt, bufx, bufbc, h0, prm, nb, t):
    l = SSD_CHUNK
    nc = t // l
    tri, sel = _ssd_consts()
    full = lambda a: pl.BlockSpec(a.shape, lambda b, c: (0,) * a.ndim)
    row = lambda width, colblk: pl.BlockSpec((l, width), lambda b, c: (b * nc + c, colblk))
    y, hout = pl.pallas_call(
        functools.partial(_ssd_prompt_kernel, nc=nc), name="ssd_prompt",
        grid=(nb, nc),
        in_specs=[row(SSD_D_INNER, 0), row(SSD_D_INNER, 1), row(SSD_BC, 4), row(LANES, 0),
                  pl.BlockSpec((None, CONV_WIDTH - 1, SSD_D_INNER), lambda b, c: (b, 0, 0)),
                  pl.BlockSpec((None, CONV_WIDTH - 1, SSD_BC), lambda b, c: (b, 0, 0)),
                  full(prm["w_conv_x"]), full(prm["b_conv_x"]), full(prm["w_conv_bc"]), full(prm["b_conv_bc"]),
                  full(prm["dt_bias"]), full(prm["a_log"]), full(prm["d_skip"]), full(prm["g_ssd_norm"]),
                  pl.BlockSpec((None, SSD_D_INNER, SSD_STATE), lambda b, c: (b, 0, 0)),
                  full(tri), full(sel)],
        out_specs=[pl.BlockSpec((l, SSD_D_INNER), lambda b, c: (b * nc + c, 0)),
                   pl.BlockSpec((None, SSD_D_INNER, SSD_STATE), lambda b, c: (b, 0, 0))],
        out_shape=[jax.ShapeDtypeStruct((nb * t, SSD_D_INNER), BF16),
                   jax.ShapeDtypeStruct((nb, SSD_D_INNER, SSD_STATE), F32)],
        scratch_shapes=[pltpu.VMEM((CONV_HALO + l, SSD_D_INNER), F32),
                        pltpu.VMEM((CONV_HALO + l, SSD_BC), F32),
                        pltpu.VMEM((SSD_STATE, SSD_D_INNER), F32)],
        compiler_params=_cparams("parallel", "arbitrary"),
    )(u, u, u, u_dt, bufx, bufbc, prm["w_conv_x"], prm["b_conv_x"], prm["w_conv_bc"], prm["b_conv_bc"],
      prm["dt_bias"], prm["a_log"], prm["d_skip"], prm["g_ssd_norm"], h0, tri, sel)
    return y, hout


def _ssd_step_prep_kernel(x_ref, bc_ref, dt_ref, bufx_ref, bufbc_ref, wx_ref, bx_ref, wbc_ref, bbc_ref,
                          dtb_ref, alog_ref, exp_ref, xs_ref, bcs_ref, dtx_ref, dae_ref):
    xs = _silu(_conv_step(bufx_ref, x_ref[...], wx_ref, bx_ref))
    xs_ref[...] = xs
    bcs_ref[...] = _silu(_conv_step(bufbc_ref, bc_ref[...], wbc_ref, bbc_ref))
    dt = _softplus(dt_ref[...] + dtb_ref[...])
    da = jnp.exp(dt * (-jnp.exp(alog_ref[...])))
    sel = exp_ref[...]
    dtx_ref[...] = _dot_sel(dt, sel) * xs
    dae_ref[...] = _dot_sel(da, sel)


def _ssd_step_prep(u, u_dt, bufx_t, bufbc_t, prm):
    m = u_dt.shape[0]
    _, sel = _ssd_consts()
    full = lambda a: pl.BlockSpec(a.shape, lambda i: (0,) * a.ndim)
    return pl.pallas_call(
        _ssd_step_prep_kernel, name="ssd_step_prep",
        grid=(1,),
        in_specs=[pl.BlockSpec((m, SSD_D_INNER), lambda i: (0, 1)),
                  pl.BlockSpec((m, SSD_BC), lambda i: (0, 4)),
                  full(u_dt), full(bufx_t), full(bufbc_t),
                  full(prm["w_conv_x"]), full(prm["b_conv_x"]), full(prm["w_conv_bc"]), full(prm["b_conv_bc"]),
                  full(prm["dt_bias"]), full(prm["a_log"]), full(sel)],
        out_specs=[pl.BlockSpec((m, SSD_D_INNER), lambda i: (0, 0)),
                   pl.BlockSpec((m, SSD_BC), lambda i: (0, 0)),
                   pl.BlockSpec((m, SSD_D_INNER), lambda i: (0, 0)),
                   pl.BlockSpec((m, SSD_D_INNER), lambda i: (0, 0))],
        out_shape=[jax.ShapeDtypeStruct((m, SSD_D_INNER), F32),
                   jax.ShapeDtypeStruct((m, SSD_BC), F32),
                   jax.ShapeDtypeStruct((m, SSD_D_INNER), F32),
                   jax.ShapeDtypeStruct((m, SSD_D_INNER), F32)],
        compiler_params=_cparams("arbitrary"),
    )(u, u, u_dt, bufx_t, bufbc_t, prm["w_conv_x"], prm["b_conv_x"], prm["w_conv_bc"], prm["b_conv_bc"],
      prm["dt_bias"], prm["a_log"], sel)


def _ssd_step_state_kernel(h0_ref, cols_ref, b_ref, c_ref, hn_ref, y_ref):
    nr = SSD_D_INNER // LANES
    pad = jnp.concatenate([cols_ref[...], jnp.zeros((LANES - 2 * nr, LANES), F32)], axis=0)
    ct = pad.T
    rows_per_group = SSD_D_INNER // SSD_GROUPS // LANES
    for r in range(nr):
        g = r // rows_per_group
        hn_ref[r * LANES:(r + 1) * LANES, :] = (ct[:, nr + r:nr + r + 1] * h0_ref[r * LANES:(r + 1) * LANES, :]
                                                + ct[:, r:r + 1] * b_ref[g:g + 1, :])
    res = _dot_nt(c_ref[...].astype(BF16), hn_ref[...].astype(BF16))
    gw = SSD_D_INNER // SSD_GROUPS
    y_ref[...] = jnp.concatenate([res[g:g + 1, g * gw:(g + 1) * gw] for g in range(SSD_GROUPS)], axis=1)


def _ssd_step_state(h0, cols, bmat, cmat):
    m = h0.shape[0]
    per = lambda shape: pl.BlockSpec((None,) + shape, lambda i: (i, 0, 0))
    return pl.pallas_call(
        _ssd_step_state_kernel, name="ssd_step_state",
        grid=(m,),
        in_specs=[per((SSD_D_INNER, SSD_STATE)), per((2 * SSD_D_INNER // LANES, LANES)),
                  per((8, SSD_STATE)), per((8, SSD_STATE))],
        out_specs=[per((SSD_D_INNER, SSD_STATE)), per((1, SSD_D_INNER))],
        out_shape=[jax.ShapeDtypeStruct((m, SSD_D_INNER, SSD_STATE), F32),
                   jax.ShapeDtypeStruct((m, 1, SSD_D_INNER), F32)],
        compiler_params=_cparams("parallel"),
    )(h0, cols, bmat, cmat)


def _ssd_step_post_kernel(y_ref, xs_ref, z_ref, dskip_ref, gnorm_ref, o_ref):
    o_ref[...] = _ssd_gate_norm(y_ref[...], xs_ref[...], z_ref[...], dskip_ref[...],
                                gnorm_ref[...]).astype(o_ref.dtype)


def _ssd_step_post(y, xs, u, prm):
    m = y.shape[0]
    full = lambda a: pl.BlockSpec(a.shape, lambda i: (0,) * a.ndim)
    return pl.pallas_call(
        _ssd_step_post_kernel, name="ssd_step_post",
        grid=(1,),
        in_specs=[full(y), full(xs), pl.BlockSpec((m, SSD_D_INNER), lambda i: (0, 0)),
                  full(prm["d_skip"]), full(prm["g_ssd_norm"])],
        out_specs=pl.BlockSpec((m, SSD_D_INNER), lambda i: (0, 0)),
        out_shape=jax.ShapeDtypeStruct((m, SSD_D_INNER), BF16),
        compiler_params=_cparams("arbitrary"),
    )(y, xs, u, prm["d_skip"], prm["g_ssd_norm"])


def _lru_gates(xc, wr_ref, br_ref, wi_ref, bi_ref, lam_ref, is_pos0):
    xb = xc.astype(BF16)
    rs, is_ = [], []
    for n in range(LRU_BLOCKS):
        blk = xb[:, n * LRU_BLOCK_DIM:(n + 1) * LRU_BLOCK_DIM]
        rs.append(_dot(blk, wr_ref[n]))
        is_.append(_dot(blk, wi_ref[n]))
    r = _sigmoid(jnp.concatenate(rs, axis=1) + br_ref[...])
    i = _sigmoid(jnp.concatenate(is_, axis=1) + bi_ref[...])
    log_a = -LRU_C * r * _softplus(-lam_ref[...])
    a = jnp.exp(log_a)
    th = jnp.tanh(log_a)
    mult = jnp.sqrt(-2.0 * th / (1.0 - th))
    if is_pos0 is not False:
        mult = jnp.where(is_pos0, 1.0, mult)
    return a, mult * i * xc


def _shift_rows(x, d, fill):
    rolled = pltpu.roll(x, d, 0)
    row = lax.broadcasted_iota(jnp.int32, x.shape, 0)
    return jnp.where(row >= d, rolled, fill)


def _lru_prompt_kernel(lx_ref, lg_ref, buf_ref, wc_ref, bc_ref, wr_ref, br_ref, wi_ref, bi_ref, lam_ref, h0_ref,
                       y_ref, hout_ref, ext_ref, h_ref, *, tc, nc, pos0):
    c = pl.program_id(1)
    first = c == 0

    @pl.when(first)
    def _():
        h_ref[...] = h0_ref[...]

    xc = _conv_chunk(ext_ref, lx_ref[...], buf_ref, wc_ref, bc_ref, first)
    pos = pos0 + c * tc + lax.broadcasted_iota(jnp.int32, (tc, 1), 0)
    a, b = _lru_gates(xc, wr_ref, br_ref, wi_ref, bi_ref, lam_ref, pos == 0)
    d = 1
    while d < tc:
        b = b + a * _shift_rows(b, d, 0.0)
        a = a * _shift_rows(a, d, 1.0)
        d *= 2
    h = b + a * h_ref[...]
    h_ref[...] = h[tc - 1:tc, :]
    y_ref[...] = (h * _gelu_tanh(lg_ref[...])).astype(y_ref.dtype)

    @pl.when(c == nc - 1)
    def _():
        hout_ref[...] = h[tc - 1:tc, :]


def _lru_prompt(u, buf, h0, prm, nb, t, pos0):
    tc = 256
    nc = t // tc
    full = lambda a: pl.BlockSpec(a.shape, lambda b, c: (0,) * a.ndim)
    y, hout = pl.pallas_call(
        functools.partial(_lru_prompt_kernel, tc=tc, nc=nc, pos0=pos0), name="lru_prompt",
        grid=(nb, nc),
        in_specs=[pl.BlockSpec((tc, LRU_WIDTH), lambda b, c: (b * nc + c, 5)),
                  pl.BlockSpec((tc, LRU_WIDTH), lambda b, c: (b * nc + c, 6)),
                  pl.BlockSpec((None, CONV_WIDTH - 1, LRU_WIDTH), lambda b, c: (b, 0, 0)),
                  full(prm["w_conv_lru"]), full(prm["b_conv_lru"]), full(prm["w_lru_r"]), full(prm["b_lru_r"]),
                  full(prm["w_lru_i"]), full(prm["b_lru_i"]), full(prm["lru_lambda"]),
                  pl.BlockSpec((None, 1, LRU_WIDTH), lambda b, c: (b, 0, 0))],
        out_specs=[pl.BlockSpec((tc, LRU_WIDTH), lambda b, c: (b * nc + c, 0)),
                   pl.BlockSpec((None, 1, LRU_WIDTH), lambda b, c: (b, 0, 0))],
        out_shape=[jax.ShapeDtypeStruct((nb * t, LRU_WIDTH), BF16),
                   jax.ShapeDtypeStruct((nb, 1, LRU_WIDTH), F32)],
        scratch_shapes=[pltpu.VMEM((CONV_HALO + tc, LRU_WIDTH), F32), pltpu.VMEM((1, LRU_WIDTH), F32)],
        compiler_params=_cparams("parallel", "arbitrary"),
    )(u, u, buf, prm["w_conv_lru"], prm["b_conv_lru"], prm["w_lru_r"], prm["b_lru_r"], prm["w_lru_i"],
      prm["b_lru_i"], prm["lru_lambda"], h0)
    return y, hout


def _lru_step_kernel(lx_ref, lg_ref, buf_ref, wc_ref, bc_ref, wr_ref, br_ref, wi_ref, bi_ref, lam_ref, h0_ref,
                     y_ref, hout_ref, *, pos0):
    xc = _conv_step(buf_ref, lx_ref[...], wc_ref, bc_ref)
    a, b = _lru_gates(xc, wr_ref, br_ref, wi_ref, bi_ref, lam_ref, pos0 == 0)
    h = b + a * h0_ref[...]
    hout_ref[...] = h
    y_ref[...] = (h * _gelu_tanh(lg_ref[...])).astype(y_ref.dtype)


def _lru_step(u, buf_t, h0, prm, pos0):
    m = h0.shape[0]
    full = lambda a: pl.BlockSpec(a.shape, lambda i: (0,) * a.ndim)
    return pl.pallas_call(
        functools.partial(_lru_step_kernel, pos0=pos0), name="lru_step",
        grid=(1,),
        in_specs=[pl.BlockSpec((m, LRU_WIDTH), lambda i: (0, 5)),
                  pl.BlockSpec((m, LRU_WIDTH), lambda i: (0, 6)),
                  full(buf_t), full(prm["w_conv_lru"]), full(prm["b_conv_lru"]), full(prm["w_lru_r"]),
                  full(prm["b_lru_r"]), full(prm["w_lru_i"]), full(prm["b_lru_i"]), full(prm["lru_lambda"]),
                  full(h0)],
        out_specs=[pl.BlockSpec((m, LRU_WIDTH), lambda i: (0, 0)),
                   pl.BlockSpec((m, LRU_WIDTH), lambda i: (0, 0))],
        out_shape=[jax.ShapeDtypeStruct((m, LRU_WIDTH), BF16),
                   jax.ShapeDtypeStruct((m, LRU_WIDTH), F32)],
        compiler_params=_cparams("arbitrary"),
    )(u, u, buf_t, prm["w_conv_lru"], prm["b_conv_lru"], prm["w_lru_r"], prm["b_lru_r"], prm["w_lru_i"],
      prm["b_lru_i"], prm["lru_lambda"], h0)


def _rope_tables(n_pos):
    half = MLA_ROPE // 2
    inv = ROPE_THETA ** (-jnp.arange(half, dtype=F32) * (2.0 / MLA_ROPE))
    ang = jnp.arange(n_pos, dtype=F32)[:, None] * inv[None, :]
    cos, sin = jnp.cos(ang), jnp.sin(ang)
    cc = jnp.concatenate([cos, cos], axis=1)
    ss = jnp.concatenate([-sin, sin], axis=1)
    return cc, ss


def _even_params(e, w_in_e, w_pool, s_pool, g_qlat, w_qb, g_kvlat, w_kvb, g_q, g_k, w_out_e):
    nh = MLA_HEADS
    w_in = jnp.pad(w_in_e[e], ((0, 0), (0, EVEN_IN_PAD - w_in_e.shape[2]))).astype(BF16)
    wq = w_qb[e].reshape(MLA_Q_LORA, nh, MLA_QK)
    wq_nope = wq[:, :, :MLA_NOPE].reshape(MLA_Q_LORA, nh * MLA_NOPE)
    wq_rope = jnp.pad(wq[:, :, MLA_NOPE:], ((0, 0), (0, 0), (0, LANES - MLA_ROPE))).reshape(MLA_Q_LORA, nh * LANES)
    wkv = w_kvb[e].reshape(MLA_KV_LORA, nh, MLA_NOPE + MLA_V)
    w_nope_t = jnp.transpose(wkv[:, :, :MLA_NOPE], (1, 2, 0))
    pad_g = lambda g: jnp.pad(g, (0, MLA_QKP - MLA_QK)).reshape(1, MLA_QKP)
    return dict(
        w_in=w_in,
        w_pool=w_pool[e].astype(BF16),
        s_pool=s_pool[e].reshape(1, POOL_WIDTH),
        g_qlat=g_qlat[e].reshape(1, MLA_Q_LORA),
        g_kvlat=g_kvlat[e].reshape(1, MLA_KV_LORA),
        w_qb=jnp.concatenate([wq_nope, wq_rope], axis=1).astype(BF16),
        w_kvb=w_kvb[e].astype(BF16),
        g_q=pad_g(g_q[e]), g_k=pad_g(g_k[e]),
        g_k_rope=g_k[e][MLA_NOPE:].reshape(MLA_ROPE, 1),
        w_nope_t=w_nope_t.astype(BF16),
        w_v=jnp.transpose(wkv[:, :, MLA_NOPE:], (1, 0, 2)).astype(BF16),
        w_out_pool=w_out_e[e][:POOL_WIDTH].astype(BF16),
        w_out_att=w_out_e[e][POOL_WIDTH:].astype(BF16),
    )


def _odd_params(o, w_in_o, w_conv_ssd, b_conv_ssd, dt_bias, a_log, d_skip, g_ssd_norm, w_conv_lru, b_conv_lru,
                w_lru_r, b_lru_r, w_lru_i, b_lru_i, lru_lambda, w_out_o):
    j2 = SSD_D_INNER + SSD_D_INNER + SSD_BC
    j3 = j2 + SSD_HEADS
    w = w_in_o[o]
    pad_heads = lambda v: jnp.pad(v, (0, LANES - SSD_HEADS)).reshape(1, LANES)
    return dict(
        w_in=jnp.concatenate([w[:, :j2], w[:, j3:]], axis=1).astype(BF16),
        w_in_dt=jnp.pad(w[:, j2:j3], ((0, 0), (0, LANES - SSD_HEADS))).astype(BF16),
        w_conv_x=w_conv_ssd[o][:, :SSD_D_INNER], w_conv_bc=w_conv_ssd[o][:, SSD_D_INNER:],
        b_conv_x=b_conv_ssd[o][:SSD_D_INNER].reshape(1, -1), b_conv_bc=b_conv_ssd[o][SSD_D_INNER:].reshape(1, -1),
        dt_bias=pad_heads(dt_bias[o]), a_log=pad_heads(a_log[o]),
        d_skip=jnp.repeat(d_skip[o], SSD_HEAD_DIM).reshape(1, SSD_D_INNER),
        g_ssd_norm=g_ssd_norm[o].reshape(1, SSD_D_INNER),
        w_conv_lru=w_conv_lru[o], b_conv_lru=b_conv_lru[o].reshape(1, LRU_WIDTH),
        w_lru_r=w_lru_r[o].astype(BF16), b_lru_r=b_lru_r[o].reshape(1, LRU_WIDTH),
        w_lru_i=w_lru_i[o].astype(BF16), b_lru_i=b_lru_i[o].reshape(1, LRU_WIDTH),
        lru_lambda=lru_lambda[o].reshape(1, LRU_WIDTH),
        w_out_ssd=w_out_o[o][:SSD_D_INNER].astype(BF16),
        w_out_lru=w_out_o[o][SSD_D_INNER:].astype(BF16),
    )


def _pad_tab(tab):
    return jnp.pad(tab, ((0, 0), (0, LANES - tab.shape[1])))


def _even_layer_prompt(rows, hn, prm, nb, t):
    u = _matmul(rows, [hn], [prm["w_in"]], name="in_even", tn=640)
    zero_buf = jnp.zeros((nb, POOL_BUF, POOL_WIDTH), F32)
    y_pool = _pool_prompt(u, zero_buf, prm["w_pool"], prm["s_pool"], nb, t, 0)
    cc, ss = _rope_tables(t)
    tm = 256
    tiles = t // tm
    q, k, v, lat = _mla_prep(u, rows.m, tm, lambda i: i % tiles, (_pad_tab(cc), _pad_tab(ss)), prm, False)
    y_att = _flash_attention(q, k, v, nb, t)
    u3 = u.reshape(nb, t, EVEN_IN_PAD)
    kr = u3[:, :, POOL_WIDTH + MLA_Q_LORA + MLA_KV_LORA:POOL_WIDTH + MLA_Q_LORA + MLA_KV_LORA + MLA_ROPE]
    mla_rows = jnp.concatenate([lat.reshape(nb, t, MLA_KV_LORA), kr], axis=-1)
    pool_new = u3[:, t - POOL_BUF:, :POOL_WIDTH]
    return [y_pool, y_att], [prm["w_out_pool"], prm["w_out_att"]], mla_rows, pool_new


def _even_layer_sample(rows, hn, prm, pool_buf, cache, layer, page_table, pos0):
    m = rows.m
    nh = MLA_HEADS
    u = _matmul(rows, [hn], [prm["w_in"]], name="in_even", tn=640)
    y_pool = _pool_step(u, jnp.transpose(pool_buf, (1, 0, 2)), prm["w_pool"], prm["s_pool"], pos0)
    cc, ss = _rope_tables(pos0 + 1)
    cc_new = jnp.broadcast_to(_pad_tab(cc[pos0:]), (m, LANES))
    ss_new = jnp.broadcast_to(_pad_tab(ss[pos0:]), (m, LANES))
    q, k, v, lat, qg = _mla_prep(u, m, m, lambda i: 0, (cc_new, ss_new), prm, True)
    qp = _bmm_heads(qg, prm["w_nope_t"], BF16)
    qp = jnp.pad(jnp.transpose(qp, (1, 0, 2)), ((0, 0), (0, 16 - nh), (0, 0)))
    q_m = jnp.transpose(q, (1, 0, 2))
    q_rope = q_m[:, :, MLA_NOPE:MLA_NOPE + MLA_ROPE]
    qr = jnp.concatenate([q_rope, jnp.zeros((m, nh, LANES - MLA_ROPE), BF16)], axis=2)
    ones_row = jnp.concatenate([jnp.zeros((m, 1, MLA_ROPE), BF16), jnp.ones((m, 1, LANES - MLA_ROPE), BF16)], axis=2)
    qr = jnp.concatenate([qr, ones_row, jnp.zeros((m, 16 - nh - 1, LANES), BF16)], axis=1)
    o_lat = _mla_decode(cache, layer, page_table, prm["w_nope_t"].reshape(nh * MLA_NOPE, MLA_KV_LORA), qp, qr,
                        prm["g_k_rope"], cc[:pos0].T, ss[:pos0].T, q_m.astype(F32),
                        jnp.transpose(k, (1, 0, 2)).astype(F32), lat.reshape(m, 1, MLA_KV_LORA))
    y_att = _bmm_heads(jnp.transpose(o_lat, (1, 0, 2)).astype(BF16), prm["w_v"], BF16)
    y_att = jnp.transpose(y_att, (1, 0, 2)).reshape(m, nh * MLA_V)
    kr = u[:, POOL_WIDTH + MLA_Q_LORA + MLA_KV_LORA:POOL_WIDTH + MLA_Q_LORA + MLA_KV_LORA + MLA_ROPE]
    mla_rows = jnp.concatenate([lat, kr], axis=-1).reshape(m, 1, MLA_KV_LORA + MLA_ROPE)
    pool_new = jnp.concatenate([pool_buf[:, 1:], u[:, None, :POOL_WIDTH]], axis=1)
    return [y_pool, y_att], [prm["w_out_pool"], prm["w_out_att"]], mla_rows, pool_new


def _odd_layer_prompt(rows, hn, prm, nb, t):
    u = _matmul(rows, [hn], [prm["w_in"]], name="in_odd", tn=1024)
    u_dt = _matmul(rows, [hn], [prm["w_in_dt"]], name="in_odd_dt", tn=LANES)
    k1 = CONV_WIDTH - 1
    y_ssd, h_ssd = _ssd_prompt(u, u_dt, jnp.zeros((nb, k1, SSD_D_INNER), F32), jnp.zeros((nb, k1, SSD_BC), F32),
                               jnp.zeros((nb, SSD_D_INNER, SSD_STATE), F32), prm, nb, t)
    y_lru, h_lru = _lru_prompt(u, jnp.zeros((nb, k1, LRU_WIDTH), F32), jnp.zeros((nb, 1, LRU_WIDTH), F32),
                               prm, nb, t, 0)
    u3 = u.reshape(nb, t, -1)
    sconv = u3[:, t - k1:, SSD_D_INNER:2 * SSD_D_INNER + SSD_BC]
    lconv = u3[:, t - k1:, 2 * SSD_D_INNER + SSD_BC:2 * SSD_D_INNER + SSD_BC + LRU_WIDTH]
    return ([y_ssd, y_lru], [prm["w_out_ssd"], prm["w_out_lru"]], sconv,
            h_ssd.reshape(nb, SSD_HEADS, SSD_HEAD_DIM, SSD_STATE), lconv, h_lru.reshape(nb, LRU_WIDTH))


def _odd_layer_sample(rows, hn, prm, sconv_buf, ssd_state, lconv_buf, lru_state, pos0):
    m = rows.m
    u = _matmul(rows, [hn], [prm["w_in"]], name="in_odd", tn=1024)
    u_dt = _matmul(rows, [hn], [prm["w_in_dt"]], name="in_odd_dt", tn=LANES)
    sconv_t = jnp.transpose(sconv_buf, (1, 0, 2))
    xs, bcs, dtx, dae = _ssd_step_prep(u, u_dt, sconv_t[:, :, :SSD_D_INNER], sconv_t[:, :, SSD_D_INNER:], prm)
    nr = SSD_D_INNER // LANES
    cols = jnp.concatenate([dtx.reshape(m, nr, LANES), dae.reshape(m, nr, LANES)], axis=1)
    gs = SSD_GROUPS * SSD_STATE
    pad8 = lambda a: jnp.pad(a.reshape(m, SSD_GROUPS, SSD_STATE), ((0, 0), (0, 8 - SSD_GROUPS), (0, 0)))
    h_new, y = _ssd_step_state(ssd_state.reshape(m, SSD_D_INNER, SSD_STATE), cols,
                               pad8(bcs[:, :gs]), pad8(bcs[:, gs:]))
    y_ssd = _ssd_step_post(y.reshape(m, SSD_D_INNER), xs, u, prm)
    y_lru, h_lru = _lru_step(u, jnp.transpose(lconv_buf, (1, 0, 2)), lru_state, prm, pos0)
    u_xbc = u[:, SSD_D_INNER:2 * SSD_D_INNER + SSD_BC]
    u_lx = u[:, 2 * SSD_D_INNER + SSD_BC:2 * SSD_D_INNER + SSD_BC + LRU_WIDTH]
    sconv = jnp.concatenate([sconv_buf[:, 1:], u_xbc[:, None]], axis=1)
    lconv = jnp.concatenate([lconv_buf[:, 1:], u_lx[:, None]], axis=1)
    return ([y_ssd, y_lru], [prm["w_out_ssd"], prm["w_out_lru"]], sconv,
            h_new.reshape(m, SSD_HEADS, SSD_HEAD_DIM, SSD_STATE), lconv, h_lru)


def kernel(x_prompt, x_sample, cache_mla, state_pool, state_ssd_conv, state_ssd, state_lru_conv, state_lru, page_table, c_prompt, c_sample, g_norm1, g_norm2, w_mod, b_mod, w_mlp1, w_mlp2, w_in_e, w_pool, s_pool, g_qlat, w_qb, g_kvlat, w_kvb, g_q, g_k, w_out_e, w_in_o, w_conv_ssd, b_conv_ssd, dt_bias, a_log, d_skip, g_ssd_norm, w_conv_lru, b_conv_lru, w_lru_r, b_lru_r, w_lru_i, b_lru_i, lru_lambda, w_out_o):
    nb, t, _ = x_prompt.shape
    ns = x_sample.shape[0]
    assert x_sample.shape[1] == 1 and t >= POOL_BUF
    pos0_s = page_table.shape[1] * PAGE_SIZE

    pad_rows = (-(ns + nb)) % 8
    c_all = jnp.concatenate([c_sample, c_prompt, jnp.zeros((pad_rows, D_MODEL), F32)], axis=0)
    mod_all = _modulation(c_all, w_mod, b_mod)

    rows_p = _Rows(nb, t, 512)
    rows_s = _Rows(ns, 1, ns)
    xp = x_prompt.reshape(nb * t, D_MODEL)
    xs = x_sample.reshape(ns, D_MODEL)
    w1 = w_mlp1.astype(BF16)
    w2 = w_mlp2.astype(BF16)

    outs_p, outs_s = {}, {}
    for layer in range(DEPTH):
        mod_p = rows_p.mod_array(mod_all[layer, ns:ns + nb])
        mod_s = rows_s.mod_array(mod_all[layer, :ns])
        hn_p = _norm_mod(rows_p, xp, g_norm1[layer], mod_p, 1, 0)
        hn_s = _norm_mod(rows_s, xs, g_norm1[layer], mod_s, 1, 0)
        if layer % 2 == 0:
            e = layer // 2
            prm = _even_params(e, w_in_e, w_pool, s_pool, g_qlat, w_qb, g_kvlat, w_kvb, g_q, g_k, w_out_e)
            a_p, w_o, mla_p, pool_p = _even_layer_prompt(rows_p, hn_p, prm, nb, t)
            a_s, _, mla_s, pool_s = _even_layer_sample(rows_s, hn_s, prm, state_pool[e], cache_mla, e,
                                                       page_table, pos0_s)
            outs_p.setdefault("mla", []).append(mla_p)
            outs_p.setdefault("pool", []).append(pool_p)
            outs_s.setdefault("mla", []).append(mla_s)
            outs_s.setdefault("pool", []).append(pool_s)
        else:
            o = layer // 2
            prm = _odd_params(o, w_in_o, w_conv_ssd, b_conv_ssd, dt_bias, a_log, d_skip, g_ssd_norm, w_conv_lru,
                              b_conv_lru, w_lru_r, b_lru_r, w_lru_i, b_lru_i, lru_lambda, w_out_o)
            a_p, w_o, sconv_p, ssd_p, lconv_p, lru_p = _odd_layer_prompt(rows_p, hn_p, prm, nb, t)
            a_s, _, sconv_s, ssd_s, lconv_s, lru_s = _odd_layer_sample(
                rows_s, hn_s, prm, state_ssd_conv[o], state_ssd[o], state_lru_conv[o], state_lru[o], pos0_s)
            for d, vals in ((outs_p, (sconv_p, ssd_p, lconv_p, lru_p)), (outs_s, (sconv_s, ssd_s, lconv_s, lru_s))):
                for name, val in zip(("sconv", "ssd", "lconv", "lru"), vals):
                    d.setdefault(name, []).append(val)
        xp = _matmul(rows_p, a_p, w_o, name="out_proj", tn=1024, res=xp, mod=mod_p, gate_chunk=2)
        xs = _matmul(rows_s, a_s, w_o, name="out_proj_s", tn=1024, res=xs, mod=mod_s, gate_chunk=2)
        hn2_p = _norm_mod(rows_p, xp, g_norm2[layer], mod_p, 4, 3)
        hn2_s = _norm_mod(rows_s, xs, g_norm2[layer], mod_s, 4, 3)
        act_p = _matmul(rows_p, [hn2_p], [w1[layer]], name="mlp1", tn=1024, act="relu2", out_dtype=BF16)
        act_s = _matmul(rows_s, [hn2_s], [(w_mlp1, layer)], name="mlp1_s", tn=1024, act="relu2", out_dtype=BF16)
        xp = _matmul(rows_p, [act_p], [w2[layer]], name="mlp2", tn=1024, tk=2048, res=xp, mod=mod_p, gate_chunk=5)
        xs = _matmul(rows_s, [act_s], [(w_mlp2, layer)], name="mlp2_s", tn=1024, tk=2048, res=xs, mod=mod_s,
                     gate_chunk=5)

    st = lambda d, name: jnp.stack(d[name])
    return (xp.reshape(nb, t, D_MODEL), xs.reshape(ns, 1, D_MODEL),
            st(outs_p, "mla"), st(outs_s, "mla"), st(outs_p, "pool"), st(outs_s, "pool"),
            st(outs_p, "sconv"), st(outs_s, "sconv"), st(outs_p, "ssd"), st(outs_s, "ssd"),
            st(outs_p, "lconv"), st(outs_s, "lconv"), st(outs_p, "lru"), st(outs_s, "lru"))
```

```python
import functools

import jax
import jax.numpy as jnp
from jax import lax
from jax.experimental import pallas as pl
from jax.experimental.pallas import tpu as pltpu

F32 = jnp.float32
BF16 = jnp.bfloat16

VMEM_LIMIT_BYTES = 48 * 1024 * 1024
LANES = 128

D_MODEL = 2048
EPS = 1e-6
N_MOD = 6
DEPTH = 2

POOL_WIDTH = 1024
POOL_WINDOWS = (2, 4, 8, 16)
POOL_GROUP_DIM = 256
POOL_BUF = 15

MLA_HEADS = 8
MLA_NOPE = 128
MLA_ROPE = 64
MLA_V = 128
MLA_QK = 192
MLA_QKP = 256
MLA_Q_LORA = 512
MLA_KV_LORA = 256
ROPE_THETA = 10000.0
PAGE_SIZE = 128
EVEN_IN_PAD = 1920

SSD_D_INNER = 2048
SSD_HEAD_DIM = 64
SSD_HEADS = 32
SSD_GROUPS = 4
SSD_STATE = 128
SSD_CHUNK = 128
SSD_BC = 1024
CONV_WIDTH = 4

LRU_WIDTH = 1024
LRU_BLOCKS = 8
LRU_BLOCK_DIM = 128
LRU_C = 8.0

MLP_HIDDEN = 8192


def _cparams(*sem):
    return pltpu.CompilerParams(dimension_semantics=sem, vmem_limit_bytes=VMEM_LIMIT_BYTES)


def _sigmoid(x):
    return 1.0 / (1.0 + jnp.exp(-x))


def _silu(x):
    return x * _sigmoid(x)


def _softplus(x):
    return jnp.maximum(x, 0.0) + jnp.log1p(jnp.exp(-jnp.abs(x)))


def _gelu_tanh(x):
    return 0.5 * x * (1.0 + jnp.tanh(0.7978845608028654 * (x + 0.044715 * (x * x * x))))


def _dot(a, b):
    return jnp.dot(a, b, preferred_element_type=F32)


def _dot_nt(a, b):
    return lax.dot_general(a, b, (((1,), (1,)), ((), ())), preferred_element_type=F32)


def _split3(x):
    hi = x.astype(BF16)
    r = x - hi.astype(F32)
    mid = r.astype(BF16)
    lo = (r - mid.astype(F32)).astype(BF16)
    return hi, mid, lo


def _dot_sel(x, sel):
    hi, mid, lo = _split3(x)
    return _dot(hi, sel) + _dot(mid, sel) + _dot(lo, sel)


def _mod_kernel(c_ref, w_ref, b_ref, o_ref):
    c = c_ref[...]
    o_ref[...] = _dot(_silu(c).astype(BF16), w_ref[...].astype(BF16)) + b_ref[...]


def _modulation(c_all, w_mod, b_mod):
    mp = c_all.shape[0]
    n = N_MOD * D_MODEL
    tn = 1024
    return pl.pallas_call(
        _mod_kernel, name="modulation",
        grid=(DEPTH, n // tn),
        in_specs=[pl.BlockSpec((mp, D_MODEL), lambda l, j: (0, 0)),
                  pl.BlockSpec((None, D_MODEL, tn), lambda l, j: (l, 0, j)),
                  pl.BlockSpec((None, 1, tn), lambda l, j: (l, 0, j))],
        out_specs=pl.BlockSpec((None, mp, tn), lambda l, j: (l, 0, j)),
        out_shape=jax.ShapeDtypeStruct((DEPTH, mp, n), F32),
        compiler_params=_cparams("parallel", "parallel"),
    )(c_all, w_mod, b_mod.reshape(DEPTH, 1, n))


class _Rows:
    def __init__(self, nb, t, tm):
        self.nb, self.t, self.tm = nb, t, tm
        self.m = nb * t
        self.per_seq = t > 1
        if self.per_seq:
            assert t % tm == 0
            self.tiles_per_seq = t // tm

    def mod_array(self, mod_rows):
        return mod_rows.reshape(self.nb, 1, -1) if self.per_seq else mod_rows

    def mod_spec(self, chunk, width, ncol, col_of):
        per_chunk = D_MODEL // width
        if self.per_seq:
            tps = self.tiles_per_seq
            return pl.BlockSpec((None, 1, width),
                                lambda *g: (g[0] // tps, 0, chunk * per_chunk + col_of(*g)))
        return pl.BlockSpec((self.tm, width), lambda *g: (g[0], chunk * per_chunk + col_of(*g)))


def _norm_mod_kernel(x_ref, g_ref, sc_ref, sh_ref, o_ref):
    x = x_ref[...]
    y = x * lax.rsqrt(jnp.mean(x * x, axis=-1, keepdims=True) + EPS) * g_ref[...]
    o_ref[...] = (y * (1.0 + sc_ref[...]) + sh_ref[...]).astype(o_ref.dtype)


def _norm_mod(rows, x, g, mod, sc_chunk, sh_chunk):
    tm = min(rows.tm, 512)
    r = _Rows(rows.nb, rows.t, tm)
    return pl.pallas_call(
        _norm_mod_kernel, name="norm_mod",
        grid=(r.m // tm,),
        in_specs=[pl.BlockSpec((tm, D_MODEL), lambda i: (i, 0)),
                  pl.BlockSpec((1, D_MODEL), lambda i: (0, 0)),
                  r.mod_spec(sc_chunk, D_MODEL, 1, lambda i: 0),
                  r.mod_spec(sh_chunk, D_MODEL, 1, lambda i: 0)],
        out_specs=pl.BlockSpec((tm, D_MODEL), lambda i: (i, 0)),
        out_shape=jax.ShapeDtypeStruct((r.m, D_MODEL), BF16),
        compiler_params=_cparams("parallel"),
    )(x, g.reshape(1, D_MODEL), mod, mod)


def _mm_kernel(*refs, n_a, nk, act, has_res):
    a_refs = refs[:n_a]
    w_refs = refs[n_a:2 * n_a]
    pos = 2 * n_a
    if has_res:
        x_ref, gt_ref = refs[pos], refs[pos + 1]
        pos += 2
    o_ref = refs[pos]
    acc_ref = refs[pos + 1] if nk > 1 else None

    part = _dot(a_refs[0][...], w_refs[0][...].astype(BF16))
    for a_ref, w_ref in zip(a_refs[1:], w_refs[1:]):
        part = part + _dot(a_ref[...], w_ref[...].astype(BF16))

    def finish(acc):
        if act == "relu2":
            acc = jnp.square(jnp.maximum(acc, 0.0))
        if has_res:
            acc = x_ref[...] + gt_ref[...] * acc
        o_ref[...] = acc.astype(o_ref.dtype)

    if nk == 1:
        finish(part)
    else:
        k = pl.program_id(2)

        @pl.when(k == 0)
        def _():
            acc_ref[...] = part

        @pl.when(k > 0)
        def _():
            acc_ref[...] += part

        @pl.when(k == nk - 1)
        def _():
            finish(acc_ref[...])


def _matmul(rows, a_list, w_list, *, name, tn, tk=None, act=None, res=None, mod=None, gate_chunk=None,
            out_dtype=F32, alias_res=False, tm=None):
    assert tm is None or res is None
    tm, m = (tm or rows.tm), rows.m
    n = (w_list[0][0] if isinstance(w_list[0], tuple) else w_list[0]).shape[-1]
    n_a = len(a_list)
    if tk is None or n_a > 1:
        nk = 1
    else:
        assert a_list[0].shape[1] % tk == 0
        nk = a_list[0].shape[1] // tk
    assert n % tn == 0 and m % tm == 0
    in_specs, args = [], []
    for a in a_list:
        kk = a.shape[1] if nk == 1 else tk
        in_specs.append(pl.BlockSpec((tm, kk), lambda i, j, k: (i, k)))
        args.append(a)
    for w in w_list:
        if isinstance(w, tuple):
            w, layer = w
            kk = w.shape[1] if nk == 1 else tk
            in_specs.append(pl.BlockSpec((None, kk, tn), lambda i, j, k, layer=layer: (layer, k, j)))
        else:
            kk = w.shape[0] if nk == 1 else tk
            in_specs.append(pl.BlockSpec((kk, tn), lambda i, j, k: (k, j)))
        args.append(w)
    has_res = res is not None
    if has_res:
        in_specs.append(pl.BlockSpec((tm, tn), lambda i, j, k: (i, j)))
        in_specs.append(rows.mod_spec(gate_chunk, tn, n // tn, lambda i, j, k: j))
        args += [res, mod]
    scratch = [pltpu.VMEM((tm, tn), F32)] if nk > 1 else []
    aliases = {2 * n_a: 0} if (has_res and alias_res) else {}
    return pl.pallas_call(
        functools.partial(_mm_kernel, n_a=n_a, nk=nk, act=act, has_res=has_res), name=name,
        input_output_aliases=aliases,
        grid=(m // tm, n // tn, nk),
        in_specs=in_specs,
        out_specs=pl.BlockSpec((tm, tn), lambda i, j, k: (i, j)),
        out_shape=jax.ShapeDtypeStruct((m, n), out_dtype),
        scratch_shapes=scratch,
        compiler_params=_cparams("parallel", "parallel", "arbitrary"),
    )(*args)


def _mm_ws_kernel(a_ref, w_ref, o_ref, wb_ref, *, act):
    @pl.when(pl.program_id(1) == 0)
    def _():
        wb_ref[...] = w_ref[...].astype(BF16)

    acc = _dot(a_ref[...], wb_ref[...])
    if act == "relu2":
        acc = jnp.square(jnp.maximum(acc, 0.0))
    o_ref[...] = acc.astype(o_ref.dtype)


def _matmul_ws(a, w_stack, layer, *, name, tm, tn, act=None, out_dtype=F32):
    m, k = a.shape
    n = w_stack.shape[-1]
    assert m % tm == 0 and n % tn == 0
    return pl.pallas_call(
        functools.partial(_mm_ws_kernel, act=act), name=name,
        grid=(n // tn, m // tm),
        in_specs=[pl.BlockSpec((tm, k), lambda j, i: (i, 0)),
                  pl.BlockSpec((None, k, tn), lambda j, i: (layer, 0, j))],
        out_specs=pl.BlockSpec((tm, tn), lambda j, i: (i, j)),
        out_shape=jax.ShapeDtypeStruct((m, n), out_dtype),
        scratch_shapes=[pltpu.VMEM((k, tn), BF16)],
        compiler_params=_cparams("parallel", "arbitrary"),
    )(a, w_stack)


def _bmm_kernel(a_ref, w_ref, o_ref):
    o_ref[...] = _dot(a_ref[...], w_ref[...]).astype(o_ref.dtype)


def _bmm_heads(a, w, out_dtype):
    h, m, k = a.shape
    n = w.shape[2]
    return pl.pallas_call(
        _bmm_kernel, name="bmm_heads",
        grid=(h,),
        in_specs=[pl.BlockSpec((None, m, k), lambda i: (i, 0, 0)),
                  pl.BlockSpec((None, k, n), lambda i: (i, 0, 0))],
        out_specs=pl.BlockSpec((None, m, n), lambda i: (i, 0, 0)),
        out_shape=jax.ShapeDtypeStruct((h, m, n), out_dtype),
        compiler_params=_cparams("parallel"),
    )(a, w)


CONV_HALO = 8


def _conv_chunk(ext_ref, u, buf_ref, w_ref, b_ref, first):
    tc = u.shape[0]
    k1 = CONV_WIDTH - 1

    @pl.when(first)
    def _():
        ext_ref[CONV_HALO - k1:CONV_HALO, :] = buf_ref[...]

    ext_ref[CONV_HALO:CONV_HALO + tc, :] = u
    y = b_ref[...] + w_ref[k1:k1 + 1, :] * u
    for k in range(k1):
        y = y + w_ref[k:k + 1, :] * ext_ref[CONV_HALO - k1 + k:CONV_HALO - k1 + k + tc, :]
    ext_ref[CONV_HALO - k1:CONV_HALO, :] = ext_ref[CONV_HALO + tc - k1:CONV_HALO + tc, :]
    return y


def _conv_step(buf_ref, u, w_ref, b_ref):
    y = b_ref[...] + w_ref[CONV_WIDTH - 1:CONV_WIDTH, :] * u
    for k in range(CONV_WIDTH - 1):
        y = y + w_ref[k:k + 1, :] * buf_ref[k]
    return y


def _pool_prompt_kernel(u_ref, buf_ref, w_ref, s_ref, o_ref, ext_ref, *, tc, pos0):
    c = pl.program_id(1)
    halo = 16

    @pl.when(c == 0)
    def _():
        ext_ref[0:1, :] = jnp.zeros((1, POOL_WIDTH), F32)
        ext_ref[1:halo, :] = buf_ref[...]

    u = u_ref[...]
    ext_ref[halo:halo + tc, :] = u
    pos = (pos0 + c * tc + lax.broadcasted_iota(jnp.int32, (tc, 1), 0)).astype(F32)
    outs = []
    for g, w in enumerate(POOL_WINDOWS):
        lo, hi = g * POOL_GROUP_DIM, (g + 1) * POOL_GROUP_DIM
        x = u[:, lo:hi]
        wsum = x
        for j in range(1, w):
            wsum = wsum + ext_ref[halo - j:halo - j + tc, lo:hi]
        cnt = jnp.minimum(pos + 1.0, float(w))
        d = (wsum / cnt - x).astype(BF16)
        outs.append(_dot(d, w_ref[g]))
    y = jnp.concatenate(outs, axis=1) * s_ref[...]
    o_ref[...] = y.astype(o_ref.dtype)
    ext_ref[0:halo, :] = ext_ref[tc:tc + halo, :]


def _pool_prompt(u, buf, w_pool, s_pool, nb, t, pos0):
    tc = 256
    nc = t // tc
    return pl.pallas_call(
        functools.partial(_pool_prompt_kernel, tc=tc, pos0=pos0), name="pool_prompt",
        grid=(nb, nc),
        in_specs=[pl.BlockSpec((tc, POOL_WIDTH), lambda b, c: (b * nc + c, 0)),
                  pl.BlockSpec((None, POOL_BUF, POOL_WIDTH), lambda b, c: (b, 0, 0)),
                  pl.BlockSpec((len(POOL_WINDOWS), POOL_GROUP_DIM, POOL_GROUP_DIM), lambda b, c: (0, 0, 0)),
                  pl.BlockSpec((1, POOL_WIDTH), lambda b, c: (0, 0))],
        out_specs=pl.BlockSpec((tc, POOL_WIDTH), lambda b, c: (b * nc + c, 0)),
        out_shape=jax.ShapeDtypeStruct((nb * t, POOL_WIDTH), BF16),
        scratch_shapes=[pltpu.VMEM((16 + tc, POOL_WIDTH), F32)],
        compiler_params=_cparams("parallel", "arbitrary"),
    )(u, buf, w_pool, s_pool)


def _pool_step_kernel(u_ref, buf_ref, w_ref, s_ref, o_ref, *, pos0):
    outs = []
    for g, w in enumerate(POOL_WINDOWS):
        lo, hi = g * POOL_GROUP_DIM, (g + 1) * POOL_GROUP_DIM
        x = u_ref[:, lo:hi]
        wsum = x
        for j in range(1, w):
            wsum = wsum + buf_ref[POOL_BUF - j, :, lo:hi]
        cnt = min(pos0 + 1.0, float(w))
        d = (wsum / cnt - x).astype(BF16)
        outs.append(_dot(d, w_ref[g]))
    o_ref[...] = (jnp.concatenate(outs, axis=1) * s_ref[...]).astype(o_ref.dtype)


def _pool_step(u, buf_t, w_pool, s_pool, pos0):
    m = buf_t.shape[1]
    return pl.pallas_call(
        functools.partial(_pool_step_kernel, pos0=pos0), name="pool_step",
        grid=(1,),
        in_specs=[pl.BlockSpec((m, POOL_WIDTH), lambda i: (0, 0)),
                  pl.BlockSpec((POOL_BUF, m, POOL_WIDTH), lambda i: (0, 0, 0)),
                  pl.BlockSpec((len(POOL_WINDOWS), POOL_GROUP_DIM, POOL_GROUP_DIM), lambda i: (0, 0, 0)),
                  pl.BlockSpec((1, POOL_WIDTH), lambda i: (0, 0))],
        out_specs=pl.BlockSpec((m, POOL_WIDTH), lambda i: (0, 0)),
        out_shape=jax.ShapeDtypeStruct((m, POOL_WIDTH), BF16),
        compiler_params=_cparams("arbitrary"),
    )(u, buf_t, w_pool, s_pool)


def _rope128(x, cc, ss):
    lane = lax.broadcasted_iota(jnp.int32, x.shape, 1)
    swapped = jnp.where(lane < MLA_ROPE // 2, pltpu.roll(x, LANES - MLA_ROPE // 2, 1),
                        pltpu.roll(x, MLA_ROPE // 2, 1))
    return x * cc + swapped * ss


def _mla_prep_kernel(uq_ref, ukv_ref, ukr_ref, gql_ref, gkvl_ref, wq_ref, wkv_ref, gq_ref, gk_ref,
                     cc_ref, ss_ref, *out_refs, emit_qg):
    if emit_qg:
        q_ref, k_ref, v_ref, lat_ref, qg_ref = out_refs
    else:
        q_ref, k_ref, v_ref, lat_ref = out_refs
    cc = cc_ref[...]
    ss = ss_ref[...]
    scale = MLA_QK ** -0.5

    uq = uq_ref[...]
    qn = uq * lax.rsqrt(jnp.mean(uq * uq, axis=-1, keepdims=True) + EPS) * gql_ref[...]
    qall = _dot(qn.astype(BF16), wq_ref[...])
    ukv = ukv_ref[...]
    lat = ukv * lax.rsqrt(jnp.mean(ukv * ukv, axis=-1, keepdims=True) + EPS) * gkvl_ref[...]
    lat_ref[...] = lat
    kvall = _dot(lat.astype(BF16), wkv_ref[...])
    kr = ukr_ref[...]
    kr_ss = jnp.sum(kr * kr, axis=-1, keepdims=True)

    gq_n, gq_r = gq_ref[:, :MLA_NOPE], gq_ref[:, MLA_NOPE:]
    gk_n, gk_r = gk_ref[:, :MLA_NOPE], gk_ref[:, MLA_NOPE:]
    nh = MLA_HEADS
    for h in range(nh):
        qnope = qall[:, h * 128:(h + 1) * 128]
        qrope = qall[:, (nh + h) * 128:(nh + h + 1) * 128]
        ssq = jnp.sum(qnope * qnope, axis=-1, keepdims=True) + jnp.sum(qrope * qrope, axis=-1, keepdims=True)
        rs = lax.rsqrt(ssq * (1.0 / MLA_QK) + EPS) * scale
        qn_h = qnope * rs * gq_n
        qr_h = _rope128(qrope * rs * gq_r, cc, ss)
        q_ref[h] = jnp.concatenate([qn_h, qr_h], axis=1).astype(q_ref.dtype)
        if emit_qg:
            qg_ref[h] = (qn_h * gk_n).astype(qg_ref.dtype)

        knope = kvall[:, h * 256:h * 256 + 128]
        ssk = jnp.sum(knope * knope, axis=-1, keepdims=True) + kr_ss
        rsk = lax.rsqrt(ssk * (1.0 / MLA_QK) + EPS)
        kr_h = _rope128(kr * rsk * gk_r, cc, ss)
        k_ref[h] = jnp.concatenate([knope * rsk * gk_n, kr_h], axis=1).astype(k_ref.dtype)
        v_ref[h] = kvall[:, h * 256 + 128:(h + 1) * 256].astype(v_ref.dtype)


def _mla_prep(u, m, tm, pos_of_tile, tabs, prm, emit_qg):
    cc_tab, ss_tab = tabs
    nh = MLA_HEADS
    out_shape = [jax.ShapeDtypeStruct((nh, m, MLA_QKP), BF16),
                 jax.ShapeDtypeStruct((nh, m, MLA_QKP), BF16),
                 jax.ShapeDtypeStruct((nh, m, MLA_V), BF16),
                 jax.ShapeDtypeStruct((m, MLA_KV_LORA), F32)]
    out_specs = [pl.BlockSpec((nh, tm, MLA_QKP), lambda i: (0, i, 0)),
                 pl.BlockSpec((nh, tm, MLA_QKP), lambda i: (0, i, 0)),
                 pl.BlockSpec((nh, tm, MLA_V), lambda i: (0, i, 0)),
                 pl.BlockSpec((tm, MLA_KV_LORA), lambda i: (i, 0))]
    if emit_qg:
        out_shape.append(jax.ShapeDtypeStruct((nh, m, MLA_NOPE), BF16))
        out_specs.append(pl.BlockSpec((nh, tm, MLA_NOPE), lambda i: (0, i, 0)))
    full = lambda a: pl.BlockSpec(a.shape, lambda i: (0,) * a.ndim)
    return pl.pallas_call(
        functools.partial(_mla_prep_kernel, emit_qg=emit_qg), name="mla_prep",
        grid=(m // tm,),
        in_specs=[pl.BlockSpec((tm, MLA_Q_LORA), lambda i: (i, POOL_WIDTH // MLA_Q_LORA)),
                  pl.BlockSpec((tm, MLA_KV_LORA), lambda i: (i, (POOL_WIDTH + MLA_Q_LORA) // MLA_KV_LORA)),
                  pl.BlockSpec((tm, LANES), lambda i: (i, (POOL_WIDTH + MLA_Q_LORA + MLA_KV_LORA) // LANES)),
                  full(prm["g_qlat"]), full(prm["g_kvlat"]), full(prm["w_qb"]), full(prm["w_kvb"]),
                  full(prm["g_q"]), full(prm["g_k"]),
                  pl.BlockSpec((tm, LANES), lambda i: (pos_of_tile(i), 0)),
                  pl.BlockSpec((tm, LANES), lambda i: (pos_of_tile(i), 0))],
        out_specs=out_specs,
        out_shape=out_shape,
        compiler_params=_cparams("parallel"),
    )(u, u, u, prm["g_qlat"], prm["g_kvlat"], prm["w_qb"], prm["w_kvb"], prm["g_q"], prm["g_k"],
      cc_tab, ss_tab)


def _flash_kernel(q_ref, k_ref, v_ref, o_ref, *, tq, tk):
    i = pl.program_id(2)
    q = q_ref[...]
    per_q = tq // tk

    def step(j, carry, band):
        m, l, acc = carry
        start = pl.multiple_of(j * tk, tk)
        kb = k_ref[pl.ds(start, tk), :]
        vb = v_ref[pl.ds(start, tk), :]
        s = _dot_nt(q, kb)
        if band is not None:
            r = lax.broadcasted_iota(jnp.int32, (tq, tk), 0)
            c = lax.broadcasted_iota(jnp.int32, (tq, tk), 1) + band * tk
            s = jnp.where(c <= r, s, -jnp.inf)
        m_new = jnp.maximum(m, jnp.max(s, axis=1, keepdims=True))
        alpha = jnp.exp(m - m_new)
        p = jnp.exp(s - m_new)
        l = alpha * l + jnp.sum(p, axis=1, keepdims=True)
        acc = alpha * acc + _dot(p.astype(BF16), vb)
        return m_new, l, acc

    carry = (jnp.full((tq, 1), -jnp.inf, F32), jnp.zeros((tq, 1), F32), jnp.zeros((tq, MLA_V), F32))
    carry = lax.fori_loop(0, i * per_q, lambda j, c: step(j, c, None), carry)
    for band in range(per_q):
        carry = step(i * per_q + band, carry, band)
    _, l, acc = carry
    o_ref[...] = (acc / l).astype(o_ref.dtype)


FLASH_TQ = 1024
FLASH_TK = 1024


def _flash_attention(q, k, v, nb, t):
    tq, tk = FLASH_TQ, FLASH_TK
    nq = t // tq
    nh = MLA_HEADS
    return pl.pallas_call(
        functools.partial(_flash_kernel, tq=tq, tk=tk), name="flash_prompt",
        grid=(nb, nh, nq),
        in_specs=[pl.BlockSpec((None, tq, MLA_QKP), lambda b, h, i: (h, b * nq + i, 0)),
                  pl.BlockSpec((None, t, MLA_QKP), lambda b, h, i: (h, b, 0)),
                  pl.BlockSpec((None, t, MLA_V), lambda b, h, i: (h, b, 0))],
        out_specs=pl.BlockSpec((tq, MLA_V), lambda b, h, i: (b * nq + i, h)),
        out_shape=jax.ShapeDtypeStruct((nb * t, nh * MLA_V), BF16),
        compiler_params=_cparams("parallel", "parallel", "arbitrary"),
    )(q, k, v)


DEC_CHUNK_PAGES = 16
DEC_SUB_PAGES = 4
DEC_QROWS = 16


def _page_copy(pt_ref, cache_ref, buf_ref, sem_ref, layer, b, i, slot):
    return pltpu.make_async_copy(cache_ref.at[layer, pt_ref[b, i]], buf_ref.at[slot, i], sem_ref.at[slot])


def _decode_kernel(pt_ref, cache_ref, wt_ref, qp_ref, qr_ref, gr_ref, cc_ref, ss_ref, qf_ref, kn_ref, latn_ref,
                   o_ref, buf_ref, sem_ref, lhs_ref, latb_ref, *, n_pages, nsamp, layer):
    b = pl.program_id(0)
    nh = MLA_HEADS
    slot = lax.rem(b, 2)
    nw = nh * MLA_NOPE

    def fetch(bb, sl):
        for i in range(n_pages):
            _page_copy(pt_ref, cache_ref, buf_ref, sem_ref, layer, bb, i, sl).start()

    @pl.when(b == 0)
    def _():
        fetch(b, slot)
        lhs_ref[0:nw, :] = wt_ref[...]

    @pl.when(b + 1 < nsamp)
    def _():
        fetch(b + 1, 1 - slot)

    for i in range(n_pages):
        _page_copy(pt_ref, cache_ref, buf_ref, sem_ref, layer, b, i, slot).wait()
    pages = [buf_ref.at[slot, i] for i in range(n_pages)]

    lhs_ref[nw:nw + DEC_QROWS, :] = qp_ref[...]
    lhs = lhs_ref[...]
    gr = gr_ref[...]
    half = MLA_ROPE // 2
    sub = DEC_SUB_PAGES * PAGE_SIZE
    chunk = DEC_CHUNK_PAGES * PAGE_SIZE
    qr = qr_ref[...]
    qr_b = [jnp.broadcast_to(qr[:, h:h + 1], (MLA_ROPE, sub)) for h in range(nh)]

    m_run = jnp.full((nh, 1), -jnp.inf, F32)
    l_run = jnp.zeros((nh, 1), F32)
    acc = jnp.zeros((nh, MLA_KV_LORA), F32)
    for c in range(n_pages // DEC_CHUNK_PAGES):
        s_parts = []
        for t in range(DEC_CHUNK_PAGES // DEC_SUB_PAGES):
            i0 = c * DEC_CHUNK_PAGES + t * DEC_SUB_PAGES
            k0 = i0 * PAGE_SIZE
            pg = pages[i0:i0 + DEC_SUB_PAGES]
            latb = jnp.concatenate([p[0:MLA_KV_LORA, :] for p in pg], axis=1).astype(BF16)
            kr = jnp.concatenate([p[MLA_KV_LORA:MLA_KV_LORA + MLA_ROPE, :] for p in pg], axis=1)
            latb_ref[:, k0:k0 + sub] = latb
            kt = _dot(lhs, latb)
            ssn = jnp.concatenate(
                [jnp.sum(jnp.square(kt[h * 128:(h + 1) * 128, :]), axis=0, keepdims=True) for h in range(nh)],
                axis=0)
            sn = kt[nw:nw + nh, :]
            krg = kr * gr
            swapped = jnp.concatenate([krg[half:, :], krg[:half, :]], axis=0)
            rot = krg * cc_ref[:, k0:k0 + sub] + swapped * ss_ref[:, k0:k0 + sub]
            sr = jnp.concatenate([jnp.sum(rot * qr_b[h], axis=0, keepdims=True) for h in range(nh)], axis=0)
            kr_ss = jnp.sum(kr * kr, axis=0, keepdims=True)
            rs = lax.rsqrt((ssn + kr_ss) * (1.0 / MLA_QK) + EPS)
            s_parts.append(rs * (sn + sr))
        s = jnp.concatenate(s_parts, axis=1)
        m_new = jnp.maximum(m_run, jnp.max(s, axis=1, keepdims=True))
        alpha = jnp.exp(m_run - m_new)
        p = jnp.exp(s - m_new)
        l_run = alpha * l_run + jnp.sum(p, axis=1, keepdims=True)
        acc = alpha * acc + _dot_nt(p.astype(BF16), latb_ref[:, c * chunk:(c + 1) * chunk])
        m_run = m_new

    s_new = jnp.sum(qf_ref[...] * kn_ref[...], axis=1, keepdims=True)
    m_fin = jnp.maximum(m_run, s_new)
    a_fin = jnp.exp(m_run - m_fin)
    p_new = jnp.exp(s_new - m_fin)
    l_fin = a_fin * l_run + p_new
    lat_new = latn_ref[...].astype(BF16).astype(F32)
    o_ref[...] = (a_fin * acc + p_new.astype(BF16).astype(F32) * lat_new) / l_fin


def _mla_decode(cache, layer, page_table, wt_nope, qp, qr, g_rope, cc_keys, ss_keys, qf, kn, lat_new):
    nsamp, n_pages = page_table.shape
    assert n_pages % DEC_CHUNK_PAGES == 0 and DEC_CHUNK_PAGES % DEC_SUB_PAGES == 0
    n_keys = n_pages * PAGE_SIZE
    kvd = cache.shape[-1]
    cache = jnp.transpose(cache, (0, 1, 3, 2))
    nh = MLA_HEADS
    grid_spec = pltpu.PrefetchScalarGridSpec(
        num_scalar_prefetch=1,
        grid=(nsamp,),
        in_specs=[
            pl.BlockSpec(memory_space=pl.ANY),
            pl.BlockSpec(wt_nope.shape, lambda b, pt: (0, 0)),
            pl.BlockSpec((None, DEC_QROWS, MLA_KV_LORA), lambda b, pt: (b, 0, 0)),
            pl.BlockSpec((None, MLA_ROPE, nh), lambda b, pt: (b, 0, 0)),
            pl.BlockSpec((MLA_ROPE, 1), lambda b, pt: (0, 0)),
            pl.BlockSpec((MLA_ROPE, n_keys), lambda b, pt: (0, 0)),
            pl.BlockSpec((MLA_ROPE, n_keys), lambda b, pt: (0, 0)),
            pl.BlockSpec((None, nh, MLA_QKP), lambda b, pt: (b, 0, 0)),
            pl.BlockSpec((None, nh, MLA_QKP), lambda b, pt: (b, 0, 0)),
            pl.BlockSpec((None, 1, MLA_KV_LORA), lambda b, pt: (b, 0, 0)),
        ],
        out_specs=pl.BlockSpec((None, nh, MLA_KV_LORA), lambda b, pt: (b, 0, 0)),
        scratch_shapes=[pltpu.VMEM((2, n_pages, kvd, PAGE_SIZE), F32), pltpu.SemaphoreType.DMA((2,)),
                        pltpu.VMEM((nh * MLA_NOPE + DEC_QROWS, MLA_KV_LORA), BF16),
                        pltpu.VMEM((MLA_KV_LORA, n_keys), BF16)],
    )
    return pl.pallas_call(
        functools.partial(_decode_kernel, n_pages=n_pages, nsamp=nsamp, layer=layer), name="mla_decode",
        grid_spec=grid_spec,
        out_shape=jax.ShapeDtypeStruct((nsamp, nh, MLA_KV_LORA), F32),
        compiler_params=_cparams("arbitrary"),
    )(page_table, cache, wt_nope, qp, qr, g_rope, cc_keys, ss_keys, qf, kn, lat_new)


def _ssd_gate_norm(y, xs, z, dskip, gnorm):
    y = (y + dskip * xs) * _silu(z)
    gw = SSD_D_INNER // SSD_GROUPS
    outs = []
    for g in range(SSD_GROUPS):
        yg = y[:, g * gw:(g + 1) * gw]
        outs.append(yg * lax.rsqrt(jnp.mean(yg * yg, axis=-1, keepdims=True) + EPS))
    return jnp.concatenate(outs, axis=1) * gnorm


def _ssd_prompt_kernel(z_ref, x_ref, bc_ref, dt_ref, bufx_ref, bufbc_ref, wx_ref, bx_ref, wbc_ref, bbc_ref,
                       dtb_ref, alog_ref, dskip_ref, gnorm_ref, h0_ref, tri_ref, exp_ref,
                       y_ref, hout_ref, extx_ref, extbc_ref, ht_ref, *, nc):
    c = pl.program_id(1)
    first = c == 0
    l = SSD_CHUNK
    xs = _silu(_conv_chunk(extx_ref, x_ref[...], bufx_ref, wx_ref, bx_ref, first))
    bcs = _silu(_conv_chunk(extbc_ref, bc_ref[...], bufbc_ref, wbc_ref, bbc_ref, first))
    gs = SSD_GROUPS * SSD_STATE
    bm, cm = bcs[:, :gs], bcs[:, gs:]

    @pl.when(first)
    def _():
        for r in range(SSD_D_INNER // LANES):
            ht_ref[:, r * LANES:(r + 1) * LANES] = h0_ref[r * LANES:(r + 1) * LANES, :].T

    dt = _softplus(dt_ref[...] + dtb_ref[...])
    a = -jnp.exp(alog_ref[...])
    acum = _dot_sel_left(tri_ref[...], dt * a)
    acum_t = acum.T
    dt_t = dt.T
    a_last = acum[l - 1:l, :]
    dend = jnp.exp(a_last - acum)
    sel = exp_ref[...]
    w_state = _dot_sel(dt * dend, sel)
    e_acc = _dot_sel(jnp.exp(acum), sel)
    e_last = _dot_sel(jnp.exp(a_last), sel)

    rows = lax.broadcasted_iota(jnp.int32, (l, l), 0)
    cols = lax.broadcasted_iota(jnp.int32, (l, l), 1)
    causal = cols <= rows
    lane = lax.broadcasted_iota(jnp.int32, (l, LANES), 1)
    xs_b = xs.astype(BF16)
    hpg = SSD_HEADS // SSD_GROUPS
    gw = SSD_D_INNER // SSD_GROUPS
    y_groups = []
    for g in range(SSD_GROUPS):
        gcols = slice(g * gw, (g + 1) * gw)
        bg = bm[:, g * SSD_STATE:(g + 1) * SSD_STATE]
        cg_b = cm[:, g * SSD_STATE:(g + 1) * SSD_STATE].astype(BF16)
        cb = _dot_nt(cg_b, bg.astype(BF16))
        ht_g = ht_ref[:, gcols]
        y_off = _dot(cg_b, ht_g.astype(BF16))
        y_pairs = []
        for pr in range(hpg // 2):
            col = (g * hpg + pr * 2) * SSD_HEAD_DIM
            res = []
            for hh in range(2):
                h = g * hpg + pr * 2 + hh
                diff = acum[:, h:h + 1] - acum_t[h:h + 1, :]
                lmat = jnp.exp(jnp.where(causal, diff, -jnp.inf)) * dt_t[h:h + 1, :]
                res.append(_dot((cb * lmat).astype(BF16), xs_b[:, col:col + LANES]))
            y_pairs.append(jnp.where(lane < SSD_HEAD_DIM, res[0], res[1]))
        y_groups.append(jnp.concatenate(y_pairs, axis=1) + e_acc[:, gcols] * y_off)
        xw = (xs[:, gcols] * w_state[:, gcols]).astype(BF16)
        ht_ref[:, gcols] = e_last[:, gcols] * ht_g + _dot(bg.T.astype(BF16), xw)
    y = jnp.concatenate(y_groups, axis=1)
    y_ref[...] = _ssd_gate_norm(y, xs, z_ref[...], dskip_ref[...], gnorm_ref[...]).astype(y_ref.dtype)

    @pl.when(c == nc - 1)
    def _():
        for r in range(SSD_D_INNER // LANES):
            hout_ref[r * LANES:(r + 1) * LANES, :] = ht_ref[:, r * LANES:(r + 1) * LANES].T


def _dot_sel_left(sel, x):
    hi, mid, lo = _split3(x)
    return _dot(sel, hi) + _dot(sel, mid) + _dot(sel, lo)


def _ssd_consts():
    l = SSD_CHUNK
    tri = (jnp.arange(l)[:, None] >= jnp.arange(l)[None, :]).astype(BF16)
    sel = (jnp.arange(LANES)[:, None] == (jnp.arange(SSD_D_INNER)[None, :] // SSD_HEAD_DIM)).astype(BF16)
    return tri, sel


def _ssd_prompt(u, u_dt, bufx, bufbc, h0, prm, nb, t):
    l = SSD_CHUNK
    nc = t // l
    tri, sel = _ssd_consts()
    full = lambda a: pl.BlockSpec(a.shape, lambda b, c: (0,) * a.ndim)
    row = lambda width, colblk: pl.BlockSpec((l, width), lambda b, c: (b * nc + c, colblk))
    y, hout = pl.pallas_call(
        functools.partial(_ssd_prompt_kernel, nc=nc), name="ssd_prompt",
        grid=(nb, nc),
        in_specs=[row(SSD_D_INNER, 0), row(SSD_D_INNER, 1), row(SSD_BC, 4), row(LANES, 0),
                  pl.BlockSpec((None, CONV_WIDTH - 1, SSD_D_INNER), lambda b, c: (b, 0, 0)),
                  pl.BlockSpec((None, CONV_WIDTH - 1, SSD_BC), lambda b, c: (b, 0, 0)),
                  full(prm["w_conv_x"]), full(prm["b_conv_x"]), full(prm["w_conv_bc"]), full(prm["b_conv_bc"]),
                  full(prm["dt_bias"]), full(prm["a_log"]), full(prm["d_skip"]), full(prm["g_ssd_norm"]),
                  pl.BlockSpec((None, SSD_D_INNER, SSD_STATE), lambda b, c: (b, 0, 0)),
                  full(tri), full(sel)],
        out_specs=[pl.BlockSpec((l, SSD_D_INNER), lambda b, c: (b * nc + c, 0)),
                   pl.BlockSpec((None, SSD_D_INNER, SSD_STATE), lambda b, c: (b, 0, 0))],
        out_shape=[jax.ShapeDtypeStruct((nb * t, SSD_D_INNER), BF16),
                   jax.ShapeDtypeStruct((nb, SSD_D_INNER, SSD_STATE), F32)],
        scratch_shapes=[pltpu.VMEM((CONV_HALO + l, SSD_D_INNER), F32),
                        pltpu.VMEM((CONV_HALO + l, SSD_BC), F32),
                        pltpu.VMEM((SSD_STATE, SSD_D_INNER), F32)],
        compiler_params=_cparams("parallel", "arbitrary"),
    )(u, u, u, u_dt, bufx, bufbc, prm["w_conv_x"], prm["b_conv_x"], prm["w_conv_bc"], prm["b_conv_bc"],
      prm["dt_bias"], prm["a_log"], prm["d_skip"], prm["g_ssd_norm"], h0, tri, sel)
    return y, hout


def _ssd_step_prep_kernel(x_ref, bc_ref, dt_ref, bufx_ref, bufbc_ref, wx_ref, bx_ref, wbc_ref, bbc_ref,
                          dtb_ref, alog_ref, exp_ref, xs_ref, bcs_ref, dtx_ref, dae_ref):
    xs = _silu(_conv_step(bufx_ref, x_ref[...], wx_ref, bx_ref))
    xs_ref[...] = xs
    bcs_ref[...] = _silu(_conv_step(bufbc_ref, bc_ref[...], wbc_ref, bbc_ref))
    dt = _softplus(dt_ref[...] + dtb_ref[...])
    da = jnp.exp(dt * (-jnp.exp(alog_ref[...])))
    sel = exp_ref[...]
    dtx_ref[...] = _dot_sel(dt, sel) * xs
    dae_ref[...] = _dot_sel(da, sel)


def _ssd_step_prep(u, u_dt, bufx_t, bufbc_t, prm):
    m = u_dt.shape[0]
    _, sel = _ssd_consts()
    full = lambda a: pl.BlockSpec(a.shape, lambda i: (0,) * a.ndim)
    return pl.pallas_call(
        _ssd_step_prep_kernel, name="ssd_step_prep",
        grid=(1,),
        in_specs=[pl.BlockSpec((m, SSD_D_INNER), lambda i: (0, 1)),
                  pl.BlockSpec((m, SSD_BC), lambda i: (0, 4)),
                  full(u_dt), full(bufx_t), full(bufbc_t),
                  full(prm["w_conv_x"]), full(prm["b_conv_x"]), full(prm["w_conv_bc"]), full(prm["b_conv_bc"]),
                  full(prm["dt_bias"]), full(prm["a_log"]), full(sel)],
        out_specs=[pl.BlockSpec((m, SSD_D_INNER), lambda i: (0, 0)),
                   pl.BlockSpec((m, SSD_BC), lambda i: (0, 0)),
                   pl.BlockSpec((m, SSD_D_INNER), lambda i: (0, 0)),
                   pl.BlockSpec((m, SSD_D_INNER), lambda i: (0, 0))],
        out_shape=[jax.ShapeDtypeStruct((m, SSD_D_INNER), F32),
                   jax.ShapeDtypeStruct((m, SSD_BC), F32),
                   jax.ShapeDtypeStruct((m, SSD_D_INNER), F32),
                   jax.ShapeDtypeStruct((m, SSD_D_INNER), F32)],
        compiler_params=_cparams("arbitrary"),
    )(u, u, u_dt, bufx_t, bufbc_t, prm["w_conv_x"], prm["b_conv_x"], prm["w_conv_bc"], prm["b_conv_bc"],
      prm["dt_bias"], prm["a_log"], sel)


def _ssd_step_state_kernel(h0_ref, cols_ref, b_ref, c_ref, hn_ref, y_ref):
    nr = SSD_D_INNER // LANES
    pad = jnp.concatenate([cols_ref[...], jnp.zeros((LANES - 2 * nr, LANES), F32)], axis=0)
    ct = pad.T
    rows_per_group = SSD_D_INNER // SSD_GROUPS // LANES
    for r in range(nr):
        g = r // rows_per_group
        hn_ref[r * LANES:(r + 1) * LANES, :] = (ct[:, nr + r:nr + r + 1] * h0_ref[r * LANES:(r + 1) * LANES, :]
                                                + ct[:, r:r + 1] * b_ref[g:g + 1, :])
    res = _dot_nt(c_ref[...].astype(BF16), hn_ref[...].astype(BF16))
    gw = SSD_D_INNER // SSD_GROUPS
    y_ref[...] = jnp.concatenate([res[g:g + 1, g * gw:(g + 1) * gw] for g in range(SSD_GROUPS)], axis=1)


def _ssd_step_state(h0, cols, bmat, cmat):
    m = h0.shape[0]
    per = lambda shape: pl.BlockSpec((None,) + shape, lambda i: (i, 0, 0))
    return pl.pallas_call(
        _ssd_step_state_kernel, name="ssd_step_state",
        grid=(m,),
        in_specs=[per((SSD_D_INNER, SSD_STATE)), per((2 * SSD_D_INNER // LANES, LANES)),
                  per((8, SSD_STATE)), per((8, SSD_STATE))],
        out_specs=[per((SSD_D_INNER, SSD_STATE)), per((1, SSD_D_INNER))],
        out_shape=[jax.ShapeDtypeStruct((m, SSD_D_INNER, SSD_STATE), F32),
                   jax.ShapeDtypeStruct((m, 1, SSD_D_INNER), F32)],
        compiler_params=_cparams("parallel"),
    )(h0, cols, bmat, cmat)


def _ssd_step_post_kernel(y_ref, xs_ref, z_ref, dskip_ref, gnorm_ref, o_ref):
    o_ref[...] = _ssd_gate_norm(y_ref[...], xs_ref[...], z_ref[...], dskip_ref[...],
                                gnorm_ref[...]).astype(o_ref.dtype)


def _ssd_step_post(y, xs, u, prm):
    m = y.shape[0]
    full = lambda a: pl.BlockSpec(a.shape, lambda i: (0,) * a.ndim)
    return pl.pallas_call(
        _ssd_step_post_kernel, name="ssd_step_post",
        grid=(1,),
        in_specs=[full(y), full(xs), pl.BlockSpec((m, SSD_D_INNER), lambda i: (0, 0)),
                  full(prm["d_skip"]), full(prm["g_ssd_norm"])],
        out_specs=pl.BlockSpec((m, SSD_D_INNER), lambda i: (0, 0)),
        out_shape=jax.ShapeDtypeStruct((m, SSD_D_INNER), BF16),
        compiler_params=_cparams("arbitrary"),
    )(y, xs, u, prm["d_skip"], prm["g_ssd_norm"])


def _lru_gates(xc, wr_ref, br_ref, wi_ref, bi_ref, lam_ref, is_pos0):
    xb = xc.astype(BF16)
    rs, is_ = [], []
    for n in range(LRU_BLOCKS):
        blk = xb[:, n * LRU_BLOCK_DIM:(n + 1) * LRU_BLOCK_DIM]
        rs.append(_dot(blk, wr_ref[n]))
        is_.append(_dot(blk, wi_ref[n]))
    r = _sigmoid(jnp.concatenate(rs, axis=1) + br_ref[...])
    i = _sigmoid(jnp.concatenate(is_, axis=1) + bi_ref[...])
    log_a = -LRU_C * r * _softplus(-lam_ref[...])
    a = jnp.exp(log_a)
    th = jnp.tanh(log_a)
    mult = jnp.sqrt(-2.0 * th / (1.0 - th))
    if is_pos0 is not False:
        mult = jnp.where(is_pos0, 1.0, mult)
    return a, mult * i * xc


def _shift_rows(x, d, fill):
    rolled = pltpu.roll(x, d, 0)
    row = lax.broadcasted_iota(jnp.int32, x.shape, 0)
    return jnp.where(row >= d, rolled, fill)


def _lru_prompt_kernel(lx_ref, lg_ref, buf_ref, wc_ref, bc_ref, wr_ref, br_ref, wi_ref, bi_ref, lam_ref, h0_ref,
                       y_ref, hout_ref, ext_ref, h_ref, *, tc, nc, pos0):
    c = pl.program_id(1)
    first = c == 0

    @pl.when(first)
    def _():
        h_ref[...] = h0_ref[...]

    xc = _conv_chunk(ext_ref, lx_ref[...], buf_ref, wc_ref, bc_ref, first)
    pos = pos0 + c * tc + lax.broadcasted_iota(jnp.int32, (tc, 1), 0)
    a, b = _lru_gates(xc, wr_ref, br_ref, wi_ref, bi_ref, lam_ref, pos == 0)
    d = 1
    while d < tc:
        b = b + a * _shift_rows(b, d, 0.0)
        a = a * _shift_rows(a, d, 1.0)
        d *= 2
    h = b + a * h_ref[...]
    h_ref[...] = h[tc - 1:tc, :]
    y_ref[...] = (h * _gelu_tanh(lg_ref[...])).astype(y_ref.dtype)

    @pl.when(c == nc - 1)
    def _():
        hout_ref[...] = h[tc - 1:tc, :]


def _lru_prompt(u, buf, h0, prm, nb, t, pos0):
    tc = 256
    nc = t // tc
    full = lambda a: pl.BlockSpec(a.shape, lambda b, c: (0,) * a.ndim)
    y, hout = pl.pallas_call(
        functools.partial(_lru_prompt_kernel, tc=tc, nc=nc, pos0=pos0), name="lru_prompt",
        grid=(nb, nc),
        in_specs=[pl.BlockSpec((tc, LRU_WIDTH), lambda b, c: (b * nc + c, 5)),
                  pl.BlockSpec((tc, LRU_WIDTH), lambda b, c: (b * nc + c, 6)),
                  pl.BlockSpec((None, CONV_WIDTH - 1, LRU_WIDTH), lambda b, c: (b, 0, 0)),
                  full(prm["w_conv_lru"]), full(prm["b_conv_lru"]), full(prm["w_lru_r"]), full(prm["b_lru_r"]),
                  full(prm["w_lru_i"]), full(prm["b_lru_i"]), full(prm["lru_lambda"]),
                  pl.BlockSpec((None, 1, LRU_WIDTH), lambda b, c: (b, 0, 0))],
        out_specs=[pl.BlockSpec((tc, LRU_WIDTH), lambda b, c: (b * nc + c, 0)),
                   pl.BlockSpec((None, 1, LRU_WIDTH), lambda b, c: (b, 0, 0))],
        out_shape=[jax.ShapeDtypeStruct((nb * t, LRU_WIDTH), BF16),
                   jax.ShapeDtypeStruct((nb, 1, LRU_WIDTH), F32)],
        scratch_shapes=[pltpu.VMEM((CONV_HALO + tc, LRU_WIDTH), F32), pltpu.VMEM((1, LRU_WIDTH), F32)],
        compiler_params=_cparams("parallel", "arbitrary"),
    )(u, u, buf, prm["w_conv_lru"], prm["b_conv_lru"], prm["w_lru_r"], prm["b_lru_r"], prm["w_lru_i"],
      prm["b_lru_i"], prm["lru_lambda"], h0)
    return y, hout


def _lru_step_kernel(lx_ref, lg_ref, buf_ref, wc_ref, bc_ref, wr_ref, br_ref, wi_ref, bi_ref, lam_ref, h0_ref,
                     y_ref, hout_ref, *, pos0):
    xc = _conv_step(buf_ref, lx_ref[...], wc_ref, bc_ref)
    a, b = _lru_gates(xc, wr_ref, br_ref, wi_ref, bi_ref, lam_ref, pos0 == 0)
    h = b + a * h0_ref[...]
    hout_ref[...] = h
    y_ref[...] = (h * _gelu_tanh(lg_ref[...])).astype(y_ref.dtype)


def _lru_step(u, buf_t, h0, prm, pos0):
    m = h0.shape[0]
    full = lambda a: pl.BlockSpec(a.shape, lambda i: (0,) * a.ndim)
    return pl.pallas_call(
        functools.partial(_lru_step_kernel, pos0=pos0), name="lru_step",
        grid=(1,),
        in_specs=[pl.BlockSpec((m, LRU_WIDTH), lambda i: (0, 5)),
                  pl.BlockSpec((m, LRU_WIDTH), lambda i: (0, 6)),
                  full(buf_t), full(prm["w_conv_lru"]), full(prm["b_conv_lru"]), full(prm["w_lru_r"]),
                  full(prm["b_lru_r"]), full(prm["w_lru_i"]), full(prm["b_lru_i"]), full(prm["lru_lambda"]),
                  full(h0)],
        out_specs=[pl.BlockSpec((m, LRU_WIDTH), lambda i: (0, 0)),
                   pl.BlockSpec((m, LRU_WIDTH), lambda i: (0, 0))],
        out_shape=[jax.ShapeDtypeStruct((m, LRU_WIDTH), BF16),
                   jax.ShapeDtypeStruct((m, LRU_WIDTH), F32)],
        compiler_params=_cparams("arbitrary"),
    )(u, u, buf_t, prm["w_conv_lru"], prm["b_conv_lru"], prm["w_lru_r"], prm["b_lru_r"], prm["w_lru_i"],
      prm["b_lru_i"], prm["lru_lambda"], h0)


def _rope_tables(n_pos):
    half = MLA_ROPE // 2
    inv = ROPE_THETA ** (-jnp.arange(half, dtype=F32) * (2.0 / MLA_ROPE))
    ang = jnp.arange(n_pos, dtype=F32)[:, None] * inv[None, :]
    cos, sin = jnp.cos(ang), jnp.sin(ang)
    cc = jnp.concatenate([cos, cos], axis=1)
    ss = jnp.concatenate([-sin, sin], axis=1)
    return cc, ss


def _even_params(e, w_in_e, w_pool, s_pool, g_qlat, w_qb, g_kvlat, w_kvb, g_q, g_k, w_out_e):
    nh = MLA_HEADS
    w_in = jnp.pad(w_in_e[e], ((0, 0), (0, EVEN_IN_PAD - w_in_e.shape[2]))).astype(BF16)
    wq = w_qb[e].reshape(MLA_Q_LORA, nh, MLA_QK)
    wq_nope = wq[:, :, :MLA_NOPE].reshape(MLA_Q_LORA, nh * MLA_NOPE)
    wq_rope = jnp.pad(wq[:, :, MLA_NOPE:], ((0, 0), (0, 0), (0, LANES - MLA_ROPE))).reshape(MLA_Q_LORA, nh * LANES)
    wkv = w_kvb[e].reshape(MLA_KV_LORA, nh, MLA_NOPE + MLA_V)
    w_nope_t = jnp.transpose(wkv[:, :, :MLA_NOPE], (1, 2, 0))
    pad_g = lambda g: jnp.pad(g, (0, MLA_QKP - MLA_QK)).reshape(1, MLA_QKP)
    return dict(
        w_in=w_in,
        w_pool=w_pool[e].astype(BF16),
        s_pool=s_pool[e].reshape(1, POOL_WIDTH),
        g_qlat=g_qlat[e].reshape(1, MLA_Q_LORA),
        g_kvlat=g_kvlat[e].reshape(1, MLA_KV_LORA),
        w_qb=jnp.concatenate([wq_nope, wq_rope], axis=1).astype(BF16),
        w_kvb=w_kvb[e].astype(BF16),
        g_q=pad_g(g_q[e]), g_k=pad_g(g_k[e]),
        g_k_rope=g_k[e][MLA_NOPE:].reshape(MLA_ROPE, 1),
        w_nope_t=w_nope_t.astype(BF16),
        w_v=jnp.transpose(wkv[:, :, MLA_NOPE:], (1, 0, 2)).astype(BF16),
        w_out_pool=w_out_e[e][:POOL_WIDTH].astype(BF16),
        w_out_att=w_out_e[e][POOL_WIDTH:].astype(BF16),
    )


def _odd_params(o, w_in_o, w_conv_ssd, b_conv_ssd, dt_bias, a_log, d_skip, g_ssd_norm, w_conv_lru, b_conv_lru,
                w_lru_r, b_lru_r, w_lru_i, b_lru_i, lru_lambda, w_out_o):
    j2 = SSD_D_INNER + SSD_D_INNER + SSD_BC
    j3 = j2 + SSD_HEADS
    w = w_in_o[o]
    pad_heads = lambda v: jnp.pad(v, (0, LANES - SSD_HEADS)).reshape(1, LANES)
    return dict(
        w_in=jnp.concatenate([w[:, :j2], w[:, j3:]], axis=1).astype(BF16),
        w_in_dt=jnp.pad(w[:, j2:j3], ((0, 0), (0, LANES - SSD_HEADS))).astype(BF16),
        w_conv_x=w_conv_ssd[o][:, :SSD_D_INNER], w_conv_bc=w_conv_ssd[o][:, SSD_D_INNER:],
        b_conv_x=b_conv_ssd[o][:SSD_D_INNER].reshape(1, -1), b_conv_bc=b_conv_ssd[o][SSD_D_INNER:].reshape(1, -1),
        dt_bias=pad_heads(dt_bias[o]), a_log=pad_heads(a_log[o]),
        d_skip=jnp.repeat(d_skip[o], SSD_HEAD_DIM).reshape(1, SSD_D_INNER),
        g_ssd_norm=g_ssd_norm[o].reshape(1, SSD_D_INNER),
        w_conv_lru=w_conv_lru[o], b_conv_lru=b_conv_lru[o].reshape(1, LRU_WIDTH),
        w_lru_r=w_lru_r[o].astype(BF16), b_lru_r=b_lru_r[o].reshape(1, LRU_WIDTH),
        w_lru_i=w_lru_i[o].astype(BF16), b_lru_i=b_lru_i[o].reshape(1, LRU_WIDTH),
        lru_lambda=lru_lambda[o].reshape(1, LRU_WIDTH),
        w_out_ssd=w_out_o[o][:SSD_D_INNER].astype(BF16),
        w_out_lru=w_out_o[o][SSD_D_INNER:].astype(BF16),
    )


def _pad_tab(tab):
    return jnp.pad(tab, ((0, 0), (0, LANES - tab.shape[1])))


def _even_layer_prompt(rows, hn, prm, nb, t):
    u = _matmul(rows, [hn], [prm["w_in"]], name="in_even", tn=EVEN_IN_PAD)
    zero_buf = jnp.zeros((nb, POOL_BUF, POOL_WIDTH), F32)
    y_pool = _pool_prompt(u, zero_buf, prm["w_pool"], prm["s_pool"], nb, t, 0)
    cc, ss = _rope_tables(t)
    tm = 256
    tiles = t // tm
    q, k, v, lat = _mla_prep(u, rows.m, tm, lambda i: i % tiles, (_pad_tab(cc), _pad_tab(ss)), prm, False)
    y_att = _flash_attention(q, k, v, nb, t)
    u3 = u.reshape(nb, t, EVEN_IN_PAD)
    kr = u3[:, :, POOL_WIDTH + MLA_Q_LORA + MLA_KV_LORA:POOL_WIDTH + MLA_Q_LORA + MLA_KV_LORA + MLA_ROPE]
    mla_rows = jnp.concatenate([lat.reshape(nb, t, MLA_KV_LORA), kr], axis=-1)
    pool_new = u3[:, t - POOL_BUF:, :POOL_WIDTH]
    return [y_pool, y_att], [prm["w_out_pool"], prm["w_out_att"]], mla_rows, pool_new


def _even_layer_sample(rows, hn, prm, pool_buf, cache, layer, page_table, pos0):
    m = rows.m
    nh = MLA_HEADS
    u = _matmul(rows, [hn], [prm["w_in"]], name="in_even", tn=640)
    y_pool = _pool_step(u, jnp.transpose(pool_buf, (1, 0, 2)), prm["w_pool"], prm["s_pool"], pos0)
    cc, ss = _rope_tables(pos0 + 1)
    cc_new = jnp.broadcast_to(_pad_tab(cc[pos0:]), (m, LANES))
    ss_new = jnp.broadcast_to(_pad_tab(ss[pos0:]), (m, LANES))
    q, k, v, lat, qg = _mla_prep(u, m, m, lambda i: 0, (cc_new, ss_new), prm, True)
    qp = _bmm_heads(qg, prm["w_nope_t"], BF16)
    qp = jnp.pad(jnp.transpose(qp, (1, 0, 2)), ((0, 0), (0, DEC_QROWS - nh), (0, 0)))
    q_m = jnp.transpose(q, (1, 0, 2))
    qr = jnp.transpose(q_m[:, :, MLA_NOPE:MLA_NOPE + MLA_ROPE], (0, 2, 1)).astype(F32)
    o_lat = _mla_decode(cache, layer, page_table, prm["w_nope_t"].reshape(nh * MLA_NOPE, MLA_KV_LORA), qp, qr,
                        prm["g_k_rope"], cc[:pos0].T, ss[:pos0].T, q_m.astype(F32),
                        jnp.transpose(k, (1, 0, 2)).astype(F32), lat.reshape(m, 1, MLA_KV_LORA))
    y_att = _bmm_heads(jnp.transpose(o_lat, (1, 0, 2)).astype(BF16), prm["w_v"], BF16)
    y_att = jnp.transpose(y_att, (1, 0, 2)).reshape(m, nh * MLA_V)
    kr = u[:, POOL_WIDTH + MLA_Q_LORA + MLA_KV_LORA:POOL_WIDTH + MLA_Q_LORA + MLA_KV_LORA + MLA_ROPE]
    mla_rows = jnp.concatenate([lat, kr], axis=-1).reshape(m, 1, MLA_KV_LORA + MLA_ROPE)
    pool_new = jnp.concatenate([pool_buf[:, 1:], u[:, None, :POOL_WIDTH]], axis=1)
    return [y_pool, y_att], [prm["w_out_pool"], prm["w_out_att"]], mla_rows, pool_new


def _odd_layer_prompt(rows, hn, prm, nb, t):
    u = _matmul(rows, [hn], [prm["w_in"]], name="in_odd", tm=1024, tn=1792)
    u_dt = _matmul(rows, [hn], [prm["w_in_dt"]], name="in_odd_dt", tn=LANES)
    k1 = CONV_WIDTH - 1
    y_ssd, h_ssd = _ssd_prompt(u, u_dt, jnp.zeros((nb, k1, SSD_D_INNER), F32), jnp.zeros((nb, k1, SSD_BC), F32),
                               jnp.zeros((nb, SSD_D_INNER, SSD_STATE), F32), prm, nb, t)
    y_lru, h_lru = _lru_prompt(u, jnp.zeros((nb, k1, LRU_WIDTH), F32), jnp.zeros((nb, 1, LRU_WIDTH), F32),
                               prm, nb, t, 0)
    u3 = u.reshape(nb, t, -1)
    sconv = u3[:, t - k1:, SSD_D_INNER:2 * SSD_D_INNER + SSD_BC]
    lconv = u3[:, t - k1:, 2 * SSD_D_INNER + SSD_BC:2 * SSD_D_INNER + SSD_BC + LRU_WIDTH]
    return ([y_ssd, y_lru], [prm["w_out_ssd"], prm["w_out_lru"]], sconv,
            h_ssd.reshape(nb, SSD_HEADS, SSD_HEAD_DIM, SSD_STATE), lconv, h_lru.reshape(nb, LRU_WIDTH))


def _odd_layer_sample(rows, hn, prm, sconv_buf, ssd_state, lconv_buf, lru_state, pos0):
    m = rows.m
    u = _matmul(rows, [hn], [prm["w_in"]], name="in_odd", tn=1024)
    u_dt = _matmul(rows, [hn], [prm["w_in_dt"]], name="in_odd_dt", tn=LANES)
    sconv_t = jnp.transpose(sconv_buf, (1, 0, 2))
    xs, bcs, dtx, dae = _ssd_step_prep(u, u_dt, sconv_t[:, :, :SSD_D_INNER], sconv_t[:, :, SSD_D_INNER:], prm)
    nr = SSD_D_INNER // LANES
    cols = jnp.concatenate([dtx.reshape(m, nr, LANES), dae.reshape(m, nr, LANES)], axis=1)
    gs = SSD_GROUPS * SSD_STATE
    pad8 = lambda a: jnp.pad(a.reshape(m, SSD_GROUPS, SSD_STATE), ((0, 0), (0, 8 - SSD_GROUPS), (0, 0)))
    h_new, y = _ssd_step_state(ssd_state.reshape(m, SSD_D_INNER, SSD_STATE), cols,
                               pad8(bcs[:, :gs]), pad8(bcs[:, gs:]))
    y_ssd = _ssd_step_post(y.reshape(m, SSD_D_INNER), xs, u, prm)
    y_lru, h_lru = _lru_step(u, jnp.transpose(lconv_buf, (1, 0, 2)), lru_state, prm, pos0)
    u_xbc = u[:, SSD_D_INNER:2 * SSD_D_INNER + SSD_BC]
    u_lx = u[:, 2 * SSD_D_INNER + SSD_BC:2 * SSD_D_INNER + SSD_BC + LRU_WIDTH]
    sconv = jnp.concatenate([sconv_buf[:, 1:], u_xbc[:, None]], axis=1)
    lconv = jnp.concatenate([lconv_buf[:, 1:], u_lx[:, None]], axis=1)
    return ([y_ssd, y_lru], [prm["w_out_ssd"], prm["w_out_lru"]], sconv,
            h_new.reshape(m, SSD_HEADS, SSD_HEAD_DIM, SSD_STATE), lconv, h_lru)


def kernel(x_prompt, x_sample, cache_mla, state_pool, state_ssd_conv, state_ssd, state_lru_conv, state_lru, page_table, c_prompt, c_sample, g_norm1, g_norm2, w_mod, b_mod, w_mlp1, w_mlp2, w_in_e, w_pool, s_pool, g_qlat, w_qb, g_kvlat, w_kvb, g_q, g_k, w_out_e, w_in_o, w_conv_ssd, b_conv_ssd, dt_bias, a_log, d_skip, g_ssd_norm, w_conv_lru, b_conv_lru, w_lru_r, b_lru_r, w_lru_i, b_lru_i, lru_lambda, w_out_o):
    nb, t, _ = x_prompt.shape
    ns = x_sample.shape[0]
    assert x_sample.shape[1] == 1 and t >= POOL_BUF
    pos0_s = page_table.shape[1] * PAGE_SIZE

    pad_rows = (-(ns + nb)) % 8
    c_all = jnp.concatenate([c_sample, c_prompt, jnp.zeros((pad_rows, D_MODEL), F32)], axis=0)
    mod_all = _modulation(c_all, w_mod, b_mod)

    rows_p = _Rows(nb, t, 512)
    rows_s = _Rows(ns, 1, ns)
    xp = x_prompt.reshape(nb * t, D_MODEL)
    xs = x_sample.reshape(ns, D_MODEL)
    w2 = w_mlp2.astype(BF16)

    outs_p, outs_s = {}, {}
    for layer in range(DEPTH):
        mod_p = rows_p.mod_array(mod_all[layer, ns:ns + nb])
        mod_s = rows_s.mod_array(mod_all[layer, :ns])
        hn_p = _norm_mod(rows_p, xp, g_norm1[layer], mod_p, 1, 0)
        hn_s = _norm_mod(rows_s, xs, g_norm1[layer], mod_s, 1, 0)
        if layer % 2 == 0:
            e = layer // 2
            prm = _even_params(e, w_in_e, w_pool, s_pool, g_qlat, w_qb, g_kvlat, w_kvb, g_q, g_k, w_out_e)
            a_p, w_o, mla_p, pool_p = _even_layer_prompt(rows_p, hn_p, prm, nb, t)
            a_s, _, mla_s, pool_s = _even_layer_sample(rows_s, hn_s, prm, state_pool[e], cache_mla, e,
                                                       page_table, pos0_s)
            outs_p.setdefault("mla", []).append(mla_p)
            outs_p.setdefault("pool", []).append(pool_p)
            outs_s.setdefault("mla", []).append(mla_s)
            outs_s.setdefault("pool", []).append(pool_s)
        else:
            o = layer // 2
            prm = _odd_params(o, w_in_o, w_conv_ssd, b_conv_ssd, dt_bias, a_log, d_skip, g_ssd_norm, w_conv_lru,
                              b_conv_lru, w_lru_r, b_lru_r, w_lru_i, b_lru_i, lru_lambda, w_out_o)
            a_p, w_o, sconv_p, ssd_p, lconv_p, lru_p = _odd_layer_prompt(rows_p, hn_p, prm, nb, t)
            a_s, _, sconv_s, ssd_s, lconv_s, lru_s = _odd_layer_sample(
                rows_s, hn_s, prm, state_ssd_conv[o], state_ssd[o], state_lru_conv[o], state_lru[o], pos0_s)
            for d, vals in ((outs_p, (sconv_p, ssd_p, lconv_p, lru_p)), (outs_s, (sconv_s, ssd_s, lconv_s, lru_s))):
                for name, val in zip(("sconv", "ssd", "lconv", "lru"), vals):
                    d.setdefault(name, []).append(val)
        xp = _matmul(rows_p, a_p, w_o, name="out_proj", tn=1024, res=xp, mod=mod_p, gate_chunk=2)
        xs = _matmul(rows_s, a_s, w_o, name="out_proj_s", tn=1024, res=xs, mod=mod_s, gate_chunk=2)
        hn2_p = _norm_mod(rows_p, xp, g_norm2[layer], mod_p, 4, 3)
        hn2_s = _norm_mod(rows_s, xs, g_norm2[layer], mod_s, 4, 3)
        act_p = _matmul_ws(hn2_p, w_mlp1, layer, name="mlp1", tm=1024, tn=1024, act="relu2", out_dtype=BF16)
        act_s = _matmul(rows_s, [hn2_s], [(w_mlp1, layer)], name="mlp1_s", tn=1024, act="relu2", out_dtype=BF16)
        xp = _matmul(rows_p, [act_p], [w2[layer]], name="mlp2", tn=512, res=xp, mod=mod_p, gate_chunk=5)
        xs = _matmul(rows_s, [act_s], [(w_mlp2, layer)], name="mlp2_s", tn=1024, tk=2048, res=xs, mod=mod_s,
                     gate_chunk=5)

    st = lambda d, name: jnp.stack(d[name])
    return (xp.reshape(nb, t, D_MODEL), xs.reshape(ns, 1, D_MODEL),
            st(outs_p, "mla"), st(outs_s, "mla"), st(outs_p, "pool"), st(outs_s, "pool"),
            st(outs_p, "sconv"), st(outs_s, "sconv"), st(outs_p, "ssd"), st(outs_s, "ssd"),
            st(outs_p, "lconv"), st(outs_s, "lconv"), st(outs_p, "lru"), st(outs_s, "lru"))
```

```python
import functools

import numpy as np
import jax
import jax.numpy as jnp
from jax import lax
from jax.experimental import pallas as pl
from jax.experimental.pallas import tpu as pltpu

F32 = jnp.float32
BF16 = jnp.bfloat16

VMEM_LIMIT_BYTES = 48 * 1024 * 1024
LANES = 128

D_MODEL = 2048
EPS = 1e-6
N_MOD = 6
DEPTH = 2

POOL_WIDTH = 1024
POOL_WINDOWS = (2, 4, 8, 16)
POOL_GROUP_DIM = 256
POOL_BUF = 15

MLA_HEADS = 8
MLA_NOPE = 128
MLA_ROPE = 64
MLA_V = 128
MLA_QK = 192
MLA_QKP = 256
MLA_Q_LORA = 512
MLA_KV_LORA = 256
ROPE_THETA = 10000.0
PAGE_SIZE = 128
EVEN_IN_PAD = 1920

SSD_D_INNER = 2048
SSD_HEAD_DIM = 64
SSD_HEADS = 32
SSD_GROUPS = 4
SSD_STATE = 128
SSD_CHUNK = 128
SSD_BC = 1024
CONV_WIDTH = 4

LRU_WIDTH = 1024
LRU_BLOCKS = 8
LRU_BLOCK_DIM = 128
LRU_C = 8.0

MLP_HIDDEN = 8192


def _cparams(*sem):
    return pltpu.CompilerParams(dimension_semantics=sem, vmem_limit_bytes=VMEM_LIMIT_BYTES)


def _sigmoid(x):
    return 1.0 / (1.0 + jnp.exp(-x))


def _silu(x):
    return x * _sigmoid(x)


def _softplus(x):
    return jnp.maximum(x, 0.0) + jnp.log1p(jnp.exp(-jnp.abs(x)))


def _gelu_tanh(x):
    return 0.5 * x * (1.0 + jnp.tanh(0.7978845608028654 * (x + 0.044715 * (x * x * x))))


def _dot(a, b):
    return jnp.dot(a, b, preferred_element_type=F32)


def _dot_nt(a, b):
    return lax.dot_general(a, b, (((1,), (1,)), ((), ())), preferred_element_type=F32)


def _split3(x):
    hi = x.astype(BF16)
    r = x - hi.astype(F32)
    mid = r.astype(BF16)
    lo = (r - mid.astype(F32)).astype(BF16)
    return hi, mid, lo


def _dot_sel(x, sel):
    hi, mid, lo = _split3(x)
    return _dot(hi, sel) + _dot(mid, sel) + _dot(lo, sel)


def _mod_kernel(c_ref, w_ref, b_ref, o_ref):
    c = c_ref[...]
    o_ref[...] = _dot(_silu(c).astype(BF16), w_ref[...].astype(BF16)) + b_ref[...]


def _modulation(c_all, w_mod, b_mod):
    mp = c_all.shape[0]
    n = N_MOD * D_MODEL
    tn = 1024
    return pl.pallas_call(
        _mod_kernel, name="modulation",
        grid=(DEPTH, n // tn),
        in_specs=[pl.BlockSpec((mp, D_MODEL), lambda l, j: (0, 0)),
                  pl.BlockSpec((None, D_MODEL, tn), lambda l, j: (l, 0, j)),
                  pl.BlockSpec((None, 1, tn), lambda l, j: (l, 0, j))],
        out_specs=pl.BlockSpec((None, mp, tn), lambda l, j: (l, 0, j)),
        out_shape=jax.ShapeDtypeStruct((DEPTH, mp, n), F32),
        compiler_params=_cparams("parallel", "parallel"),
    )(c_all, w_mod, b_mod.reshape(DEPTH, 1, n))


class _Rows:
    def __init__(self, nb, t, tm):
        self.nb, self.t, self.tm = nb, t, tm
        self.m = nb * t
        self.per_seq = t > 1
        if self.per_seq:
            assert t % tm == 0
            self.tiles_per_seq = t // tm

    def mod_array(self, mod_rows):
        return mod_rows.reshape(self.nb, 1, -1) if self.per_seq else mod_rows

    def mod_spec(self, chunk, width, ncol, col_of):
        per_chunk = D_MODEL // width
        if self.per_seq:
            tps = self.tiles_per_seq
            return pl.BlockSpec((None, 1, width),
                                lambda *g: (g[0] // tps, 0, chunk * per_chunk + col_of(*g)))
        return pl.BlockSpec((self.tm, width), lambda *g: (g[0], chunk * per_chunk + col_of(*g)))


def _norm_mod_kernel(x_ref, g_ref, sc_ref, sh_ref, o_ref):
    x = x_ref[...]
    y = x * lax.rsqrt(jnp.mean(x * x, axis=-1, keepdims=True) + EPS) * g_ref[...]
    o_ref[...] = (y * (1.0 + sc_ref[...]) + sh_ref[...]).astype(o_ref.dtype)


def _norm_mod(rows, x, g, mod, sc_chunk, sh_chunk):
    tm = min(rows.tm, 512)
    r = _Rows(rows.nb, rows.t, tm)
    return pl.pallas_call(
        _norm_mod_kernel, name="norm_mod",
        grid=(r.m // tm,),
        in_specs=[pl.BlockSpec((tm, D_MODEL), lambda i: (i, 0)),
                  pl.BlockSpec((1, D_MODEL), lambda i: (0, 0)),
                  r.mod_spec(sc_chunk, D_MODEL, 1, lambda i: 0),
                  r.mod_spec(sh_chunk, D_MODEL, 1, lambda i: 0)],
        out_specs=pl.BlockSpec((tm, D_MODEL), lambda i: (i, 0)),
        out_shape=jax.ShapeDtypeStruct((r.m, D_MODEL), BF16),
        compiler_params=_cparams("parallel"),
    )(x, g.reshape(1, D_MODEL), mod, mod)


def _mm_kernel(*refs, n_a, nk, act, has_res):
    a_refs = refs[:n_a]
    w_refs = refs[n_a:2 * n_a]
    pos = 2 * n_a
    if has_res:
        x_ref, gt_ref = refs[pos], refs[pos + 1]
        pos += 2
    o_ref = refs[pos]
    acc_ref = refs[pos + 1] if nk > 1 else None

    part = _dot(a_refs[0][...], w_refs[0][...].astype(BF16))
    for a_ref, w_ref in zip(a_refs[1:], w_refs[1:]):
        part = part + _dot(a_ref[...], w_ref[...].astype(BF16))

    def finish(acc):
        if act == "relu2":
            acc = jnp.square(jnp.maximum(acc, 0.0))
        if has_res:
            acc = x_ref[...] + gt_ref[...] * acc
        o_ref[...] = acc.astype(o_ref.dtype)

    if nk == 1:
        finish(part)
    else:
        k = pl.program_id(2)

        @pl.when(k == 0)
        def _():
            acc_ref[...] = part

        @pl.when(k > 0)
        def _():
            acc_ref[...] += part

        @pl.when(k == nk - 1)
        def _():
            finish(acc_ref[...])


def _matmul(rows, a_list, w_list, *, name, tn, tk=None, act=None, res=None, mod=None, gate_chunk=None,
            out_dtype=F32, alias_res=False, tm=None):
    assert tm is None or res is None
    tm, m = (tm or rows.tm), rows.m
    n = (w_list[0][0] if isinstance(w_list[0], tuple) else w_list[0]).shape[-1]
    n_a = len(a_list)
    if tk is None or n_a > 1:
        nk = 1
    else:
        assert a_list[0].shape[1] % tk == 0
        nk = a_list[0].shape[1] // tk
    assert n % tn == 0 and m % tm == 0
    in_specs, args = [], []
    for a in a_list:
        kk = a.shape[1] if nk == 1 else tk
        in_specs.append(pl.BlockSpec((tm, kk), lambda i, j, k: (i, k)))
        args.append(a)
    for w in w_list:
        if isinstance(w, tuple):
            w, layer = w
            kk = w.shape[1] if nk == 1 else tk
            in_specs.append(pl.BlockSpec((None, kk, tn), lambda i, j, k, layer=layer: (layer, k, j)))
        else:
            kk = w.shape[0] if nk == 1 else tk
            in_specs.append(pl.BlockSpec((kk, tn), lambda i, j, k: (k, j)))
        args.append(w)
    has_res = res is not None
    if has_res:
        in_specs.append(pl.BlockSpec((tm, tn), lambda i, j, k: (i, j)))
        in_specs.append(rows.mod_spec(gate_chunk, tn, n // tn, lambda i, j, k: j))
        args += [res, mod]
    scratch = [pltpu.VMEM((tm, tn), F32)] if nk > 1 else []
    aliases = {2 * n_a: 0} if (has_res and alias_res) else {}
    return pl.pallas_call(
        functools.partial(_mm_kernel, n_a=n_a, nk=nk, act=act, has_res=has_res), name=name,
        input_output_aliases=aliases,
        grid=(m // tm, n // tn, nk),
        in_specs=in_specs,
        out_specs=pl.BlockSpec((tm, tn), lambda i, j, k: (i, j)),
        out_shape=jax.ShapeDtypeStruct((m, n), out_dtype),
        scratch_shapes=scratch,
        compiler_params=_cparams("parallel", "parallel", "arbitrary"),
    )(*args)


def _mm_ws_kernel(a_ref, w_ref, o_ref, wb_ref, *, act):
    @pl.when(pl.program_id(1) == 0)
    def _():
        wb_ref[...] = w_ref[...].astype(BF16)

    acc = _dot(a_ref[...], wb_ref[...])
    if act == "relu2":
        acc = jnp.square(jnp.maximum(acc, 0.0))
    o_ref[...] = acc.astype(o_ref.dtype)


def _matmul_ws(a, w_stack, layer, *, name, tm, tn, act=None, out_dtype=F32):
    m, k = a.shape
    n = w_stack.shape[-1]
    assert m % tm == 0 and n % tn == 0
    return pl.pallas_call(
        functools.partial(_mm_ws_kernel, act=act), name=name,
        grid=(n // tn, m // tm),
        in_specs=[pl.BlockSpec((tm, k), lambda j, i: (i, 0)),
                  pl.BlockSpec((None, k, tn), lambda j, i: (layer, 0, j))],
        out_specs=pl.BlockSpec((tm, tn), lambda j, i: (i, j)),
        out_shape=jax.ShapeDtypeStruct((m, n), out_dtype),
        scratch_shapes=[pltpu.VMEM((k, tn), BF16)],
        compiler_params=_cparams("parallel", "arbitrary"),
    )(a, w_stack)


def _bmm_kernel(a_ref, w_ref, o_ref):
    o_ref[...] = _dot(a_ref[...], w_ref[...]).astype(o_ref.dtype)


def _bmm_heads(a, w, out_dtype):
    h, m, k = a.shape
    n = w.shape[2]
    return pl.pallas_call(
        _bmm_kernel, name="bmm_heads",
        grid=(h,),
        in_specs=[pl.BlockSpec((None, m, k), lambda i: (i, 0, 0)),
                  pl.BlockSpec((None, k, n), lambda i: (i, 0, 0))],
        out_specs=pl.BlockSpec((None, m, n), lambda i: (i, 0, 0)),
        out_shape=jax.ShapeDtypeStruct((h, m, n), out_dtype),
        compiler_params=_cparams("parallel"),
    )(a, w)


CONV_HALO = 8


def _conv_chunk(ext_ref, u, buf_ref, w_ref, b_ref, first):
    tc = u.shape[0]
    k1 = CONV_WIDTH - 1

    @pl.when(first)
    def _():
        ext_ref[CONV_HALO - k1:CONV_HALO, :] = buf_ref[...]

    ext_ref[CONV_HALO:CONV_HALO + tc, :] = u
    y = b_ref[...] + w_ref[k1:k1 + 1, :] * u
    for k in range(k1):
        y = y + w_ref[k:k + 1, :] * ext_ref[CONV_HALO - k1 + k:CONV_HALO - k1 + k + tc, :]
    ext_ref[CONV_HALO - k1:CONV_HALO, :] = ext_ref[CONV_HALO + tc - k1:CONV_HALO + tc, :]
    return y


def _conv_step(buf_ref, u, w_ref, b_ref):
    y = b_ref[...] + w_ref[CONV_WIDTH - 1:CONV_WIDTH, :] * u
    for k in range(CONV_WIDTH - 1):
        y = y + w_ref[k:k + 1, :] * buf_ref[k]
    return y


def _pool_prompt_kernel(u_ref, buf_ref, w_ref, s_ref, o_ref, ext_ref, *, tc, pos0):
    c = pl.program_id(1)
    halo = 16

    @pl.when(c == 0)
    def _():
        ext_ref[0:1, :] = jnp.zeros((1, POOL_WIDTH), F32)
        ext_ref[1:halo, :] = buf_ref[...]

    u = u_ref[...]
    ext_ref[halo:halo + tc, :] = u
    pos = (pos0 + c * tc + lax.broadcasted_iota(jnp.int32, (tc, 1), 0)).astype(F32)
    outs = []
    for g, w in enumerate(POOL_WINDOWS):
        lo, hi = g * POOL_GROUP_DIM, (g + 1) * POOL_GROUP_DIM
        x = u[:, lo:hi]
        wsum = x
        for j in range(1, w):
            wsum = wsum + ext_ref[halo - j:halo - j + tc, lo:hi]
        cnt = jnp.minimum(pos + 1.0, float(w))
        d = (wsum / cnt - x).astype(BF16)
        outs.append(_dot(d, w_ref[g]))
    y = jnp.concatenate(outs, axis=1) * s_ref[...]
    o_ref[...] = y.astype(o_ref.dtype)
    ext_ref[0:halo, :] = ext_ref[tc:tc + halo, :]


def _pool_prompt(u, buf, w_pool, s_pool, nb, t, pos0):
    tc = 256
    nc = t // tc
    return pl.pallas_call(
        functools.partial(_pool_prompt_kernel, tc=tc, pos0=pos0), name="pool_prompt",
        grid=(nb, nc),
        in_specs=[pl.BlockSpec((tc, POOL_WIDTH), lambda b, c: (b * nc + c, 0)),
                  pl.BlockSpec((None, POOL_BUF, POOL_WIDTH), lambda b, c: (b, 0, 0)),
                  pl.BlockSpec((len(POOL_WINDOWS), POOL_GROUP_DIM, POOL_GROUP_DIM), lambda b, c: (0, 0, 0)),
                  pl.BlockSpec((1, POOL_WIDTH), lambda b, c: (0, 0))],
        out_specs=pl.BlockSpec((tc, POOL_WIDTH), lambda b, c: (b * nc + c, 0)),
        out_shape=jax.ShapeDtypeStruct((nb * t, POOL_WIDTH), BF16),
        scratch_shapes=[pltpu.VMEM((16 + tc, POOL_WIDTH), F32)],
        compiler_params=_cparams("parallel", "arbitrary"),
    )(u, buf, w_pool, s_pool)


def _pool_step_kernel(u_ref, buf_ref, w_ref, s_ref, o_ref, *, pos0):
    outs = []
    for g, w in enumerate(POOL_WINDOWS):
        lo, hi = g * POOL_GROUP_DIM, (g + 1) * POOL_GROUP_DIM
        x = u_ref[:, lo:hi]
        wsum = x
        for j in range(1, w):
            wsum = wsum + buf_ref[POOL_BUF - j, :, lo:hi]
        cnt = min(pos0 + 1.0, float(w))
        d = (wsum / cnt - x).astype(BF16)
        outs.append(_dot(d, w_ref[g]))
    o_ref[...] = (jnp.concatenate(outs, axis=1) * s_ref[...]).astype(o_ref.dtype)


def _pool_step(u, buf_t, w_pool, s_pool, pos0):
    m = buf_t.shape[1]
    return pl.pallas_call(
        functools.partial(_pool_step_kernel, pos0=pos0), name="pool_step",
        grid=(1,),
        in_specs=[pl.BlockSpec((m, POOL_WIDTH), lambda i: (0, 0)),
                  pl.BlockSpec((POOL_BUF, m, POOL_WIDTH), lambda i: (0, 0, 0)),
                  pl.BlockSpec((len(POOL_WINDOWS), POOL_GROUP_DIM, POOL_GROUP_DIM), lambda i: (0, 0, 0)),
                  pl.BlockSpec((1, POOL_WIDTH), lambda i: (0, 0))],
        out_specs=pl.BlockSpec((m, POOL_WIDTH), lambda i: (0, 0)),
        out_shape=jax.ShapeDtypeStruct((m, POOL_WIDTH), BF16),
        compiler_params=_cparams("arbitrary"),
    )(u, buf_t, w_pool, s_pool)


def _rope128(x, cc, ss):
    lane = lax.broadcasted_iota(jnp.int32, x.shape, 1)
    swapped = jnp.where(lane < MLA_ROPE // 2, pltpu.roll(x, LANES - MLA_ROPE // 2, 1),
                        pltpu.roll(x, MLA_ROPE // 2, 1))
    return x * cc + swapped * ss


def _mla_prep_kernel(uq_ref, ukv_ref, ukr_ref, gql_ref, gkvl_ref, wq_ref, wkv_ref, gq_ref, gk_ref,
                     cc_ref, ss_ref, *out_refs, emit_qg):
    if emit_qg:
        q_ref, k_ref, v_ref, lat_ref, qg_ref = out_refs
    else:
        q_ref, k_ref, v_ref, lat_ref = out_refs
    cc = cc_ref[...]
    ss = ss_ref[...]
    scale = MLA_QK ** -0.5

    uq = uq_ref[...]
    qn = uq * lax.rsqrt(jnp.mean(uq * uq, axis=-1, keepdims=True) + EPS) * gql_ref[...]
    qall = _dot(qn.astype(BF16), wq_ref[...])
    ukv = ukv_ref[...]
    lat = ukv * lax.rsqrt(jnp.mean(ukv * ukv, axis=-1, keepdims=True) + EPS) * gkvl_ref[...]
    lat_ref[...] = lat
    kvall = _dot(lat.astype(BF16), wkv_ref[...])
    kr = ukr_ref[...]
    kr_ss = jnp.sum(kr * kr, axis=-1, keepdims=True)

    gq_n, gq_r = gq_ref[:, :MLA_NOPE], gq_ref[:, MLA_NOPE:]
    gk_n, gk_r = gk_ref[:, :MLA_NOPE], gk_ref[:, MLA_NOPE:]
    nh = MLA_HEADS
    for h in range(nh):
        qnope = qall[:, h * 128:(h + 1) * 128]
        qrope = qall[:, (nh + h) * 128:(nh + h + 1) * 128]
        ssq = jnp.sum(qnope * qnope, axis=-1, keepdims=True) + jnp.sum(qrope * qrope, axis=-1, keepdims=True)
        rs = lax.rsqrt(ssq * (1.0 / MLA_QK) + EPS) * scale
        qn_h = qnope * rs * gq_n
        qr_h = _rope128(qrope * rs * gq_r, cc, ss)
        q_ref[h] = jnp.concatenate([qn_h, qr_h], axis=1).astype(q_ref.dtype)
        if emit_qg:
            qg_ref[h] = (qn_h * gk_n).astype(qg_ref.dtype)

        knope = kvall[:, h * 256:h * 256 + 128]
        ssk = jnp.sum(knope * knope, axis=-1, keepdims=True) + kr_ss
        rsk = lax.rsqrt(ssk * (1.0 / MLA_QK) + EPS)
        kr_h = _rope128(kr * rsk * gk_r, cc, ss)
        k_ref[h] = jnp.concatenate([knope * rsk * gk_n, kr_h], axis=1).astype(k_ref.dtype)
        v_ref[h] = kvall[:, h * 256 + 128:(h + 1) * 256].astype(v_ref.dtype)


def _mla_prep(u, m, tm, pos_of_tile, tabs, prm, emit_qg):
    cc_tab, ss_tab = tabs
    nh = MLA_HEADS
    out_shape = [jax.ShapeDtypeStruct((nh, m, MLA_QKP), BF16),
                 jax.ShapeDtypeStruct((nh, m, MLA_QKP), BF16),
                 jax.ShapeDtypeStruct((nh, m, MLA_V), BF16),
                 jax.ShapeDtypeStruct((m, MLA_KV_LORA), F32)]
    out_specs = [pl.BlockSpec((nh, tm, MLA_QKP), lambda i: (0, i, 0)),
                 pl.BlockSpec((nh, tm, MLA_QKP), lambda i: (0, i, 0)),
                 pl.BlockSpec((nh, tm, MLA_V), lambda i: (0, i, 0)),
                 pl.BlockSpec((tm, MLA_KV_LORA), lambda i: (i, 0))]
    if emit_qg:
        out_shape.append(jax.ShapeDtypeStruct((nh, m, MLA_NOPE), BF16))
        out_specs.append(pl.BlockSpec((nh, tm, MLA_NOPE), lambda i: (0, i, 0)))
    full = lambda a: pl.BlockSpec(a.shape, lambda i: (0,) * a.ndim)
    return pl.pallas_call(
        functools.partial(_mla_prep_kernel, emit_qg=emit_qg), name="mla_prep",
        grid=(m // tm,),
        in_specs=[pl.BlockSpec((tm, MLA_Q_LORA), lambda i: (i, POOL_WIDTH // MLA_Q_LORA)),
                  pl.BlockSpec((tm, MLA_KV_LORA), lambda i: (i, (POOL_WIDTH + MLA_Q_LORA) // MLA_KV_LORA)),
                  pl.BlockSpec((tm, LANES), lambda i: (i, (POOL_WIDTH + MLA_Q_LORA + MLA_KV_LORA) // LANES)),
                  full(prm["g_qlat"]), full(prm["g_kvlat"]), full(prm["w_qb"]), full(prm["w_kvb"]),
                  full(prm["g_q"]), full(prm["g_k"]),
                  pl.BlockSpec((tm, LANES), lambda i: (pos_of_tile(i), 0)),
                  pl.BlockSpec((tm, LANES), lambda i: (pos_of_tile(i), 0))],
        out_specs=out_specs,
        out_shape=out_shape,
        compiler_params=_cparams("parallel"),
    )(u, u, u, prm["g_qlat"], prm["g_kvlat"], prm["w_qb"], prm["w_kvb"], prm["g_q"], prm["g_k"],
      cc_tab, ss_tab)


def _flash_kernel(q_ref, k_ref, v_ref, o_ref, *, tq, tk):
    i = pl.program_id(2)
    q = q_ref[...]
    per_q = tq // tk

    def step(j, carry, band):
        m, l, acc = carry
        start = pl.multiple_of(j * tk, tk)
        kb = k_ref[pl.ds(start, tk), :]
        vb = v_ref[pl.ds(start, tk), :]
        s = _dot_nt(q, kb)
        if band is not None:
            r = lax.broadcasted_iota(jnp.int32, (tq, tk), 0)
            c = lax.broadcasted_iota(jnp.int32, (tq, tk), 1) + band * tk
            s = jnp.where(c <= r, s, -jnp.inf)
        m_new = jnp.maximum(m, jnp.max(s, axis=1, keepdims=True))
        alpha = jnp.exp(m - m_new)
        p = jnp.exp(s - m_new)
        l = alpha * l + jnp.sum(p, axis=1, keepdims=True)
        acc = alpha * acc + _dot(p.astype(BF16), vb)
        return m_new, l, acc

    carry = (jnp.full((tq, 1), -jnp.inf, F32), jnp.zeros((tq, 1), F32), jnp.zeros((tq, MLA_V), F32))
    carry = lax.fori_loop(0, i * per_q, lambda j, c: step(j, c, None), carry)
    for band in range(per_q):
        carry = step(i * per_q + band, carry, band)
    _, l, acc = carry
    o_ref[...] = (acc / l).astype(o_ref.dtype)


FLASH_TQ = 1024
FLASH_TK = 1024


def _flash_attention(q, k, v, nb, t):
    tq, tk = FLASH_TQ, FLASH_TK
    nq = t // tq
    nh = MLA_HEADS
    return pl.pallas_call(
        functools.partial(_flash_kernel, tq=tq, tk=tk), name="flash_prompt",
        grid=(nb, nh, nq),
        in_specs=[pl.BlockSpec((None, tq, MLA_QKP), lambda b, h, i: (h, b * nq + i, 0)),
                  pl.BlockSpec((None, t, MLA_QKP), lambda b, h, i: (h, b, 0)),
                  pl.BlockSpec((None, t, MLA_V), lambda b, h, i: (h, b, 0))],
        out_specs=pl.BlockSpec((tq, MLA_V), lambda b, h, i: (b * nq + i, h)),
        out_shape=jax.ShapeDtypeStruct((nb * t, nh * MLA_V), BF16),
        compiler_params=_cparams("parallel", "parallel", "arbitrary"),
    )(q, k, v)


DEC_CHUNK_PAGES = 8
DEC_SUB_PAGES = 8
DEC_QROWS = 16


def _page_copy(pt_ref, cache_ref, buf_ref, sem_ref, layer, b, i, slot):
    return pltpu.make_async_copy(cache_ref.at[layer, pt_ref[b, i]], buf_ref.at[slot, i], sem_ref.at[slot])


def _decode_kernel(pt_ref, cache_ref, wt_ref, qp_ref, qr_ref, gr_ref, cc_ref, ss_ref, qf_ref, kn_ref, latn_ref,
                   o_ref, buf_ref, sem_ref, lhs_ref, *latb_refs, n_pages, nsamp, layer):
    b = pl.program_id(0)
    nh = MLA_HEADS
    slot = lax.rem(b, 2)
    nw = nh * MLA_NOPE

    def fetch(bb, sl):
        for i in range(n_pages):
            _page_copy(pt_ref, cache_ref, buf_ref, sem_ref, layer, bb, i, sl).start()

    @pl.when(b == 0)
    def _():
        fetch(b, slot)
        lhs_ref[0:nw, :] = wt_ref[...]

    @pl.when(b + 1 < nsamp)
    def _():
        fetch(b + 1, 1 - slot)

    for i in range(n_pages):
        _page_copy(pt_ref, cache_ref, buf_ref, sem_ref, layer, b, i, slot).wait()
    pages = [buf_ref.at[slot, i] for i in range(n_pages)]

    lhs_ref[nw:nw + DEC_QROWS, :] = qp_ref[...]
    lhs = lhs_ref[...]
    gr = gr_ref[...]
    half = MLA_ROPE // 2
    sub = DEC_SUB_PAGES * PAGE_SIZE
    chunk = DEC_CHUNK_PAGES * PAGE_SIZE
    qr = qr_ref[...]

    def tile_scores(c, t):
        i0 = c * DEC_CHUNK_PAGES + t * DEC_SUB_PAGES
        k0 = i0 * PAGE_SIZE
        pg = pages[i0:i0 + DEC_SUB_PAGES]
        latb = jnp.concatenate([p[0:MLA_KV_LORA, :] for p in pg], axis=1).astype(BF16)
        kr = jnp.concatenate([p[MLA_KV_LORA:MLA_KV_LORA + MLA_ROPE, :] for p in pg], axis=1)
        latb_refs[c][:, t * sub:(t + 1) * sub] = latb
        kt = _dot(lhs, latb)
        ssn = jnp.concatenate(
            [jnp.sum(jnp.square(kt[h * 128:(h + 1) * 128, :]), axis=0, keepdims=True) for h in range(nh)],
            axis=0)
        sn = kt[nw:nw + nh, :]
        krg = kr * gr
        swapped = jnp.concatenate([krg[half:, :], krg[:half, :]], axis=0)
        rot = krg * cc_ref[:, k0:k0 + sub] + swapped * ss_ref[:, k0:k0 + sub]
        feat = jnp.concatenate([rot, kr * kr], axis=0).astype(BF16)
        rr = _dot(qr, feat)
        rs = lax.rsqrt((ssn + rr[nh:nh + 1]) * (1.0 / MLA_QK) + EPS)
        return rs * (sn + rr[:nh])

    def chunk_update(c, s, state):
        m_run, l_run, acc = state
        m_new = jnp.maximum(m_run, jnp.max(s, axis=1, keepdims=True))
        alpha = jnp.exp(m_run - m_new)
        p = jnp.exp(s - m_new)
        l_run = alpha * l_run + jnp.sum(p, axis=1, keepdims=True)
        acc = alpha * acc + _dot_nt(p.astype(BF16), latb_refs[c][...])
        return m_new, l_run, acc

    n_chunks = n_pages // DEC_CHUNK_PAGES
    n_tiles = DEC_CHUNK_PAGES // DEC_SUB_PAGES
    state = (jnp.full((nh, 1), -jnp.inf, F32), jnp.zeros((nh, 1), F32), jnp.zeros((nh, MLA_KV_LORA), F32))
    s_prev = None
    for c in range(n_chunks):
        s_cur = jnp.concatenate([tile_scores(c, t) for t in range(n_tiles)], axis=1)
        if s_prev is not None:
            state = chunk_update(c - 1, s_prev, state)
        s_prev = s_cur
    m_run, l_run, acc = chunk_update(n_chunks - 1, s_prev, state)

    s_new = jnp.sum(qf_ref[...] * kn_ref[...], axis=1, keepdims=True)
    m_fin = jnp.maximum(m_run, s_new)
    a_fin = jnp.exp(m_run - m_fin)
    p_new = jnp.exp(s_new - m_fin)
    l_fin = a_fin * l_run + p_new
    lat_new = latn_ref[...].astype(BF16).astype(F32)
    o_ref[...] = (a_fin * acc + p_new.astype(BF16).astype(F32) * lat_new) / l_fin


def _mla_decode(cache, layer, page_table, wt_nope, qp, qr, g_rope, cc_keys, ss_keys, qf, kn, lat_new):
    nsamp, n_pages = page_table.shape
    assert n_pages % DEC_CHUNK_PAGES == 0 and DEC_CHUNK_PAGES % DEC_SUB_PAGES == 0
    n_keys = n_pages * PAGE_SIZE
    kvd = cache.shape[-1]
    cache = jnp.transpose(cache, (0, 1, 3, 2))
    nh = MLA_HEADS
    grid_spec = pltpu.PrefetchScalarGridSpec(
        num_scalar_prefetch=1,
        grid=(nsamp,),
        in_specs=[
            pl.BlockSpec(memory_space=pl.ANY),
            pl.BlockSpec(wt_nope.shape, lambda b, pt: (0, 0)),
            pl.BlockSpec((None, DEC_QROWS, MLA_KV_LORA), lambda b, pt: (b, 0, 0)),
            pl.BlockSpec((None, DEC_QROWS, LANES), lambda b, pt: (b, 0, 0)),
            pl.BlockSpec((MLA_ROPE, 1), lambda b, pt: (0, 0)),
            pl.BlockSpec((MLA_ROPE, n_keys), lambda b, pt: (0, 0)),
            pl.BlockSpec((MLA_ROPE, n_keys), lambda b, pt: (0, 0)),
            pl.BlockSpec((None, nh, MLA_QKP), lambda b, pt: (b, 0, 0)),
            pl.BlockSpec((None, nh, MLA_QKP), lambda b, pt: (b, 0, 0)),
            pl.BlockSpec((None, 1, MLA_KV_LORA), lambda b, pt: (b, 0, 0)),
        ],
        out_specs=pl.BlockSpec((None, nh, MLA_KV_LORA), lambda b, pt: (b, 0, 0)),
        scratch_shapes=[pltpu.VMEM((2, n_pages, kvd, PAGE_SIZE), F32), pltpu.SemaphoreType.DMA((2,)),
                        pltpu.VMEM((nh * MLA_NOPE + DEC_QROWS, MLA_KV_LORA), BF16),
                        ] + [pltpu.VMEM((MLA_KV_LORA, DEC_CHUNK_PAGES * PAGE_SIZE), BF16)
                             for _ in range(n_pages // DEC_CHUNK_PAGES)],
    )
    return pl.pallas_call(
        functools.partial(_decode_kernel, n_pages=n_pages, nsamp=nsamp, layer=layer), name="mla_decode",
        grid_spec=grid_spec,
        out_shape=jax.ShapeDtypeStruct((nsamp, nh, MLA_KV_LORA), F32),
        compiler_params=_cparams("arbitrary"),
    )(page_table, cache, wt_nope, qp, qr, g_rope, cc_keys, ss_keys, qf, kn, lat_new)


def _ssd_gate_norm(y, xs, z, dskip, gnorm):
    y = (y + dskip * xs) * _silu(z)
    gw = SSD_D_INNER // SSD_GROUPS
    outs = []
    for g in range(SSD_GROUPS):
        yg = y[:, g * gw:(g + 1) * gw]
        outs.append(yg * lax.rsqrt(jnp.mean(yg * yg, axis=-1, keepdims=True) + EPS))
    return jnp.concatenate(outs, axis=1) * gnorm


def _ssd_prompt_kernel(z_ref, x_ref, bc_ref, dt_ref, bufx_ref, bufbc_ref, wx_ref, bx_ref, wbc_ref, bbc_ref,
                       dtb_ref, alog_ref, dskip_ref, gnorm_ref, h0_ref, tri_ref, exp_ref,
                       y_ref, hout_ref, extx_ref, extbc_ref, ht_ref, *, nc):
    c = pl.program_id(1)
    first = c == 0
    l = SSD_CHUNK
    xs = _silu(_conv_chunk(extx_ref, x_ref[...], bufx_ref, wx_ref, bx_ref, first))
    bcs = _silu(_conv_chunk(extbc_ref, bc_ref[...], bufbc_ref, wbc_ref, bbc_ref, first))
    gs = SSD_GROUPS * SSD_STATE
    bm, cm = bcs[:, :gs], bcs[:, gs:]

    @pl.when(first)
    def _():
        for r in range(SSD_D_INNER // LANES):
            ht_ref[:, r * LANES:(r + 1) * LANES] = h0_ref[r * LANES:(r + 1) * LANES, :].T

    dt = _softplus(dt_ref[...] + dtb_ref[...])
    a = -jnp.exp(alog_ref[...])
    acum = _dot_sel_left(tri_ref[...], dt * a)
    acum_t = acum.T
    dt_t = dt.T
    a_last = acum[l - 1:l, :]
    dend = jnp.exp(a_last - acum)
    sel = exp_ref[...]
    w_state = _dot_sel(dt * dend, sel)
    e_acc = _dot_sel(jnp.exp(acum), sel)
    e_last = _dot_sel(jnp.exp(a_last), sel)

    rows = lax.broadcasted_iota(jnp.int32, (l, l), 0)
    cols = lax.broadcasted_iota(jnp.int32, (l, l), 1)
    causal = cols <= rows
    lane = lax.broadcasted_iota(jnp.int32, (l, LANES), 1)
    xs_b = xs.astype(BF16)
    hpg = SSD_HEADS // SSD_GROUPS
    gw = SSD_D_INNER // SSD_GROUPS
    y_groups = []
    for g in range(SSD_GROUPS):
        gcols = slice(g * gw, (g + 1) * gw)
        bg = bm[:, g * SSD_STATE:(g + 1) * SSD_STATE]
        cg_b = cm[:, g * SSD_STATE:(g + 1) * SSD_STATE].astype(BF16)
        cb = _dot_nt(cg_b, bg.astype(BF16))
        ht_g = ht_ref[:, gcols]
        y_off = _dot(cg_b, ht_g.astype(BF16))
        y_pairs = []
        for pr in range(hpg // 2):
            col = (g * hpg + pr * 2) * SSD_HEAD_DIM
            res = []
            for hh in range(2):
                h = g * hpg + pr * 2 + hh
                diff = acum[:, h:h + 1] - acum_t[h:h + 1, :]
                lmat = jnp.exp(jnp.where(causal, diff, -jnp.inf)) * dt_t[h:h + 1, :]
                res.append(_dot((cb * lmat).astype(BF16), xs_b[:, col:col + LANES]))
            y_pairs.append(jnp.where(lane < SSD_HEAD_DIM, res[0], res[1]))
        y_groups.append(jnp.concatenate(y_pairs, axis=1) + e_acc[:, gcols] * y_off)
        xw = (xs[:, gcols] * w_state[:, gcols]).astype(BF16)
        ht_ref[:, gcols] = e_last[:, gcols] * ht_g + _dot(bg.T.astype(BF16), xw)
    y = jnp.concatenate(y_groups, axis=1)
    y_ref[...] = _ssd_gate_norm(y, xs, z_ref[...], dskip_ref[...], gnorm_ref[...]).astype(y_ref.dtype)

    @pl.when(c == nc - 1)
    def _():
        for r in range(SSD_D_INNER // LANES):
            hout_ref[r * LANES:(r + 1) * LANES, :] = ht_ref[:, r * LANES:(r + 1) * LANES].T


def _dot_sel_left(sel, x):
    hi, mid, lo = _split3(x)
    return _dot(sel, hi) + _dot(sel, mid) + _dot(sel, lo)


def _ssd_consts():
    l = SSD_CHUNK
    tri = (jnp.arange(l)[:, None] >= jnp.arange(l)[None, :]).astype(BF16)
    sel = (jnp.arange(LANES)[:, None] == (jnp.arange(SSD_D_INNER)[None, :] // SSD_HEAD_DIM)).astype(BF16)
    return tri, sel


def _ssd_prompt(u, u_dt, bufx, bufbc, h0, prm, nb, t):
    l = SSD_CHUNK
    nc = t // l
    tri, sel = _ssd_consts()
    full = lambda a: pl.BlockSpec(a.shape, lambda b, c: (0,) * a.ndim)
    row = lambda width, colblk: pl.BlockSpec((l, width), lambda b, c: (b * nc + c, colblk))
    y, hout = pl.pallas_call(
        functools.partial(_ssd_prompt_kernel, nc=nc), name="ssd_prompt",
        grid=(nb, nc),
        in_specs=[row(SSD_D_INNER, 0), row(SSD_D_INNER, 1), row(SSD_BC, 4), row(LANES, 0),
                  pl.BlockSpec((None, CONV_WIDTH - 1, SSD_D_INNER), lambda b, c: (b, 0, 0)),
                  pl.BlockSpec((None, CONV_WIDTH - 1, SSD_BC), lambda b, c: (b, 0, 0)),
                  full(prm["w_conv_x"]), full(prm["b_conv_x"]), full(prm["w_conv_bc"]), full(prm["b_conv_bc"]),
                  full(prm["dt_bias"]), full(prm["a_log"]), full(prm["d_skip"]), full(prm["g_ssd_norm"]),
                  pl.BlockSpec((None, SSD_D_INNER, SSD_STATE), lambda b, c: (b, 0, 0)),
                  full(tri), full(sel)],
        out_specs=[pl.BlockSpec((l, SSD_D_INNER), lambda b, c: (b * nc + c, 0)),
                   pl.BlockSpec((None, SSD_D_INNER, SSD_STATE), lambda b, c: (b, 0, 0))],
        out_shape=[jax.ShapeDtypeStruct((nb * t, SSD_D_INNER), BF16),
                   jax.ShapeDtypeStruct((nb, SSD_D_INNER, SSD_STATE), F32)],
        scratch_shapes=[pltpu.VMEM((CONV_HALO + l, SSD_D_INNER), F32),
                        pltpu.VMEM((CONV_HALO + l, SSD_BC), F32),
                        pltpu.VMEM((SSD_STATE, SSD_D_INNER), F32)],
        compiler_params=_cparams("parallel", "arbitrary"),
    )(u, u, u, u_dt, bufx, bufbc, prm["w_conv_x"], prm["b_conv_x"], prm["w_conv_bc"], prm["b_conv_bc"],
      prm["dt_bias"], prm["a_log"], prm["d_skip"], prm["g_ssd_norm"], h0, tri, sel)
    return y, hout


def _ssd_step_prep_kernel(x_ref, bc_ref, dt_ref, bufx_ref, bufbc_ref, wx_ref, bx_ref, wbc_ref, bbc_ref,
                          dtb_ref, alog_ref, exp_ref, xs_ref, bcs_ref, dtx_ref, dae_ref):
    xs = _silu(_conv_step(bufx_ref, x_ref[...], wx_ref, bx_ref))
    xs_ref[...] = xs
    bcs_ref[...] = _silu(_conv_step(bufbc_ref, bc_ref[...], wbc_ref, bbc_ref))
    dt = _softplus(dt_ref[...] + dtb_ref[...])
    da = jnp.exp(dt * (-jnp.exp(alog_ref[...])))
    sel = exp_ref[...]
    dtx_ref[...] = _dot_sel(dt, sel) * xs
    dae_ref[...] = _dot_sel(da, sel)


def _ssd_step_prep(u, u_dt, bufx_t, bufbc_t, prm):
    m = u_dt.shape[0]
    _, sel = _ssd_consts()
    full = lambda a: pl.BlockSpec(a.shape, lambda i: (0,) * a.ndim)
    return pl.pallas_call(
        _ssd_step_prep_kernel, name="ssd_step_prep",
        grid=(1,),
        in_specs=[pl.BlockSpec((m, SSD_D_INNER), lambda i: (0, 1)),
                  pl.BlockSpec((m, SSD_BC), lambda i: (0, 4)),
                  full(u_dt), full(bufx_t), full(bufbc_t),
                  full(prm["w_conv_x"]), full(prm["b_conv_x"]), full(prm["w_conv_bc"]), full(prm["b_conv_bc"]),
                  full(prm["dt_bias"]), full(prm["a_log"]), full(sel)],
        out_specs=[pl.BlockSpec((m, SSD_D_INNER), lambda i: (0, 0)),
                   pl.BlockSpec((m, SSD_BC), lambda i: (0, 0)),
                   pl.BlockSpec((m, SSD_D_INNER), lambda i: (0, 0)),
                   pl.BlockSpec((m, SSD_D_INNER), lambda i: (0, 0))],
        out_shape=[jax.ShapeDtypeStruct((m, SSD_D_INNER), F32),
                   jax.ShapeDtypeStruct((m, SSD_BC), F32),
                   jax.ShapeDtypeStruct((m, SSD_D_INNER), F32),
                   jax.ShapeDtypeStruct((m, SSD_D_INNER), F32)],
        compiler_params=_cparams("arbitrary"),
    )(u, u, u_dt, bufx_t, bufbc_t, prm["w_conv_x"], prm["b_conv_x"], prm["w_conv_bc"], prm["b_conv_bc"],
      prm["dt_bias"], prm["a_log"], sel)


SSD_STEP_SAMPLES = 2


def _ssd_step_state_kernel(h0_ref, cols_ref, b_ref, c_ref, hn_ref, y_ref):
    nr = SSD_D_INNER // LANES
    rows_per_group = SSD_D_INNER // SSD_GROUPS // LANES
    gw = SSD_D_INNER // SSD_GROUPS
    for s in range(SSD_STEP_SAMPLES):
        pad = jnp.concatenate([cols_ref[s], jnp.zeros((LANES - 2 * nr, LANES), F32)], axis=0)
        ct = pad.T
        for r in range(nr):
            g = r // rows_per_group
            hn_ref[s, r * LANES:(r + 1) * LANES, :] = (
                ct[:, nr + r:nr + r + 1] * h0_ref[s, r * LANES:(r + 1) * LANES, :]
                + ct[:, r:r + 1] * b_ref[s, g:g + 1, :])
        res = _dot_nt(c_ref[s].astype(BF16), hn_ref[s].astype(BF16))
        y_ref[s] = jnp.concatenate([res[g:g + 1, g * gw:(g + 1) * gw] for g in range(SSD_GROUPS)], axis=1)


def _ssd_step_state(h0, cols, bmat, cmat):
    m = h0.shape[0]
    sps = SSD_STEP_SAMPLES
    assert m % sps == 0
    per = lambda shape: pl.BlockSpec((sps,) + shape, lambda i: (i, 0, 0))
    return pl.pallas_call(
        _ssd_step_state_kernel, name="ssd_step_state",
        grid=(m // sps,),
        in_specs=[per((SSD_D_INNER, SSD_STATE)), per((2 * SSD_D_INNER // LANES, LANES)),
                  per((8, SSD_STATE)), per((8, SSD_STATE))],
        out_specs=[per((SSD_D_INNER, SSD_STATE)), per((1, SSD_D_INNER))],
        out_shape=[jax.ShapeDtypeStruct((m, SSD_D_INNER, SSD_STATE), F32),
                   jax.ShapeDtypeStruct((m, 1, SSD_D_INNER), F32)],
        compiler_params=_cparams("parallel"),
    )(h0, cols, bmat, cmat)


def _ssd_step_post_kernel(y_ref, xs_ref, z_ref, dskip_ref, gnorm_ref, o_ref):
    o_ref[...] = _ssd_gate_norm(y_ref[...], xs_ref[...], z_ref[...], dskip_ref[...],
                                gnorm_ref[...]).astype(o_ref.dtype)


def _ssd_step_post(y, xs, u, prm):
    m = y.shape[0]
    full = lambda a: pl.BlockSpec(a.shape, lambda i: (0,) * a.ndim)
    return pl.pallas_call(
        _ssd_step_post_kernel, name="ssd_step_post",
        grid=(1,),
        in_specs=[full(y), full(xs), pl.BlockSpec((m, SSD_D_INNER), lambda i: (0, 0)),
                  full(prm["d_skip"]), full(prm["g_ssd_norm"])],
        out_specs=pl.BlockSpec((m, SSD_D_INNER), lambda i: (0, 0)),
        out_shape=jax.ShapeDtypeStruct((m, SSD_D_INNER), BF16),
        compiler_params=_cparams("arbitrary"),
    )(y, xs, u, prm["d_skip"], prm["g_ssd_norm"])


def _lru_gates(xc, wr_ref, br_ref, wi_ref, bi_ref, lam_ref, is_pos0):
    xb = xc.astype(BF16)
    rs, is_ = [], []
    for n in range(LRU_BLOCKS):
        blk = xb[:, n * LRU_BLOCK_DIM:(n + 1) * LRU_BLOCK_DIM]
        rs.append(_dot(blk, wr_ref[n]))
        is_.append(_dot(blk, wi_ref[n]))
    r = _sigmoid(jnp.concatenate(rs, axis=1) + br_ref[...])
    i = _sigmoid(jnp.concatenate(is_, axis=1) + bi_ref[...])
    log_a = -LRU_C * r * _softplus(-lam_ref[...])
    a = jnp.exp(log_a)
    th = jnp.tanh(log_a)
    mult = jnp.sqrt(-2.0 * th / (1.0 - th))
    if is_pos0 is not False:
        mult = jnp.where(is_pos0, 1.0, mult)
    return a, mult * i * xc


def _shift_rows(x, d, fill):
    rolled = pltpu.roll(x, d, 0)
    row = lax.broadcasted_iota(jnp.int32, x.shape, 0)
    return jnp.where(row >= d, rolled, fill)


def _lru_prompt_kernel(lx_ref, lg_ref, buf_ref, wc_ref, bc_ref, wr_ref, br_ref, wi_ref, bi_ref, lam_ref, h0_ref,
                       y_ref, hout_ref, ext_ref, h_ref, *, tc, nc, pos0):
    c = pl.program_id(1)
    first = c == 0

    @pl.when(first)
    def _():
        h_ref[...] = h0_ref[...]

    xc = _conv_chunk(ext_ref, lx_ref[...], buf_ref, wc_ref, bc_ref, first)
    pos = pos0 + c * tc + lax.broadcasted_iota(jnp.int32, (tc, 1), 0)
    a, b = _lru_gates(xc, wr_ref, br_ref, wi_ref, bi_ref, lam_ref, pos == 0)
    d = 1
    while d < tc:
        b = b + a * _shift_rows(b, d, 0.0)
        a = a * _shift_rows(a, d, 1.0)
        d *= 2
    h = b + a * h_ref[...]
    h_ref[...] = h[tc - 1:tc, :]
    y_ref[...] = (h * _gelu_tanh(lg_ref[...])).astype(y_ref.dtype)

    @pl.when(c == nc - 1)
    def _():
        hout_ref[...] = h[tc - 1:tc, :]


def _lru_prompt(u, buf, h0, prm, nb, t, pos0):
    tc = 256
    nc = t // tc
    full = lambda a: pl.BlockSpec(a.shape, lambda b, c: (0,) * a.ndim)
    y, hout = pl.pallas_call(
        functools.partial(_lru_prompt_kernel, tc=tc, nc=nc, pos0=pos0), name="lru_prompt",
        grid=(nb, nc),
        in_specs=[pl.BlockSpec((tc, LRU_WIDTH), lambda b, c: (b * nc + c, 5)),
                  pl.BlockSpec((tc, LRU_WIDTH), lambda b, c: (b * nc + c, 6)),
                  pl.BlockSpec((None, CONV_WIDTH - 1, LRU_WIDTH), lambda b, c: (b, 0, 0)),
                  full(prm["w_conv_lru"]), full(prm["b_conv_lru"]), full(prm["w_lru_r"]), full(prm["b_lru_r"]),
                  full(prm["w_lru_i"]), full(prm["b_lru_i"]), full(prm["lru_lambda"]),
                  pl.BlockSpec((None, 1, LRU_WIDTH), lambda b, c: (b, 0, 0))],
        out_specs=[pl.BlockSpec((tc, LRU_WIDTH), lambda b, c: (b * nc + c, 0)),
                   pl.BlockSpec((None, 1, LRU_WIDTH), lambda b, c: (b, 0, 0))],
        out_shape=[jax.ShapeDtypeStruct((nb * t, LRU_WIDTH), BF16),
                   jax.ShapeDtypeStruct((nb, 1, LRU_WIDTH), F32)],
        scratch_shapes=[pltpu.VMEM((CONV_HALO + tc, LRU_WIDTH), F32), pltpu.VMEM((1, LRU_WIDTH), F32)],
        compiler_params=_cparams("parallel", "arbitrary"),
    )(u, u, buf, prm["w_conv_lru"], prm["b_conv_lru"], prm["w_lru_r"], prm["b_lru_r"], prm["w_lru_i"],
      prm["b_lru_i"], prm["lru_lambda"], h0)
    return y, hout


def _lru_step_kernel(lx_ref, lg_ref, buf_ref, wc_ref, bc_ref, wr_ref, br_ref, wi_ref, bi_ref, lam_ref, h0_ref,
                     y_ref, hout_ref, *, pos0):
    xc = _conv_step(buf_ref, lx_ref[...], wc_ref, bc_ref)
    a, b = _lru_gates(xc, wr_ref, br_ref, wi_ref, bi_ref, lam_ref, pos0 == 0)
    h = b + a * h0_ref[...]
    hout_ref[...] = h
    y_ref[...] = (h * _gelu_tanh(lg_ref[...])).astype(y_ref.dtype)


def _lru_step(u, buf_t, h0, prm, pos0):
    m = h0.shape[0]
    full = lambda a: pl.BlockSpec(a.shape, lambda i: (0,) * a.ndim)
    return pl.pallas_call(
        functools.partial(_lru_step_kernel, pos0=pos0), name="lru_step",
        grid=(1,),
        in_specs=[pl.BlockSpec((m, LRU_WIDTH), lambda i: (0, 5)),
                  pl.BlockSpec((m, LRU_WIDTH), lambda i: (0, 6)),
                  full(buf_t), full(prm["w_conv_lru"]), full(prm["b_conv_lru"]), full(prm["w_lru_r"]),
                  full(prm["b_lru_r"]), full(prm["w_lru_i"]), full(prm["b_lru_i"]), full(prm["lru_lambda"]),
                  full(h0)],
        out_specs=[pl.BlockSpec((m, LRU_WIDTH), lambda i: (0, 0)),
                   pl.BlockSpec((m, LRU_WIDTH), lambda i: (0, 0))],
        out_shape=[jax.ShapeDtypeStruct((m, LRU_WIDTH), BF16),
                   jax.ShapeDtypeStruct((m, LRU_WIDTH), F32)],
        compiler_params=_cparams("arbitrary"),
    )(u, u, buf_t, prm["w_conv_lru"], prm["b_conv_lru"], prm["w_lru_r"], prm["b_lru_r"], prm["w_lru_i"],
      prm["b_lru_i"], prm["lru_lambda"], h0)


def _rope_tables(n_pos):
    half = MLA_ROPE // 2
    inv = ROPE_THETA ** (-np.arange(half, dtype=np.float64) * (2.0 / MLA_ROPE))
    ang = np.arange(n_pos, dtype=np.float64)[:, None] * inv[None, :]
    cos, sin = np.cos(ang), np.sin(ang)
    cc = jnp.asarray(np.concatenate([cos, cos], axis=1), F32)
    ss = jnp.asarray(np.concatenate([-sin, sin], axis=1), F32)
    return cc, ss


def _even_params(e, w_in_e, w_pool, s_pool, g_qlat, w_qb, g_kvlat, w_kvb, g_q, g_k, w_out_e):
    nh = MLA_HEADS
    w_in = jnp.pad(w_in_e[e], ((0, 0), (0, EVEN_IN_PAD - w_in_e.shape[2]))).astype(BF16)
    wq = w_qb[e].reshape(MLA_Q_LORA, nh, MLA_QK)
    wq_nope = wq[:, :, :MLA_NOPE].reshape(MLA_Q_LORA, nh * MLA_NOPE)
    wq_rope = jnp.pad(wq[:, :, MLA_NOPE:], ((0, 0), (0, 0), (0, LANES - MLA_ROPE))).reshape(MLA_Q_LORA, nh * LANES)
    wkv = w_kvb[e].reshape(MLA_KV_LORA, nh, MLA_NOPE + MLA_V)
    w_nope_t = jnp.transpose(wkv[:, :, :MLA_NOPE], (1, 2, 0))
    pad_g = lambda g: jnp.pad(g, (0, MLA_QKP - MLA_QK)).reshape(1, MLA_QKP)
    return dict(
        w_in=w_in,
        w_pool=w_pool[e].astype(BF16),
        s_pool=s_pool[e].reshape(1, POOL_WIDTH),
        g_qlat=g_qlat[e].reshape(1, MLA_Q_LORA),
        g_kvlat=g_kvlat[e].reshape(1, MLA_KV_LORA),
        w_qb=jnp.concatenate([wq_nope, wq_rope], axis=1).astype(BF16),
        w_kvb=w_kvb[e].astype(BF16),
        g_q=pad_g(g_q[e]), g_k=pad_g(g_k[e]),
        g_k_rope=g_k[e][MLA_NOPE:].reshape(MLA_ROPE, 1),
        w_nope_t=w_nope_t.astype(BF16),
        w_v=jnp.transpose(wkv[:, :, MLA_NOPE:], (1, 0, 2)).astype(BF16),
        w_out_pool=w_out_e[e][:POOL_WIDTH].astype(BF16),
        w_out_att=w_out_e[e][POOL_WIDTH:].astype(BF16),
    )


def _odd_params(o, w_in_o, w_conv_ssd, b_conv_ssd, dt_bias, a_log, d_skip, g_ssd_norm, w_conv_lru, b_conv_lru,
                w_lru_r, b_lru_r, w_lru_i, b_lru_i, lru_lambda, w_out_o):
    j2 = SSD_D_INNER + SSD_D_INNER + SSD_BC
    j3 = j2 + SSD_HEADS
    w = w_in_o[o]
    pad_heads = lambda v: jnp.pad(v, (0, LANES - SSD_HEADS)).reshape(1, LANES)
    return dict(
        w_in=jnp.concatenate([w[:, :j2].astype(BF16), w[:, j3:].astype(BF16)], axis=1),
        w_in_dt=jnp.pad(w[:, j2:j3], ((0, 0), (0, LANES - SSD_HEADS))).astype(BF16),
        w_conv_x=w_conv_ssd[o][:, :SSD_D_INNER], w_conv_bc=w_conv_ssd[o][:, SSD_D_INNER:],
        b_conv_x=b_conv_ssd[o][:SSD_D_INNER].reshape(1, -1), b_conv_bc=b_conv_ssd[o][SSD_D_INNER:].reshape(1, -1),
        dt_bias=pad_heads(dt_bias[o]), a_log=pad_heads(a_log[o]),
        d_skip=jnp.repeat(d_skip[o], SSD_HEAD_DIM).reshape(1, SSD_D_INNER),
        g_ssd_norm=g_ssd_norm[o].reshape(1, SSD_D_INNER),
        w_conv_lru=w_conv_lru[o], b_conv_lru=b_conv_lru[o].reshape(1, LRU_WIDTH),
        w_lru_r=w_lru_r[o].astype(BF16), b_lru_r=b_lru_r[o].reshape(1, LRU_WIDTH),
        w_lru_i=w_lru_i[o].astype(BF16), b_lru_i=b_lru_i[o].reshape(1, LRU_WIDTH),
        lru_lambda=lru_lambda[o].reshape(1, LRU_WIDTH),
        w_out_ssd=w_out_o[o][:SSD_D_INNER].astype(BF16),
        w_out_lru=w_out_o[o][SSD_D_INNER:].astype(BF16),
    )


def _pad_tab(tab):
    return jnp.pad(tab, ((0, 0), (0, LANES - tab.shape[1])))


def _even_layer_prompt(rows, hn, prm, nb, t):
    u = _matmul(rows, [hn], [prm["w_in"]], name="in_even", tn=EVEN_IN_PAD)
    zero_buf = jnp.zeros((nb, POOL_BUF, POOL_WIDTH), F32)
    y_pool = _pool_prompt(u, zero_buf, prm["w_pool"], prm["s_pool"], nb, t, 0)
    cc, ss = _rope_tables(t)
    tm = 256
    tiles = t // tm
    q, k, v, lat = _mla_prep(u, rows.m, tm, lambda i: i % tiles, (_pad_tab(cc), _pad_tab(ss)), prm, False)
    y_att = _flash_attention(q, k, v, nb, t)
    u3 = u.reshape(nb, t, EVEN_IN_PAD)
    kr = u3[:, :, POOL_WIDTH + MLA_Q_LORA + MLA_KV_LORA:POOL_WIDTH + MLA_Q_LORA + MLA_KV_LORA + MLA_ROPE]
    mla_rows = jnp.concatenate([lat.reshape(nb, t, MLA_KV_LORA), kr], axis=-1)
    pool_new = u3[:, t - POOL_BUF:, :POOL_WIDTH]
    return [y_pool, y_att], [prm["w_out_pool"], prm["w_out_att"]], mla_rows, pool_new


def _even_layer_sample(rows, hn, prm, pool_buf, cache, layer, page_table, pos0):
    m = rows.m
    nh = MLA_HEADS
    u = _matmul(rows, [hn], [prm["w_in"]], name="in_even", tn=640)
    y_pool = _pool_step(u, jnp.transpose(pool_buf, (1, 0, 2)), prm["w_pool"], prm["s_pool"], pos0)
    cc, ss = _rope_tables(pos0 + 1)
    cc_new = jnp.broadcast_to(_pad_tab(cc[pos0:]), (m, LANES))
    ss_new = jnp.broadcast_to(_pad_tab(ss[pos0:]), (m, LANES))
    q, k, v, lat, qg = _mla_prep(u, m, m, lambda i: 0, (cc_new, ss_new), prm, True)
    qp = _bmm_heads(qg, prm["w_nope_t"], BF16)
    qp = jnp.pad(jnp.transpose(qp, (1, 0, 2)), ((0, 0), (0, DEC_QROWS - nh), (0, 0)))
    q_m = jnp.transpose(q, (1, 0, 2))
    qr = jnp.pad(q_m[:, :, MLA_NOPE:MLA_NOPE + MLA_ROPE], ((0, 0), (0, 0), (0, LANES - MLA_ROPE)))
    ones_row = jnp.concatenate([jnp.zeros((m, 1, MLA_ROPE), BF16), jnp.ones((m, 1, LANES - MLA_ROPE), BF16)], axis=2)
    qr = jnp.concatenate([qr, ones_row, jnp.zeros((m, DEC_QROWS - nh - 1, LANES), BF16)], axis=1)
    o_lat = _mla_decode(cache, layer, page_table, prm["w_nope_t"].reshape(nh * MLA_NOPE, MLA_KV_LORA), qp, qr,
                        prm["g_k_rope"], cc[:pos0].T, ss[:pos0].T, q_m.astype(F32),
                        jnp.transpose(k, (1, 0, 2)).astype(F32), lat.reshape(m, 1, MLA_KV_LORA))
    y_att = _bmm_heads(jnp.transpose(o_lat, (1, 0, 2)).astype(BF16), prm["w_v"], BF16)
    y_att = jnp.transpose(y_att, (1, 0, 2)).reshape(m, nh * MLA_V)
    kr = u[:, POOL_WIDTH + MLA_Q_LORA + MLA_KV_LORA:POOL_WIDTH + MLA_Q_LORA + MLA_KV_LORA + MLA_ROPE]
    mla_rows = jnp.concatenate([lat, kr], axis=-1).reshape(m, 1, MLA_KV_LORA + MLA_ROPE)
    pool_new = jnp.concatenate([pool_buf[:, 1:], u[:, None, :POOL_WIDTH]], axis=1)
    return [y_pool, y_att], [prm["w_out_pool"], prm["w_out_att"]], mla_rows, pool_new


def _odd_layer_prompt(rows, hn, prm, nb, t):
    u = _matmul(rows, [hn], [prm["w_in"]], name="in_odd", tm=1024, tn=1792)
    u_dt = _matmul(rows, [hn], [prm["w_in_dt"]], name="in_odd_dt", tn=LANES)
    k1 = CONV_WIDTH - 1
    y_ssd, h_ssd = _ssd_prompt(u, u_dt, jnp.zeros((nb, k1, SSD_D_INNER), F32), jnp.zeros((nb, k1, SSD_BC), F32),
                               jnp.zeros((nb, SSD_D_INNER, SSD_STATE), F32), prm, nb, t)
    y_lru, h_lru = _lru_prompt(u, jnp.zeros((nb, k1, LRU_WIDTH), F32), jnp.zeros((nb, 1, LRU_WIDTH), F32),
                               prm, nb, t, 0)
    u3 = u.reshape(nb, t, -1)
    sconv = u3[:, t - k1:, SSD_D_INNER:2 * SSD_D_INNER + SSD_BC]
    lconv = u3[:, t - k1:, 2 * SSD_D_INNER + SSD_BC:2 * SSD_D_INNER + SSD_BC + LRU_WIDTH]
    return ([y_ssd, y_lru], [prm["w_out_ssd"], prm["w_out_lru"]], sconv,
            h_ssd.reshape(nb, SSD_HEADS, SSD_HEAD_DIM, SSD_STATE), lconv, h_lru.reshape(nb, LRU_WIDTH))


def _odd_layer_sample(rows, hn, prm, sconv_buf, ssd_state, lconv_buf, lru_state, pos0):
    m = rows.m
    u = _matmul(rows, [hn], [prm["w_in"]], name="in_odd", tn=1024)
    u_dt = _matmul(rows, [hn], [prm["w_in_dt"]], name="in_odd_dt", tn=LANES)
    sconv_t = jnp.transpose(sconv_buf, (1, 0, 2))
    xs, bcs, dtx, dae = _ssd_step_prep(u, u_dt, sconv_t[:, :, :SSD_D_INNER], sconv_t[:, :, SSD_D_INNER:], prm)
    nr = SSD_D_INNER // LANES
    cols = jnp.concatenate([dtx.reshape(m, nr, LANES), dae.reshape(m, nr, LANES)], axis=1)
    gs = SSD_GROUPS * SSD_STATE
    pad8 = lambda a: jnp.pad(a.reshape(m, SSD_GROUPS, SSD_STATE), ((0, 0), (0, 8 - SSD_GROUPS), (0, 0)))
    h_new, y = _ssd_step_state(ssd_state.reshape(m, SSD_D_INNER, SSD_STATE), cols,
                               pad8(bcs[:, :gs]), pad8(bcs[:, gs:]))
    y_ssd = _ssd_step_post(y.reshape(m, SSD_D_INNER), xs, u, prm)
    y_lru, h_lru = _lru_step(u, jnp.transpose(lconv_buf, (1, 0, 2)), lru_state, prm, pos0)
    u_xbc = u[:, SSD_D_INNER:2 * SSD_D_INNER + SSD_BC]
    u_lx = u[:, 2 * SSD_D_INNER + SSD_BC:2 * SSD_D_INNER + SSD_BC + LRU_WIDTH]
    sconv = jnp.concatenate([sconv_buf[:, 1:], u_xbc[:, None]], axis=1)
    lconv = jnp.concatenate([lconv_buf[:, 1:], u_lx[:, None]], axis=1)
    return ([y_ssd, y_lru], [prm["w_out_ssd"], prm["w_out_lru"]], sconv,
            h_new.reshape(m, SSD_HEADS, SSD_HEAD_DIM, SSD_STATE), lconv, h_lru)


def kernel(x_prompt, x_sample, cache_mla, state_pool, state_ssd_conv, state_ssd, state_lru_conv, state_lru, page_table, c_prompt, c_sample, g_norm1, g_norm2, w_mod, b_mod, w_mlp1, w_mlp2, w_in_e, w_pool, s_pool, g_qlat, w_qb, g_kvlat, w_kvb, g_q, g_k, w_out_e, w_in_o, w_conv_ssd, b_conv_ssd, dt_bias, a_log, d_skip, g_ssd_norm, w_conv_lru, b_conv_lru, w_lru_r, b_lru_r, w_lru_i, b_lru_i, lru_lambda, w_out_o):
    nb, t, _ = x_prompt.shape
    ns = x_sample.shape[0]
    assert x_sample.shape[1] == 1 and t >= POOL_BUF
    pos0_s = page_table.shape[1] * PAGE_SIZE

    pad_rows = (-(ns + nb)) % 8
    c_all = jnp.concatenate([c_sample, c_prompt, jnp.zeros((pad_rows, D_MODEL), F32)], axis=0)
    mod_all = _modulation(c_all, w_mod, b_mod)

    rows_p = _Rows(nb, t, 512)
    rows_s = _Rows(ns, 1, ns)
    xp = x_prompt.reshape(nb * t, D_MODEL)
    xs = x_sample.reshape(ns, D_MODEL)
    w2 = w_mlp2.astype(BF16)

    outs_p, outs_s = {}, {}
    for layer in range(DEPTH):
        mod_p = rows_p.mod_array(mod_all[layer, ns:ns + nb])
        mod_s = rows_s.mod_array(mod_all[layer, :ns])
        hn_p = _norm_mod(rows_p, xp, g_norm1[layer], mod_p, 1, 0)
        hn_s = _norm_mod(rows_s, xs, g_norm1[layer], mod_s, 1, 0)
        if layer % 2 == 0:
            e = layer // 2
            prm = _even_params(e, w_in_e, w_pool, s_pool, g_qlat, w_qb, g_kvlat, w_kvb, g_q, g_k, w_out_e)
            a_p, w_o, mla_p, pool_p = _even_layer_prompt(rows_p, hn_p, prm, nb, t)
            a_s, _, mla_s, pool_s = _even_layer_sample(rows_s, hn_s, prm, state_pool[e], cache_mla, e,
                                                       page_table, pos0_s)
            outs_p.setdefault("mla", []).append(mla_p)
            outs_p.setdefault("pool", []).append(pool_p)
            outs_s.setdefault("mla", []).append(mla_s)
            outs_s.setdefault("pool", []).append(pool_s)
        else:
            o = layer // 2
            prm = _odd_params(o, w_in_o, w_conv_ssd, b_conv_ssd, dt_bias, a_log, d_skip, g_ssd_norm, w_conv_lru,
                              b_conv_lru, w_lru_r, b_lru_r, w_lru_i, b_lru_i, lru_lambda, w_out_o)
            a_p, w_o, sconv_p, ssd_p, lconv_p, lru_p = _odd_layer_prompt(rows_p, hn_p, prm, nb, t)
            a_s, _, sconv_s, ssd_s, lconv_s, lru_s = _odd_layer_sample(
                rows_s, hn_s, prm, state_ssd_conv[o], state_ssd[o], state_lru_conv[o], state_lru[o], pos0_s)
            for d, vals in ((outs_p, (sconv_p, ssd_p, lconv_p, lru_p)), (outs_s, (sconv_s, ssd_s, lconv_s, lru_s))):
                for name, val in zip(("sconv", "ssd", "lconv", "lru"), vals):
                    d.setdefault(name, []).append(val)
        xp = _matmul(rows_p, a_p, w_o, name="out_proj", tn=1024, res=xp, mod=mod_p, gate_chunk=2)
        xs = _matmul(rows_s, a_s, w_o, name="out_proj_s", tn=1024, res=xs, mod=mod_s, gate_chunk=2)
        hn2_p = _norm_mod(rows_p, xp, g_norm2[layer], mod_p, 4, 3)
        hn2_s = _norm_mod(rows_s, xs, g_norm2[layer], mod_s, 4, 3)
        act_p = _matmul_ws(hn2_p, w_mlp1, layer, name="mlp1", tm=1024, tn=1024, act="relu2", out_dtype=BF16)
        act_s = _matmul(rows_s, [hn2_s], [(w_mlp1, layer)], name="mlp1_s", tn=1024, act="relu2", out_dtype=BF16)
        xp = _matmul(rows_p, [act_p], [(w2, layer)], name="mlp2", tn=512, res=xp, mod=mod_p, gate_chunk=5)
        xs = _matmul(rows_s, [act_s], [(w_mlp2, layer)], name="mlp2_s", tn=1024, tk=2048, res=xs, mod=mod_s,
                     gate_chunk=5)

    st = lambda d, name: jnp.stack(d[name])
    return (xp.reshape(nb, t, D_MODEL), xs.reshape(ns, 1, D_MODEL),
            st(outs_p, "mla"), st(outs_s, "mla"), st(outs_p, "pool"), st(outs_s, "pool"),
            st(outs_p, "sconv"), st(outs_s, "sconv"), st(outs_p, "ssd"), st(outs_s, "ssd"),
            st(outs_p, "lconv"), st(outs_s, "lconv"), st(outs_p, "lru"), st(outs_s, "lru"))
```

```python
import functools

import numpy as np
import jax
import jax.numpy as jnp
from jax import lax
from jax.experimental import pallas as pl
from jax.experimental.pallas import tpu as pltpu

F32 = jnp.float32
BF16 = jnp.bfloat16

VMEM_LIMIT_BYTES = 48 * 1024 * 1024
LANES = 128

D_MODEL = 2048
EPS = 1e-6
N_MOD = 6
DEPTH = 2

POOL_WIDTH = 1024
POOL_WINDOWS = (2, 4, 8, 16)
POOL_GROUP_DIM = 256
POOL_BUF = 15

MLA_HEADS = 8
MLA_NOPE = 128
MLA_ROPE = 64
MLA_V = 128
MLA_QK = 192
MLA_QKP = 256
MLA_Q_LORA = 512
MLA_KV_LORA = 256
ROPE_THETA = 10000.0
PAGE_SIZE = 128
EVEN_IN_PAD = 1920

SSD_D_INNER = 2048
SSD_HEAD_DIM = 64
SSD_HEADS = 32
SSD_GROUPS = 4
SSD_STATE = 128
SSD_CHUNK = 128
SSD_BC = 1024
CONV_WIDTH = 4

LRU_WIDTH = 1024
LRU_BLOCKS = 8
LRU_BLOCK_DIM = 128
LRU_C = 8.0

MLP_HIDDEN = 8192


def _cparams(*sem):
    return pltpu.CompilerParams(dimension_semantics=sem, vmem_limit_bytes=VMEM_LIMIT_BYTES)


def _sigmoid(x):
    return 1.0 / (1.0 + jnp.exp(-x))


def _silu(x):
    return x * _sigmoid(x)


def _softplus(x):
    return jnp.maximum(x, 0.0) + jnp.log1p(jnp.exp(-jnp.abs(x)))


def _gelu_tanh(x):
    return 0.5 * x * (1.0 + jnp.tanh(0.7978845608028654 * (x + 0.044715 * (x * x * x))))


def _dot(a, b):
    return jnp.dot(a, b, preferred_element_type=F32)


def _dot_nt(a, b):
    return lax.dot_general(a, b, (((1,), (1,)), ((), ())), preferred_element_type=F32)


def _split3(x):
    hi = x.astype(BF16)
    r = x - hi.astype(F32)
    mid = r.astype(BF16)
    lo = (r - mid.astype(F32)).astype(BF16)
    return hi, mid, lo


def _dot_sel(x, sel):
    hi, mid, lo = _split3(x)
    return _dot(hi, sel) + _dot(mid, sel) + _dot(lo, sel)


def _mod_kernel(c_ref, w_ref, b_ref, o_ref):
    c = c_ref[...]
    o_ref[...] = _dot(_silu(c).astype(BF16), w_ref[...].astype(BF16)) + b_ref[...]


def _modulation(c_all, w_mod, b_mod):
    mp = c_all.shape[0]
    n = N_MOD * D_MODEL
    tn = 1024
    return pl.pallas_call(
        _mod_kernel, name="modulation",
        grid=(DEPTH, n // tn),
        in_specs=[pl.BlockSpec((mp, D_MODEL), lambda l, j: (0, 0)),
                  pl.BlockSpec((None, D_MODEL, tn), lambda l, j: (l, 0, j)),
                  pl.BlockSpec((None, 1, tn), lambda l, j: (l, 0, j))],
        out_specs=pl.BlockSpec((None, mp, tn), lambda l, j: (l, 0, j)),
        out_shape=jax.ShapeDtypeStruct((DEPTH, mp, n), F32),
        compiler_params=_cparams("parallel", "parallel"),
    )(c_all, w_mod, b_mod.reshape(DEPTH, 1, n))


class _Rows:
    def __init__(self, nb, t, tm):
        self.nb, self.t, self.tm = nb, t, tm
        self.m = nb * t
        self.per_seq = t > 1
        if self.per_seq:
            assert t % tm == 0
            self.tiles_per_seq = t // tm

    def mod_array(self, mod_rows):
        return mod_rows.reshape(self.nb, 1, -1) if self.per_seq else mod_rows

    def mod_spec(self, chunk, width, ncol, col_of):
        per_chunk = D_MODEL // width
        if self.per_seq:
            tps = self.tiles_per_seq
            return pl.BlockSpec((None, 1, width),
                                lambda *g: (g[0] // tps, 0, chunk * per_chunk + col_of(*g)))
        return pl.BlockSpec((self.tm, width), lambda *g: (g[0], chunk * per_chunk + col_of(*g)))


def _norm_mod_kernel(x_ref, g_ref, sc_ref, sh_ref, o_ref):
    x = x_ref[...]
    y = x * lax.rsqrt(jnp.mean(x * x, axis=-1, keepdims=True) + EPS) * g_ref[...]
    o_ref[...] = (y * (1.0 + sc_ref[...]) + sh_ref[...]).astype(o_ref.dtype)


def _norm_mod(rows, x, g, mod, sc_chunk, sh_chunk):
    tm = min(rows.tm, 512)
    r = _Rows(rows.nb, rows.t, tm)
    return pl.pallas_call(
        _norm_mod_kernel, name="norm_mod",
        grid=(r.m // tm,),
        in_specs=[pl.BlockSpec((tm, D_MODEL), lambda i: (i, 0)),
                  pl.BlockSpec((1, D_MODEL), lambda i: (0, 0)),
                  r.mod_spec(sc_chunk, D_MODEL, 1, lambda i: 0),
                  r.mod_spec(sh_chunk, D_MODEL, 1, lambda i: 0)],
        out_specs=pl.BlockSpec((tm, D_MODEL), lambda i: (i, 0)),
        out_shape=jax.ShapeDtypeStruct((r.m, D_MODEL), BF16),
        compiler_params=_cparams("parallel"),
    )(x, g.reshape(1, D_MODEL), mod, mod)


def _mm_kernel(*refs, n_a, nk, act, has_res, w_t):
    a_refs = refs[:n_a]
    w_refs = refs[n_a:2 * n_a]
    pos = 2 * n_a
    if has_res:
        x_ref, gt_ref = refs[pos], refs[pos + 1]
        pos += 2
    o_ref = refs[pos]
    acc_ref = refs[pos + 1] if nk > 1 else None

    dot = _dot_nt if w_t else _dot
    part = dot(a_refs[0][...], w_refs[0][...].astype(BF16))
    for a_ref, w_ref in zip(a_refs[1:], w_refs[1:]):
        part = part + dot(a_ref[...], w_ref[...].astype(BF16))

    def finish(acc):
        if act == "relu2":
            acc = jnp.square(jnp.maximum(acc, 0.0))
        if has_res:
            acc = x_ref[...] + gt_ref[...] * acc
        o_ref[...] = acc.astype(o_ref.dtype)

    if nk == 1:
        finish(part)
    else:
        k = pl.program_id(2)

        @pl.when(k == 0)
        def _():
            acc_ref[...] = part

        @pl.when(k > 0)
        def _():
            acc_ref[...] += part

        @pl.when(k == nk - 1)
        def _():
            finish(acc_ref[...])


def _matmul(rows, a_list, w_list, *, name, tn, tk=None, act=None, res=None, mod=None, gate_chunk=None,
            out_dtype=F32, alias_res=False, tm=None, w_t=False):
    assert tm is None or res is None
    tm, m = (tm or rows.tm), rows.m
    n_a = len(a_list)
    if w_t:
        assert n_a == 1 and tk is None and not isinstance(w_list[0], tuple)
        n, kfull = w_list[0].shape
        return pl.pallas_call(
            functools.partial(_mm_kernel, n_a=1, nk=1, act=act, has_res=False, w_t=True), name=name,
            grid=(m // tm, n // tn, 1),
            in_specs=[pl.BlockSpec((tm, kfull), lambda i, j, k: (i, 0)),
                      pl.BlockSpec((tn, kfull), lambda i, j, k: (j, 0))],
            out_specs=pl.BlockSpec((tm, tn), lambda i, j, k: (i, j)),
            out_shape=jax.ShapeDtypeStruct((m, n), out_dtype),
            compiler_params=_cparams("parallel", "parallel", "arbitrary"),
        )(a_list[0], w_list[0])
    n = (w_list[0][0] if isinstance(w_list[0], tuple) else w_list[0]).shape[-1]
    if tk is None or n_a > 1:
        nk = 1
    else:
        assert a_list[0].shape[1] % tk == 0
        nk = a_list[0].shape[1] // tk
    assert n % tn == 0 and m % tm == 0
    in_specs, args = [], []
    for a in a_list:
        kk = a.shape[1] if nk == 1 else tk
        in_specs.append(pl.BlockSpec((tm, kk), lambda i, j, k: (i, k)))
        args.append(a)
    for w in w_list:
        if isinstance(w, tuple):
            w, layer = w
            kk = w.shape[1] if nk == 1 else tk
            in_specs.append(pl.BlockSpec((None, kk, tn), lambda i, j, k, layer=layer: (layer, k, j)))
        else:
            kk = w.shape[0] if nk == 1 else tk
            in_specs.append(pl.BlockSpec((kk, tn), lambda i, j, k: (k, j)))
        args.append(w)
    has_res = res is not None
    if has_res:
        in_specs.append(pl.BlockSpec((tm, tn), lambda i, j, k: (i, j)))
        in_specs.append(rows.mod_spec(gate_chunk, tn, n // tn, lambda i, j, k: j))
        args += [res, mod]
    scratch = [pltpu.VMEM((tm, tn), F32)] if nk > 1 else []
    aliases = {2 * n_a: 0} if (has_res and alias_res) else {}
    return pl.pallas_call(
        functools.partial(_mm_kernel, n_a=n_a, nk=nk, act=act, has_res=has_res, w_t=False), name=name,
        input_output_aliases=aliases,
        grid=(m // tm, n // tn, nk),
        in_specs=in_specs,
        out_specs=pl.BlockSpec((tm, tn), lambda i, j, k: (i, j)),
        out_shape=jax.ShapeDtypeStruct((m, n), out_dtype),
        scratch_shapes=scratch,
        compiler_params=_cparams("parallel", "parallel", "arbitrary"),
    )(*args)


def _mm_ws_kernel(a_ref, w_ref, o_ref, wb_ref, *, act):
    @pl.when(pl.program_id(1) == 0)
    def _():
        wb_ref[...] = w_ref[...].astype(BF16)

    acc = _dot(a_ref[...], wb_ref[...])
    if act == "relu2":
        acc = jnp.square(jnp.maximum(acc, 0.0))
    o_ref[...] = acc.astype(o_ref.dtype)


def _matmul_ws(a, w_stack, layer, *, name, tm, tn, act=None, out_dtype=F32):
    m, k = a.shape
    n = w_stack.shape[-1]
    assert m % tm == 0 and n % tn == 0
    return pl.pallas_call(
        functools.partial(_mm_ws_kernel, act=act), name=name,
        grid=(n // tn, m // tm),
        in_specs=[pl.BlockSpec((tm, k), lambda j, i: (i, 0)),
                  pl.BlockSpec((None, k, tn), lambda j, i: (layer, 0, j))],
        out_specs=pl.BlockSpec((tm, tn), lambda j, i: (i, j)),
        out_shape=jax.ShapeDtypeStruct((m, n), out_dtype),
        scratch_shapes=[pltpu.VMEM((k, tn), BF16)],
        compiler_params=_cparams("parallel", "arbitrary"),
    )(a, w_stack)


def _bmm_kernel(a_ref, w_ref, o_ref):
    o_ref[...] = _dot(a_ref[...], w_ref[...]).astype(o_ref.dtype)


def _bmm_heads(a, w, out_dtype):
    h, m, k = a.shape
    n = w.shape[2]
    return pl.pallas_call(
        _bmm_kernel, name="bmm_heads",
        grid=(h,),
        in_specs=[pl.BlockSpec((None, m, k), lambda i: (i, 0, 0)),
                  pl.BlockSpec((None, k, n), lambda i: (i, 0, 0))],
        out_specs=pl.BlockSpec((None, m, n), lambda i: (i, 0, 0)),
        out_shape=jax.ShapeDtypeStruct((h, m, n), out_dtype),
        compiler_params=_cparams("parallel"),
    )(a, w)


CONV_HALO = 8


def _conv_chunk(ext_ref, u, buf_ref, w_ref, b_ref, first):
    tc = u.shape[0]
    k1 = CONV_WIDTH - 1

    @pl.when(first)
    def _():
        ext_ref[CONV_HALO - k1:CONV_HALO, :] = buf_ref[...]

    ext_ref[CONV_HALO:CONV_HALO + tc, :] = u
    y = b_ref[...] + w_ref[k1:k1 + 1, :] * u
    for k in range(k1):
        y = y + w_ref[k:k + 1, :] * ext_ref[CONV_HALO - k1 + k:CONV_HALO - k1 + k + tc, :]
    ext_ref[CONV_HALO - k1:CONV_HALO, :] = ext_ref[CONV_HALO + tc - k1:CONV_HALO + tc, :]
    return y


def _conv_step(buf_ref, u, w_ref, b_ref):
    y = b_ref[...] + w_ref[CONV_WIDTH - 1:CONV_WIDTH, :] * u
    for k in range(CONV_WIDTH - 1):
        y = y + w_ref[k:k + 1, :] * buf_ref[k]
    return y


def _pool_prompt_kernel(u_ref, buf_ref, w_ref, s_ref, o_ref, ext_ref, *, tc, pos0):
    c = pl.program_id(1)
    halo = 16

    @pl.when(c == 0)
    def _():
        ext_ref[0:1, :] = jnp.zeros((1, POOL_WIDTH), F32)
        ext_ref[1:halo, :] = buf_ref[...]

    u = u_ref[...]
    ext_ref[halo:halo + tc, :] = u
    pos = (pos0 + c * tc + lax.broadcasted_iota(jnp.int32, (tc, 1), 0)).astype(F32)
    outs = []
    for g, w in enumerate(POOL_WINDOWS):
        lo, hi = g * POOL_GROUP_DIM, (g + 1) * POOL_GROUP_DIM
        x = u[:, lo:hi]
        wsum = x
        for j in range(1, w):
            wsum = wsum + ext_ref[halo - j:halo - j + tc, lo:hi]
        cnt = jnp.minimum(pos + 1.0, float(w))
        d = (wsum / cnt - x).astype(BF16)
        outs.append(_dot(d, w_ref[g]))
    y = jnp.concatenate(outs, axis=1) * s_ref[...]
    o_ref[...] = y.astype(o_ref.dtype)
    ext_ref[0:halo, :] = ext_ref[tc:tc + halo, :]


def _pool_prompt(u, buf, w_pool, s_pool, nb, t, pos0):
    tc = 256
    nc = t // tc
    return pl.pallas_call(
        functools.partial(_pool_prompt_kernel, tc=tc, pos0=pos0), name="pool_prompt",
        grid=(nb, nc),
        in_specs=[pl.BlockSpec((tc, POOL_WIDTH), lambda b, c: (b * nc + c, 0)),
                  pl.BlockSpec((None, POOL_BUF, POOL_WIDTH), lambda b, c: (b, 0, 0)),
                  pl.BlockSpec((len(POOL_WINDOWS), POOL_GROUP_DIM, POOL_GROUP_DIM), lambda b, c: (0, 0, 0)),
                  pl.BlockSpec((1, POOL_WIDTH), lambda b, c: (0, 0))],
        out_specs=pl.BlockSpec((tc, POOL_WIDTH), lambda b, c: (b * nc + c, 0)),
        out_shape=jax.ShapeDtypeStruct((nb * t, POOL_WIDTH), BF16),
        scratch_shapes=[pltpu.VMEM((16 + tc, POOL_WIDTH), F32)],
        compiler_params=_cparams("parallel", "arbitrary"),
    )(u, buf, w_pool, s_pool)


def _pool_step_kernel(u_ref, buf_ref, w_ref, s_ref, o_ref, *, pos0):
    outs = []
    for g, w in enumerate(POOL_WINDOWS):
        lo, hi = g * POOL_GROUP_DIM, (g + 1) * POOL_GROUP_DIM
        x = u_ref[:, lo:hi]
        wsum = x
        for j in range(1, w):
            wsum = wsum + buf_ref[POOL_BUF - j, :, lo:hi]
        cnt = min(pos0 + 1.0, float(w))
        d = (wsum / cnt - x).astype(BF16)
        outs.append(_dot(d, w_ref[g]))
    o_ref[...] = (jnp.concatenate(outs, axis=1) * s_ref[...]).astype(o_ref.dtype)


def _pool_step(u, buf_t, w_pool, s_pool, pos0):
    m = buf_t.shape[1]
    return pl.pallas_call(
        functools.partial(_pool_step_kernel, pos0=pos0), name="pool_step",
        grid=(1,),
        in_specs=[pl.BlockSpec((m, POOL_WIDTH), lambda i: (0, 0)),
                  pl.BlockSpec((POOL_BUF, m, POOL_WIDTH), lambda i: (0, 0, 0)),
                  pl.BlockSpec((len(POOL_WINDOWS), POOL_GROUP_DIM, POOL_GROUP_DIM), lambda i: (0, 0, 0)),
                  pl.BlockSpec((1, POOL_WIDTH), lambda i: (0, 0))],
        out_specs=pl.BlockSpec((m, POOL_WIDTH), lambda i: (0, 0)),
        out_shape=jax.ShapeDtypeStruct((m, POOL_WIDTH), BF16),
        compiler_params=_cparams("arbitrary"),
    )(u, buf_t, w_pool, s_pool)


def _rope128(x, cc, ss):
    lane = lax.broadcasted_iota(jnp.int32, x.shape, 1)
    swapped = jnp.where(lane < MLA_ROPE // 2, pltpu.roll(x, LANES - MLA_ROPE // 2, 1),
                        pltpu.roll(x, MLA_ROPE // 2, 1))
    return x * cc + swapped * ss


def _mla_prep_kernel(uq_ref, ukv_ref, ukr_ref, gql_ref, gkvl_ref, wq_ref, wkv_ref, gq_ref, gk_ref,
                     cc_ref, ss_ref, *out_refs, emit_qg):
    if emit_qg:
        q_ref, k_ref, v_ref, lat_ref, qg_ref = out_refs
    else:
        q_ref, k_ref, v_ref, lat_ref = out_refs
    cc = cc_ref[...]
    ss = ss_ref[...]
    scale = MLA_QK ** -0.5

    uq = uq_ref[...]
    qn = uq * lax.rsqrt(jnp.mean(uq * uq, axis=-1, keepdims=True) + EPS) * gql_ref[...]
    qall = _dot(qn.astype(BF16), wq_ref[...])
    ukv = ukv_ref[...]
    lat = ukv * lax.rsqrt(jnp.mean(ukv * ukv, axis=-1, keepdims=True) + EPS) * gkvl_ref[...]
    lat_ref[...] = lat
    kvall = _dot(lat.astype(BF16), wkv_ref[...])
    kr = ukr_ref[...]
    kr_ss = jnp.sum(kr * kr, axis=-1, keepdims=True)

    gq_n, gq_r = gq_ref[:, :MLA_NOPE], gq_ref[:, MLA_NOPE:]
    gk_n, gk_r = gk_ref[:, :MLA_NOPE], gk_ref[:, MLA_NOPE:]
    nh = MLA_HEADS
    for h in range(nh):
        qnope = qall[:, h * 128:(h + 1) * 128]
        qrope = qall[:, (nh + h) * 128:(nh + h + 1) * 128]
        ssq = jnp.sum(qnope * qnope, axis=-1, keepdims=True) + jnp.sum(qrope * qrope, axis=-1, keepdims=True)
        rs = lax.rsqrt(ssq * (1.0 / MLA_QK) + EPS) * scale
        qn_h = qnope * rs * gq_n
        qr_h = _rope128(qrope * rs * gq_r, cc, ss)
        q_ref[h] = jnp.concatenate([qn_h, qr_h], axis=1).astype(q_ref.dtype)
        if emit_qg:
            qg_ref[h] = (qn_h * gk_n).astype(qg_ref.dtype)

        knope = kvall[:, h * 256:h * 256 + 128]
        ssk = jnp.sum(knope * knope, axis=-1, keepdims=True) + kr_ss
        rsk = lax.rsqrt(ssk * (1.0 / MLA_QK) + EPS)
        kr_h = _rope128(kr * rsk * gk_r, cc, ss)
        k_ref[h] = jnp.concatenate([knope * rsk * gk_n, kr_h], axis=1).astype(k_ref.dtype)
        v_ref[h] = kvall[:, h * 256 + 128:(h + 1) * 256].astype(v_ref.dtype)


def _mla_prep(u, m, tm, pos_of_tile, tabs, prm, emit_qg):
    cc_tab, ss_tab = tabs
    nh = MLA_HEADS
    out_shape = [jax.ShapeDtypeStruct((nh, m, MLA_QKP), BF16),
                 jax.ShapeDtypeStruct((nh, m, MLA_QKP), BF16),
                 jax.ShapeDtypeStruct((nh, m, MLA_V), BF16),
                 jax.ShapeDtypeStruct((m, MLA_KV_LORA), F32)]
    out_specs = [pl.BlockSpec((nh, tm, MLA_QKP), lambda i: (0, i, 0)),
                 pl.BlockSpec((nh, tm, MLA_QKP), lambda i: (0, i, 0)),
                 pl.BlockSpec((nh, tm, MLA_V), lambda i: (0, i, 0)),
                 pl.BlockSpec((tm, MLA_KV_LORA), lambda i: (i, 0))]
    if emit_qg:
        out_shape.append(jax.ShapeDtypeStruct((nh, m, MLA_NOPE), BF16))
        out_specs.append(pl.BlockSpec((nh, tm, MLA_NOPE), lambda i: (0, i, 0)))
    full = lambda a: pl.BlockSpec(a.shape, lambda i: (0,) * a.ndim)
    return pl.pallas_call(
        functools.partial(_mla_prep_kernel, emit_qg=emit_qg), name="mla_prep",
        grid=(m // tm,),
        in_specs=[pl.BlockSpec((tm, MLA_Q_LORA), lambda i: (i, POOL_WIDTH // MLA_Q_LORA)),
                  pl.BlockSpec((tm, MLA_KV_LORA), lambda i: (i, (POOL_WIDTH + MLA_Q_LORA) // MLA_KV_LORA)),
                  pl.BlockSpec((tm, LANES), lambda i: (i, (POOL_WIDTH + MLA_Q_LORA + MLA_KV_LORA) // LANES)),
                  full(prm["g_qlat"]), full(prm["g_kvlat"]), full(prm["w_qb"]), full(prm["w_kvb"]),
                  full(prm["g_q"]), full(prm["g_k"]),
                  pl.BlockSpec((tm, LANES), lambda i: (pos_of_tile(i), 0)),
                  pl.BlockSpec((tm, LANES), lambda i: (pos_of_tile(i), 0))],
        out_specs=out_specs,
        out_shape=out_shape,
        compiler_params=_cparams("parallel"),
    )(u, u, u, prm["g_qlat"], prm["g_kvlat"], prm["w_qb"], prm["w_kvb"], prm["g_q"], prm["g_k"],
      cc_tab, ss_tab)


def _flash_kernel(q_ref, k_ref, v_ref, o_ref, *, tq, tk):
    i = pl.program_id(2)
    q = q_ref[...]
    per_q = tq // tk

    def step(j, carry, band):
        m, l, acc = carry
        start = pl.multiple_of(j * tk, tk)
        kb = k_ref[pl.ds(start, tk), :]
        vb = v_ref[pl.ds(start, tk), :]
        s = _dot_nt(q, kb)
        if band is not None:
            r = lax.broadcasted_iota(jnp.int32, (tq, tk), 0)
            c = lax.broadcasted_iota(jnp.int32, (tq, tk), 1) + band * tk
            s = jnp.where(c <= r, s, -jnp.inf)
        m_new = jnp.maximum(m, jnp.max(s, axis=1, keepdims=True))
        alpha = jnp.exp(m - m_new)
        p = jnp.exp(s - m_new)
        l = alpha * l + jnp.sum(p, axis=1, keepdims=True)
        acc = alpha * acc + _dot(p.astype(BF16), vb)
        return m_new, l, acc

    carry = (jnp.full((tq, 1), -jnp.inf, F32), jnp.zeros((tq, 1), F32), jnp.zeros((tq, MLA_V), F32))
    carry = lax.fori_loop(0, i * per_q, lambda j, c: step(j, c, None), carry)
    for band in range(per_q):
        carry = step(i * per_q + band, carry, band)
    _, l, acc = carry
    o_ref[...] = (acc / l).astype(o_ref.dtype)


FLASH_TQ = 1024
FLASH_TK = 1024


def _flash_attention(q, k, v, nb, t):
    tq, tk = FLASH_TQ, FLASH_TK
    nq = t // tq
    nh = MLA_HEADS
    return pl.pallas_call(
        functools.partial(_flash_kernel, tq=tq, tk=tk), name="flash_prompt",
        grid=(nb, nh, nq),
        in_specs=[pl.BlockSpec((None, tq, MLA_QKP), lambda b, h, i: (h, b * nq + i, 0)),
                  pl.BlockSpec((None, t, MLA_QKP), lambda b, h, i: (h, b, 0)),
                  pl.BlockSpec((None, t, MLA_V), lambda b, h, i: (h, b, 0))],
        out_specs=pl.BlockSpec((tq, MLA_V), lambda b, h, i: (b * nq + i, h)),
        out_shape=jax.ShapeDtypeStruct((nb * t, nh * MLA_V), BF16),
        compiler_params=_cparams("parallel", "parallel", "arbitrary"),
    )(q, k, v)


DEC_CHUNK_PAGES = 8
DEC_SUB_PAGES = 8
DEC_QROWS = 16


def _page_copy(pt_ref, cache_ref, buf_ref, sem_ref, layer, b, i, slot):
    return pltpu.make_async_copy(cache_ref.at[layer, pt_ref[b, i]], buf_ref.at[slot, i], sem_ref.at[slot])


def _decode_kernel(pt_ref, cache_ref, wt_ref, qp_ref, qr_ref, gr_ref, cc_ref, ss_ref, qf_ref, kn_ref, latn_ref,
                   o_ref, buf_ref, sem_ref, lhs_ref, *latb_refs, n_pages, nsamp, layer):
    b = pl.program_id(0)
    nh = MLA_HEADS
    slot = lax.rem(b, 2)
    nw = nh * MLA_NOPE

    def fetch(bb, sl):
        for i in range(n_pages):
            _page_copy(pt_ref, cache_ref, buf_ref, sem_ref, layer, bb, i, sl).start()

    @pl.when(b == 0)
    def _():
        fetch(b, slot)
        lhs_ref[0:nw, :] = wt_ref[...]

    @pl.when(b + 1 < nsamp)
    def _():
        fetch(b + 1, 1 - slot)

    for i in range(n_pages):
        _page_copy(pt_ref, cache_ref, buf_ref, sem_ref, layer, b, i, slot).wait()
    pages = [buf_ref.at[slot, i] for i in range(n_pages)]

    lhs_ref[nw:nw + DEC_QROWS, :] = qp_ref[...]
    lhs = lhs_ref[...]
    gr = gr_ref[...]
    half = MLA_ROPE // 2
    sub = DEC_SUB_PAGES * PAGE_SIZE
    chunk = DEC_CHUNK_PAGES * PAGE_SIZE
    qr = qr_ref[...]

    def tile_scores(c, t):
        i0 = c * DEC_CHUNK_PAGES + t * DEC_SUB_PAGES
        k0 = i0 * PAGE_SIZE
        pg = pages[i0:i0 + DEC_SUB_PAGES]
        latb = jnp.concatenate([p[0:MLA_KV_LORA, :] for p in pg], axis=1).astype(BF16)
        kr = jnp.concatenate([p[MLA_KV_LORA:MLA_KV_LORA + MLA_ROPE, :] for p in pg], axis=1)
        latb_refs[c][:, t * sub:(t + 1) * sub] = latb
        kt = _dot(lhs, latb)
        ssn = jnp.concatenate(
            [jnp.sum(jnp.square(kt[h * 128:(h + 1) * 128, :]), axis=0, keepdims=True) for h in range(nh)],
            axis=0)
        sn = kt[nw:nw + nh, :]
        krg = kr * gr
        swapped = jnp.concatenate([krg[half:, :], krg[:half, :]], axis=0)
        rot = krg * cc_ref[:, k0:k0 + sub] + swapped * ss_ref[:, k0:k0 + sub]
        feat = jnp.concatenate([rot, kr * kr], axis=0).astype(BF16)
        rr = _dot(qr, feat)
        rs = lax.rsqrt((ssn + rr[nh:nh + 1]) * (1.0 / MLA_QK) + EPS)
        return rs * (sn + rr[:nh])

    def chunk_update(c, s, state):
        m_run, l_run, acc = state
        m_new = jnp.maximum(m_run, jnp.max(s, axis=1, keepdims=True))
        alpha = jnp.exp(m_run - m_new)
        p = jnp.exp(s - m_new)
        l_run = alpha * l_run + jnp.sum(p, axis=1, keepdims=True)
        acc = alpha * acc + _dot_nt(p.astype(BF16), latb_refs[c][...])
        return m_new, l_run, acc

    n_chunks = n_pages // DEC_CHUNK_PAGES
    n_tiles = DEC_CHUNK_PAGES // DEC_SUB_PAGES
    state = (jnp.full((nh, 1), -jnp.inf, F32), jnp.zeros((nh, 1), F32), jnp.zeros((nh, MLA_KV_LORA), F32))
    s_prev = None
    for c in range(n_chunks):
        s_cur = jnp.concatenate([tile_scores(c, t) for t in range(n_tiles)], axis=1)
        if s_prev is not None:
            state = chunk_update(c - 1, s_prev, state)
        s_prev = s_cur
    m_run, l_run, acc = chunk_update(n_chunks - 1, s_prev, state)

    s_new = jnp.sum(qf_ref[...] * kn_ref[...], axis=1, keepdims=True)
    m_fin = jnp.maximum(m_run, s_new)
    a_fin = jnp.exp(m_run - m_fin)
    p_new = jnp.exp(s_new - m_fin)
    l_fin = a_fin * l_run + p_new
    lat_new = latn_ref[...].astype(BF16).astype(F32)
    o_ref[...] = (a_fin * acc + p_new.astype(BF16).astype(F32) * lat_new) / l_fin


def _mla_decode(cache, layer, page_table, wt_nope, qp, qr, g_rope, cc_keys, ss_keys, qf, kn, lat_new):
    nsamp, n_pages = page_table.shape
    assert n_pages % DEC_CHUNK_PAGES == 0 and DEC_CHUNK_PAGES % DEC_SUB_PAGES == 0
    n_keys = n_pages * PAGE_SIZE
    kvd = cache.shape[-1]
    cache = jnp.transpose(cache, (0, 1, 3, 2))
    nh = MLA_HEADS
    grid_spec = pltpu.PrefetchScalarGridSpec(
        num_scalar_prefetch=1,
        grid=(nsamp,),
        in_specs=[
            pl.BlockSpec(memory_space=pl.ANY),
            pl.BlockSpec(wt_nope.shape, lambda b, pt: (0, 0)),
            pl.BlockSpec((None, DEC_QROWS, MLA_KV_LORA), lambda b, pt: (b, 0, 0)),
            pl.BlockSpec((None, DEC_QROWS, LANES), lambda b, pt: (b, 0, 0)),
            pl.BlockSpec((MLA_ROPE, 1), lambda b, pt: (0, 0)),
            pl.BlockSpec((MLA_ROPE, n_keys), lambda b, pt: (0, 0)),
            pl.BlockSpec((MLA_ROPE, n_keys), lambda b, pt: (0, 0)),
            pl.BlockSpec((None, nh, MLA_QKP), lambda b, pt: (b, 0, 0)),
            pl.BlockSpec((None, nh, MLA_QKP), lambda b, pt: (b, 0, 0)),
            pl.BlockSpec((None, 1, MLA_KV_LORA), lambda b, pt: (b, 0, 0)),
        ],
        out_specs=pl.BlockSpec((None, nh, MLA_KV_LORA), lambda b, pt: (b, 0, 0)),
        scratch_shapes=[pltpu.VMEM((2, n_pages, kvd, PAGE_SIZE), F32), pltpu.SemaphoreType.DMA((2,)),
                        pltpu.VMEM((nh * MLA_NOPE + DEC_QROWS, MLA_KV_LORA), BF16),
                        ] + [pltpu.VMEM((MLA_KV_LORA, DEC_CHUNK_PAGES * PAGE_SIZE), BF16)
                             for _ in range(n_pages // DEC_CHUNK_PAGES)],
    )
    return pl.pallas_call(
        functools.partial(_decode_kernel, n_pages=n_pages, nsamp=nsamp, layer=layer), name="mla_decode",
        grid_spec=grid_spec,
        out_shape=jax.ShapeDtypeStruct((nsamp, nh, MLA_KV_LORA), F32),
        compiler_params=_cparams("arbitrary"),
    )(page_table, cache, wt_nope, qp, qr, g_rope, cc_keys, ss_keys, qf, kn, lat_new)


def _ssd_gate_norm(y, xs, z, dskip, gnorm):
    y = (y + dskip * xs) * _silu(z)
    gw = SSD_D_INNER // SSD_GROUPS
    outs = []
    for g in range(SSD_GROUPS):
        yg = y[:, g * gw:(g + 1) * gw]
        outs.append(yg * lax.rsqrt(jnp.mean(yg * yg, axis=-1, keepdims=True) + EPS))
    return jnp.concatenate(outs, axis=1) * gnorm


def _ssd_prompt_kernel(z_ref, x_ref, bc_ref, dt_ref, bufx_ref, bufbc_ref, wx_ref, bx_ref, wbc_ref, bbc_ref,
                       dtb_ref, alog_ref, dskip_ref, gnorm_ref, h0_ref, tri_ref, exp_ref,
                       y_ref, hout_ref, extx_ref, extbc_ref, ht_ref, *, nc):
    c = pl.program_id(1)
    first = c == 0
    l = SSD_CHUNK
    xs = _silu(_conv_chunk(extx_ref, x_ref[...], bufx_ref, wx_ref, bx_ref, first))
    bcs = _silu(_conv_chunk(extbc_ref, bc_ref[...], bufbc_ref, wbc_ref, bbc_ref, first))
    gs = SSD_GROUPS * SSD_STATE
    bm, cm = bcs[:, :gs], bcs[:, gs:]

    @pl.when(first)
    def _():
        for r in range(SSD_D_INNER // LANES):
            ht_ref[:, r * LANES:(r + 1) * LANES] = h0_ref[r * LANES:(r + 1) * LANES, :].T

    dt = _softplus(dt_ref[...] + dtb_ref[...])
    a = -jnp.exp(alog_ref[...])
    acum = _dot_sel_left(tri_ref[...], dt * a)
    acum_t = acum.T
    dt_t = dt.T
    a_last = acum[l - 1:l, :]
    dend = jnp.exp(a_last - acum)
    sel = exp_ref[...]
    w_state = _dot_sel(dt * dend, sel)
    e_acc = _dot_sel(jnp.exp(acum), sel)
    e_last = _dot_sel(jnp.exp(a_last), sel)

    rows = lax.broadcasted_iota(jnp.int32, (l, l), 0)
    cols = lax.broadcasted_iota(jnp.int32, (l, l), 1)
    causal = cols <= rows
    lane = lax.broadcasted_iota(jnp.int32, (l, LANES), 1)
    xs_b = xs.astype(BF16)
    hpg = SSD_HEADS // SSD_GROUPS
    gw = SSD_D_INNER // SSD_GROUPS
    y_groups = []
    for g in range(SSD_GROUPS):
        gcols = slice(g * gw, (g + 1) * gw)
        bg = bm[:, g * SSD_STATE:(g + 1) * SSD_STATE]
        cg_b = cm[:, g * SSD_STATE:(g + 1) * SSD_STATE].astype(BF16)
        cb = _dot_nt(cg_b, bg.astype(BF16))
        ht_g = ht_ref[:, gcols]
        y_off = _dot(cg_b, ht_g.astype(BF16))
        y_pairs = []
        for pr in range(hpg // 2):
            col = (g * hpg + pr * 2) * SSD_HEAD_DIM
            res = []
            for hh in range(2):
                h = g * hpg + pr * 2 + hh
                diff = acum[:, h:h + 1] - acum_t[h:h + 1, :]
                lmat = jnp.exp(jnp.where(causal, diff, -jnp.inf)) * dt_t[h:h + 1, :]
                res.append(_dot((cb * lmat).astype(BF16), xs_b[:, col:col + LANES]))
            y_pairs.append(jnp.where(lane < SSD_HEAD_DIM, res[0], res[1]))
        y_groups.append(jnp.concatenate(y_pairs, axis=1) + e_acc[:, gcols] * y_off)
        xw = (xs[:, gcols] * w_state[:, gcols]).astype(BF16)
        ht_ref[:, gcols] = e_last[:, gcols] * ht_g + _dot(bg.T.astype(BF16), xw)
    y = jnp.concatenate(y_groups, axis=1)
    y_ref[...] = _ssd_gate_norm(y, xs, z_ref[...], dskip_ref[...], gnorm_ref[...]).astype(y_ref.dtype)

    @pl.when(c == nc - 1)
    def _():
        for r in range(SSD_D_INNER // LANES):
            hout_ref[r * LANES:(r + 1) * LANES, :] = ht_ref[:, r * LANES:(r + 1) * LANES].T


def _dot_sel_left(sel, x):
    hi, mid, lo = _split3(x)
    return _dot(sel, hi) + _dot(sel, mid) + _dot(sel, lo)


def _ssd_consts():
    l = SSD_CHUNK
    tri = (jnp.arange(l)[:, None] >= jnp.arange(l)[None, :]).astype(BF16)
    sel = (jnp.arange(LANES)[:, None] == (jnp.arange(SSD_D_INNER)[None, :] // SSD_HEAD_DIM)).astype(BF16)
    return tri, sel


def _ssd_prompt(u, u_dt, bufx, bufbc, h0, prm, nb, t):
    l = SSD_CHUNK
    nc = t // l
    tri, sel = _ssd_consts()
    full = lambda a: pl.BlockSpec(a.shape, lambda b, c: (0,) * a.ndim)
    row = lambda width, colblk: pl.BlockSpec((l, width), lambda b, c: (b * nc + c, colblk))
    y, hout = pl.pallas_call(
        functools.partial(_ssd_prompt_kernel, nc=nc), name="ssd_prompt",
        grid=(nb, nc),
        in_specs=[row(SSD_D_INNER, 0), row(SSD_D_INNER, 1), row(SSD_BC, 4), row(LANES, 0),
                  pl.BlockSpec((None, CONV_WIDTH - 1, SSD_D_INNER), lambda b, c: (b, 0, 0)),
                  pl.BlockSpec((None, CONV_WIDTH - 1, SSD_BC), lambda b, c: (b, 0, 0)),
                  full(prm["w_conv_x"]), full(prm["b_conv_x"]), full(prm["w_conv_bc"]), full(prm["b_conv_bc"]),
                  full(prm["dt_bias"]), full(prm["a_log"]), full(prm["d_skip"]), full(prm["g_ssd_norm"]),
                  pl.BlockSpec((None, SSD_D_INNER, SSD_STATE), lambda b, c: (b, 0, 0)),
                  full(tri), full(sel)],
        out_specs=[pl.BlockSpec((l, SSD_D_INNER), lambda b, c: (b * nc + c, 0)),
                   pl.BlockSpec((None, SSD_D_INNER, SSD_STATE), lambda b, c: (b, 0, 0))],
        out_shape=[jax.ShapeDtypeStruct((nb * t, SSD_D_INNER), BF16),
                   jax.ShapeDtypeStruct((nb, SSD_D_INNER, SSD_STATE), F32)],
        scratch_shapes=[pltpu.VMEM((CONV_HALO + l, SSD_D_INNER), F32),
                        pltpu.VMEM((CONV_HALO + l, SSD_BC), F32),
                        pltpu.VMEM((SSD_STATE, SSD_D_INNER), F32)],
        compiler_params=_cparams("parallel", "arbitrary"),
    )(u, u, u, u_dt, bufx, bufbc, prm["w_conv_x"], prm["b_conv_x"], prm["w_conv_bc"], prm["b_conv_bc"],
      prm["dt_bias"], prm["a_log"], prm["d_skip"], prm["g_ssd_norm"], h0, tri, sel)
    return y, hout


def _ssd_step_prep_kernel(x_ref, bc_ref, dt_ref, bufx_ref, bufbc_ref, wx_ref, bx_ref, wbc_ref, bbc_ref,
                          dtb_ref, alog_ref, exp_ref, xs_ref, bcs_ref, dtx_ref, dae_ref):
    xs = _silu(_conv_step(bufx_ref, x_ref[...], wx_ref, bx_ref))
    xs_ref[...] = xs
    bcs_ref[...] = _silu(_conv_step(bufbc_ref, bc_ref[...], wbc_ref, bbc_ref))
    dt = _softplus(dt_ref[...] + dtb_ref[...])
    da = jnp.exp(dt * (-jnp.exp(alog_ref[...])))
    sel = exp_ref[...]
    dtx_ref[...] = _dot_sel(dt, sel) * xs
    dae_ref[...] = _dot_sel(da, sel)


def _ssd_step_prep(u, u_dt, bufx_t, bufbc_t, prm):
    m = u_dt.shape[0]
    _, sel = _ssd_consts()
    full = lambda a: pl.BlockSpec(a.shape, lambda i: (0,) * a.ndim)
    return pl.pallas_call(
        _ssd_step_prep_kernel, name="ssd_step_prep",
        grid=(1,),
        in_specs=[pl.BlockSpec((m, SSD_D_INNER), lambda i: (0, 1)),
                  pl.BlockSpec((m, SSD_BC), lambda i: (0, 4)),
                  full(u_dt), full(bufx_t), full(bufbc_t),
                  full(prm["w_conv_x"]), full(prm["b_conv_x"]), full(prm["w_conv_bc"]), full(prm["b_conv_bc"]),
                  full(prm["dt_bias"]), full(prm["a_log"]), full(sel)],
        out_specs=[pl.BlockSpec((m, SSD_D_INNER), lambda i: (0, 0)),
                   pl.BlockSpec((m, SSD_BC), lambda i: (0, 0)),
                   pl.BlockSpec((m, SSD_D_INNER), lambda i: (0, 0)),
                   pl.BlockSpec((m, SSD_D_INNER), lambda i: (0, 0))],
        out_shape=[jax.ShapeDtypeStruct((m, SSD_D_INNER), F32),
                   jax.ShapeDtypeStruct((m, SSD_BC), F32),
                   jax.ShapeDtypeStruct((m, SSD_D_INNER), F32),
                   jax.ShapeDtypeStruct((m, SSD_D_INNER), F32)],
        compiler_params=_cparams("arbitrary"),
    )(u, u, u_dt, bufx_t, bufbc_t, prm["w_conv_x"], prm["b_conv_x"], prm["w_conv_bc"], prm["b_conv_bc"],
      prm["dt_bias"], prm["a_log"], sel)


SSD_STEP_SAMPLES = 2


def _ssd_step_state_kernel(h0_ref, cols_ref, b_ref, c_ref, hn_ref, y_ref):
    nr = SSD_D_INNER // LANES
    rows_per_group = SSD_D_INNER // SSD_GROUPS // LANES
    gw = SSD_D_INNER // SSD_GROUPS
    for s in range(SSD_STEP_SAMPLES):
        pad = jnp.concatenate([cols_ref[s], jnp.zeros((LANES - 2 * nr, LANES), F32)], axis=0)
        ct = pad.T
        for r in range(nr):
            g = r // rows_per_group
            hn_ref[s, r * LANES:(r + 1) * LANES, :] = (
                ct[:, nr + r:nr + r + 1] * h0_ref[s, r * LANES:(r + 1) * LANES, :]
                + ct[:, r:r + 1] * b_ref[s, g:g + 1, :])
        res = _dot_nt(c_ref[s].astype(BF16), hn_ref[s].astype(BF16))
        y_ref[s] = jnp.concatenate([res[g:g + 1, g * gw:(g + 1) * gw] for g in range(SSD_GROUPS)], axis=1)


def _ssd_step_state(h0, cols, bmat, cmat):
    m = h0.shape[0]
    sps = SSD_STEP_SAMPLES
    assert m % sps == 0
    per = lambda shape: pl.BlockSpec((sps,) + shape, lambda i: (i, 0, 0))
    return pl.pallas_call(
        _ssd_step_state_kernel, name="ssd_step_state",
        grid=(m // sps,),
        in_specs=[per((SSD_D_INNER, SSD_STATE)), per((2 * SSD_D_INNER // LANES, LANES)),
                  per((8, SSD_STATE)), per((8, SSD_STATE))],
        out_specs=[per((SSD_D_INNER, SSD_STATE)), per((1, SSD_D_INNER))],
        out_shape=[jax.ShapeDtypeStruct((m, SSD_D_INNER, SSD_STATE), F32),
                   jax.ShapeDtypeStruct((m, 1, SSD_D_INNER), F32)],
        compiler_params=_cparams("parallel"),
    )(h0, cols, bmat, cmat)


def _ssd_step_post_kernel(y_ref, xs_ref, z_ref, dskip_ref, gnorm_ref, o_ref):
    o_ref[...] = _ssd_gate_norm(y_ref[...], xs_ref[...], z_ref[...], dskip_ref[...],
                                gnorm_ref[...]).astype(o_ref.dtype)


def _ssd_step_post(y, xs, u, prm):
    m = y.shape[0]
    full = lambda a: pl.BlockSpec(a.shape, lambda i: (0,) * a.ndim)
    return pl.pallas_call(
        _ssd_step_post_kernel, name="ssd_step_post",
        grid=(1,),
        in_specs=[full(y), full(xs), pl.BlockSpec((m, SSD_D_INNER), lambda i: (0, 0)),
                  full(prm["d_skip"]), full(prm["g_ssd_norm"])],
        out_specs=pl.BlockSpec((m, SSD_D_INNER), lambda i: (0, 0)),
        out_shape=jax.ShapeDtypeStruct((m, SSD_D_INNER), BF16),
        compiler_params=_cparams("arbitrary"),
    )(y, xs, u, prm["d_skip"], prm["g_ssd_norm"])


def _lru_gates(xc, wr_ref, br_ref, wi_ref, bi_ref, lam_ref, is_pos0):
    xb = xc.astype(BF16)
    rs, is_ = [], []
    for n in range(LRU_BLOCKS):
        blk = xb[:, n * LRU_BLOCK_DIM:(n + 1) * LRU_BLOCK_DIM]
        rs.append(_dot(blk, wr_ref[n]))
        is_.append(_dot(blk, wi_ref[n]))
    r = _sigmoid(jnp.concatenate(rs, axis=1) + br_ref[...])
    i = _sigmoid(jnp.concatenate(is_, axis=1) + bi_ref[...])
    log_a = -LRU_C * r * _softplus(-lam_ref[...])
    a = jnp.exp(log_a)
    th = jnp.tanh(log_a)
    mult = jnp.sqrt(-2.0 * th / (1.0 - th))
    if is_pos0 is not False:
        mult = jnp.where(is_pos0, 1.0, mult)
    return a, mult * i * xc


SUBLANES = 8


def _scan_rows(a, b, h_in):
    tc, w = a.shape
    ng = tc // SUBLANES
    a3 = a.reshape(ng, SUBLANES, w)
    b3 = b.reshape(ng, SUBLANES, w)
    sub = lax.broadcasted_iota(jnp.int32, a3.shape, 1)
    d = 1
    while d < SUBLANES:
        keep = sub >= d
        b3 = b3 + a3 * jnp.where(keep, pltpu.roll(b3, d, 1), 0.0)
        a3 = a3 * jnp.where(keep, pltpu.roll(a3, d, 1), 1.0)
        d *= 2
    carry = h_in
    outs = []
    for g in range(ng):
        hg = b3[g] + a3[g] * carry
        outs.append(hg)
        carry = hg[SUBLANES - 1:SUBLANES, :]
    return jnp.concatenate(outs, axis=0), carry


def _lru_prompt_kernel(lx_ref, lg_ref, buf_ref, wc_ref, bc_ref, wr_ref, br_ref, wi_ref, bi_ref, lam_ref, h0_ref,
                       y_ref, hout_ref, ext_ref, h_ref, *, tc, nc, pos0):
    c = pl.program_id(1)
    first = c == 0

    @pl.when(first)
    def _():
        h_ref[...] = h0_ref[...]

    xc = _conv_chunk(ext_ref, lx_ref[...], buf_ref, wc_ref, bc_ref, first)
    pos = pos0 + c * tc + lax.broadcasted_iota(jnp.int32, (tc, 1), 0)
    a, b = _lru_gates(xc, wr_ref, br_ref, wi_ref, bi_ref, lam_ref, pos == 0)
    h, h_last = _scan_rows(a, b, h_ref[...])
    h_ref[...] = h_last
    y_ref[...] = (h * _gelu_tanh(lg_ref[...])).astype(y_ref.dtype)

    @pl.when(c == nc - 1)
    def _():
        hout_ref[...] = h_last


def _lru_prompt(u, buf, h0, prm, nb, t, pos0):
    tc = 256
    nc = t // tc
    full = lambda a: pl.BlockSpec(a.shape, lambda b, c: (0,) * a.ndim)
    y, hout = pl.pallas_call(
        functools.partial(_lru_prompt_kernel, tc=tc, nc=nc, pos0=pos0), name="lru_prompt",
        grid=(nb, nc),
        in_specs=[pl.BlockSpec((tc, LRU_WIDTH), lambda b, c: (b * nc + c, 5)),
                  pl.BlockSpec((tc, LRU_WIDTH), lambda b, c: (b * nc + c, 6)),
                  pl.BlockSpec((None, CONV_WIDTH - 1, LRU_WIDTH), lambda b, c: (b, 0, 0)),
                  full(prm["w_conv_lru"]), full(prm["b_conv_lru"]), full(prm["w_lru_r"]), full(prm["b_lru_r"]),
                  full(prm["w_lru_i"]), full(prm["b_lru_i"]), full(prm["lru_lambda"]),
                  pl.BlockSpec((None, 1, LRU_WIDTH), lambda b, c: (b, 0, 0))],
        out_specs=[pl.BlockSpec((tc, LRU_WIDTH), lambda b, c: (b * nc + c, 0)),
                   pl.BlockSpec((None, 1, LRU_WIDTH), lambda b, c: (b, 0, 0))],
        out_shape=[jax.ShapeDtypeStruct((nb * t, LRU_WIDTH), BF16),
                   jax.ShapeDtypeStruct((nb, 1, LRU_WIDTH), F32)],
        scratch_shapes=[pltpu.VMEM((CONV_HALO + tc, LRU_WIDTH), F32), pltpu.VMEM((1, LRU_WIDTH), F32)],
        compiler_params=_cparams("parallel", "arbitrary"),
    )(u, u, buf, prm["w_conv_lru"], prm["b_conv_lru"], prm["w_lru_r"], prm["b_lru_r"], prm["w_lru_i"],
      prm["b_lru_i"], prm["lru_lambda"], h0)
    return y, hout


def _lru_step_kernel(lx_ref, lg_ref, buf_ref, wc_ref, bc_ref, wr_ref, br_ref, wi_ref, bi_ref, lam_ref, h0_ref,
                     y_ref, hout_ref, *, pos0):
    xc = _conv_step(buf_ref, lx_ref[...], wc_ref, bc_ref)
    a, b = _lru_gates(xc, wr_ref, br_ref, wi_ref, bi_ref, lam_ref, pos0 == 0)
    h = b + a * h0_ref[...]
    hout_ref[...] = h
    y_ref[...] = (h * _gelu_tanh(lg_ref[...])).astype(y_ref.dtype)


def _lru_step(u, buf_t, h0, prm, pos0):
    m = h0.shape[0]
    full = lambda a: pl.BlockSpec(a.shape, lambda i: (0,) * a.ndim)
    return pl.pallas_call(
        functools.partial(_lru_step_kernel, pos0=pos0), name="lru_step",
        grid=(1,),
        in_specs=[pl.BlockSpec((m, LRU_WIDTH), lambda i: (0, 5)),
                  pl.BlockSpec((m, LRU_WIDTH), lambda i: (0, 6)),
                  full(buf_t), full(prm["w_conv_lru"]), full(prm["b_conv_lru"]), full(prm["w_lru_r"]),
                  full(prm["b_lru_r"]), full(prm["w_lru_i"]), full(prm["b_lru_i"]), full(prm["lru_lambda"]),
                  full(h0)],
        out_specs=[pl.BlockSpec((m, LRU_WIDTH), lambda i: (0, 0)),
                   pl.BlockSpec((m, LRU_WIDTH), lambda i: (0, 0))],
        out_shape=[jax.ShapeDtypeStruct((m, LRU_WIDTH), BF16),
                   jax.ShapeDtypeStruct((m, LRU_WIDTH), F32)],
        compiler_params=_cparams("arbitrary"),
    )(u, u, buf_t, prm["w_conv_lru"], prm["b_conv_lru"], prm["w_lru_r"], prm["b_lru_r"], prm["w_lru_i"],
      prm["b_lru_i"], prm["lru_lambda"], h0)


def _rope_tables(n_pos):
    half = MLA_ROPE // 2
    inv = ROPE_THETA ** (-np.arange(half, dtype=np.float64) * (2.0 / MLA_ROPE))
    ang = np.arange(n_pos, dtype=np.float64)[:, None] * inv[None, :]
    cos, sin = np.cos(ang), np.sin(ang)
    cc = jnp.asarray(np.concatenate([cos, cos], axis=1), F32)
    ss = jnp.asarray(np.concatenate([-sin, sin], axis=1), F32)
    return cc, ss


def _even_params(e, w_in_e, w_pool, s_pool, g_qlat, w_qb, g_kvlat, w_kvb, g_q, g_k, w_out_e):
    nh = MLA_HEADS
    w_in = jnp.pad(jnp.transpose(w_in_e[e]).astype(BF16), ((0, EVEN_IN_PAD - w_in_e.shape[2]), (0, 0)))
    wq = w_qb[e].reshape(MLA_Q_LORA, nh, MLA_QK)
    wq_nope = wq[:, :, :MLA_NOPE].reshape(MLA_Q_LORA, nh * MLA_NOPE)
    wq_rope = jnp.pad(wq[:, :, MLA_NOPE:], ((0, 0), (0, 0), (0, LANES - MLA_ROPE))).reshape(MLA_Q_LORA, nh * LANES)
    wkv = w_kvb[e].reshape(MLA_KV_LORA, nh, MLA_NOPE + MLA_V)
    w_nope_t = jnp.transpose(wkv[:, :, :MLA_NOPE], (1, 2, 0))
    pad_g = lambda g: jnp.pad(g, (0, MLA_QKP - MLA_QK)).reshape(1, MLA_QKP)
    return dict(
        w_in=w_in,
        w_pool=w_pool[e].astype(BF16),
        s_pool=s_pool[e].reshape(1, POOL_WIDTH),
        g_qlat=g_qlat[e].reshape(1, MLA_Q_LORA),
        g_kvlat=g_kvlat[e].reshape(1, MLA_KV_LORA),
        w_qb=jnp.concatenate([wq_nope, wq_rope], axis=1).astype(BF16),
        w_kvb=w_kvb[e].astype(BF16),
        g_q=pad_g(g_q[e]), g_k=pad_g(g_k[e]),
        g_k_rope=g_k[e][MLA_NOPE:].reshape(MLA_ROPE, 1),
        w_nope_t=w_nope_t.astype(BF16),
        w_v=jnp.transpose(wkv[:, :, MLA_NOPE:], (1, 0, 2)).astype(BF16),
        w_out_pool=w_out_e[e][:POOL_WIDTH].astype(BF16),
        w_out_att=w_out_e[e][POOL_WIDTH:].astype(BF16),
    )


ODD_J2 = SSD_D_INNER + SSD_D_INNER + SSD_BC
ODD_J3 = ODD_J2 + SSD_HEADS
ODD_MAIN = ODD_J2 + 2 * LRU_WIDTH


def _prep_w_in_odd_kernel(w_ref, main_ref, dt_ref):
    kb = w_ref.shape[1]
    main_ref[0:ODD_J2, :] = w_ref[0:ODD_J2, :].astype(BF16)
    main_ref[ODD_J2:ODD_MAIN, :] = w_ref[ODD_J3:ODD_J3 + 2 * LRU_WIDTH, :].astype(BF16)
    dt_ref[...] = jnp.concatenate([w_ref[ODD_J2:ODD_J3, :], jnp.zeros((LANES - SSD_HEADS, kb), F32)],
                                  axis=0).astype(BF16)


def _prep_w_in_odd(w_in_o, o):
    _, k, n = w_in_o.shape
    kb = 256
    return pl.pallas_call(
        _prep_w_in_odd_kernel, name="prep_w_in_odd",
        grid=(k // kb,),
        in_specs=[pl.BlockSpec((None, n, kb), lambda i: (o, 0, i))],
        out_specs=[pl.BlockSpec((ODD_MAIN, kb), lambda i: (0, i)), pl.BlockSpec((LANES, kb), lambda i: (0, i))],
        out_shape=[jax.ShapeDtypeStruct((ODD_MAIN, k), BF16), jax.ShapeDtypeStruct((LANES, k), BF16)],
        compiler_params=_cparams("parallel"),
    )(jnp.transpose(w_in_o, (0, 2, 1)))


def _odd_params(o, w_in_o, w_conv_ssd, b_conv_ssd, dt_bias, a_log, d_skip, g_ssd_norm, w_conv_lru, b_conv_lru,
                w_lru_r, b_lru_r, w_lru_i, b_lru_i, lru_lambda, w_out_o):
    pad_heads = lambda v: jnp.pad(v, (0, LANES - SSD_HEADS)).reshape(1, LANES)
    w_in, w_in_dt = _prep_w_in_odd(w_in_o, o)
    return dict(
        w_in=w_in,
        w_in_dt=w_in_dt,
        w_conv_x=w_conv_ssd[o][:, :SSD_D_INNER], w_conv_bc=w_conv_ssd[o][:, SSD_D_INNER:],
        b_conv_x=b_conv_ssd[o][:SSD_D_INNER].reshape(1, -1), b_conv_bc=b_conv_ssd[o][SSD_D_INNER:].reshape(1, -1),
        dt_bias=pad_heads(dt_bias[o]), a_log=pad_heads(a_log[o]),
        d_skip=jnp.repeat(d_skip[o], SSD_HEAD_DIM).reshape(1, SSD_D_INNER),
        g_ssd_norm=g_ssd_norm[o].reshape(1, SSD_D_INNER),
        w_conv_lru=w_conv_lru[o], b_conv_lru=b_conv_lru[o].reshape(1, LRU_WIDTH),
        w_lru_r=w_lru_r[o].astype(BF16), b_lru_r=b_lru_r[o].reshape(1, LRU_WIDTH),
        w_lru_i=w_lru_i[o].astype(BF16), b_lru_i=b_lru_i[o].reshape(1, LRU_WIDTH),
        lru_lambda=lru_lambda[o].reshape(1, LRU_WIDTH),
        w_out_ssd=w_out_o[o][:SSD_D_INNER].astype(BF16),
        w_out_lru=w_out_o[o][SSD_D_INNER:].astype(BF16),
    )


def _pad_tab(tab):
    return jnp.pad(tab, ((0, 0), (0, LANES - tab.shape[1])))


def _even_layer_prompt(rows, hn, prm, nb, t):
    u = _matmul(rows, [hn], [prm["w_in"]], name="in_even", tn=EVEN_IN_PAD, w_t=True)
    zero_buf = jnp.zeros((nb, POOL_BUF, POOL_WIDTH), F32)
    y_pool = _pool_prompt(u, zero_buf, prm["w_pool"], prm["s_pool"], nb, t, 0)
    cc, ss = _rope_tables(t)
    tm = 256
    tiles = t // tm
    q, k, v, lat = _mla_prep(u, rows.m, tm, lambda i: i % tiles, (_pad_tab(cc), _pad_tab(ss)), prm, False)
    y_att = _flash_attention(q, k, v, nb, t)
    u3 = u.reshape(nb, t, EVEN_IN_PAD)
    kr = u3[:, :, POOL_WIDTH + MLA_Q_LORA + MLA_KV_LORA:POOL_WIDTH + MLA_Q_LORA + MLA_KV_LORA + MLA_ROPE]
    mla_rows = jnp.concatenate([lat.reshape(nb, t, MLA_KV_LORA), kr], axis=-1)
    pool_new = u3[:, t - POOL_BUF:, :POOL_WIDTH]
    return [y_pool, y_att], [prm["w_out_pool"], prm["w_out_att"]], mla_rows, pool_new


def _even_layer_sample(rows, hn, prm, pool_buf, cache, layer, page_table, pos0):
    m = rows.m
    nh = MLA_HEADS
    u = _matmul(rows, [hn], [prm["w_in"]], name="in_even", tn=640, w_t=True)
    y_pool = _pool_step(u, jnp.transpose(pool_buf, (1, 0, 2)), prm["w_pool"], prm["s_pool"], pos0)
    cc, ss = _rope_tables(pos0 + 1)
    cc_new = jnp.broadcast_to(_pad_tab(cc[pos0:]), (m, LANES))
    ss_new = jnp.broadcast_to(_pad_tab(ss[pos0:]), (m, LANES))
    q, k, v, lat, qg = _mla_prep(u, m, m, lambda i: 0, (cc_new, ss_new), prm, True)
    qp = _bmm_heads(qg, prm["w_nope_t"], BF16)
    qp = jnp.pad(jnp.transpose(qp, (1, 0, 2)), ((0, 0), (0, DEC_QROWS - nh), (0, 0)))
    q_m = jnp.transpose(q, (1, 0, 2))
    qr = jnp.pad(q_m[:, :, MLA_NOPE:MLA_NOPE + MLA_ROPE], ((0, 0), (0, 0), (0, LANES - MLA_ROPE)))
    ones_row = jnp.concatenate([jnp.zeros((m, 1, MLA_ROPE), BF16), jnp.ones((m, 1, LANES - MLA_ROPE), BF16)], axis=2)
    qr = jnp.concatenate([qr, ones_row, jnp.zeros((m, DEC_QROWS - nh - 1, LANES), BF16)], axis=1)
    o_lat = _mla_decode(cache, layer, page_table, prm["w_nope_t"].reshape(nh * MLA_NOPE, MLA_KV_LORA), qp, qr,
                        prm["g_k_rope"], cc[:pos0].T, ss[:pos0].T, q_m.astype(F32),
                        jnp.transpose(k, (1, 0, 2)).astype(F32), lat.reshape(m, 1, MLA_KV_LORA))
    y_att = _bmm_heads(jnp.transpose(o_lat, (1, 0, 2)).astype(BF16), prm["w_v"], BF16)
    y_att = jnp.transpose(y_att, (1, 0, 2)).reshape(m, nh * MLA_V)
    kr = u[:, POOL_WIDTH + MLA_Q_LORA + MLA_KV_LORA:POOL_WIDTH + MLA_Q_LORA + MLA_KV_LORA + MLA_ROPE]
    mla_rows = jnp.concatenate([lat, kr], axis=-1).reshape(m, 1, MLA_KV_LORA + MLA_ROPE)
    pool_new = jnp.concatenate([pool_buf[:, 1:], u[:, None, :POOL_WIDTH]], axis=1)
    return [y_pool, y_att], [prm["w_out_pool"], prm["w_out_att"]], mla_rows, pool_new


def _odd_layer_prompt(rows, hn, prm, nb, t):
    u = _matmul(rows, [hn], [prm["w_in"]], name="in_odd", tm=1024, tn=1792, w_t=True)
    u_dt = _matmul(rows, [hn], [prm["w_in_dt"]], name="in_odd_dt", tn=LANES, w_t=True)
    k1 = CONV_WIDTH - 1
    y_ssd, h_ssd = _ssd_prompt(u, u_dt, jnp.zeros((nb, k1, SSD_D_INNER), F32), jnp.zeros((nb, k1, SSD_BC), F32),
                               jnp.zeros((nb, SSD_D_INNER, SSD_STATE), F32), prm, nb, t)
    y_lru, h_lru = _lru_prompt(u, jnp.zeros((nb, k1, LRU_WIDTH), F32), jnp.zeros((nb, 1, LRU_WIDTH), F32),
                               prm, nb, t, 0)
    u3 = u.reshape(nb, t, -1)
    sconv = u3[:, t - k1:, SSD_D_INNER:2 * SSD_D_INNER + SSD_BC]
    lconv = u3[:, t - k1:, 2 * SSD_D_INNER + SSD_BC:2 * SSD_D_INNER + SSD_BC + LRU_WIDTH]
    return ([y_ssd, y_lru], [prm["w_out_ssd"], prm["w_out_lru"]], sconv,
            h_ssd.reshape(nb, SSD_HEADS, SSD_HEAD_DIM, SSD_STATE), lconv, h_lru.reshape(nb, LRU_WIDTH))


def _odd_layer_sample(rows, hn, prm, sconv_buf, ssd_state, lconv_buf, lru_state, pos0):
    m = rows.m
    u = _matmul(rows, [hn], [prm["w_in"]], name="in_odd", tn=1024, w_t=True)
    u_dt = _matmul(rows, [hn], [prm["w_in_dt"]], name="in_odd_dt", tn=LANES, w_t=True)
    sconv_t = jnp.transpose(sconv_buf, (1, 0, 2))
    xs, bcs, dtx, dae = _ssd_step_prep(u, u_dt, sconv_t[:, :, :SSD_D_INNER], sconv_t[:, :, SSD_D_INNER:], prm)
    nr = SSD_D_INNER // LANES
    cols = jnp.concatenate([dtx.reshape(m, nr, LANES), dae.reshape(m, nr, LANES)], axis=1)
    gs = SSD_GROUPS * SSD_STATE
    pad8 = lambda a: jnp.pad(a.reshape(m, SSD_GROUPS, SSD_STATE), ((0, 0), (0, 8 - SSD_GROUPS), (0, 0)))
    h_new, y = _ssd_step_state(ssd_state.reshape(m, SSD_D_INNER, SSD_STATE), cols,
                               pad8(bcs[:, :gs]), pad8(bcs[:, gs:]))
    y_ssd = _ssd_step_post(y.reshape(m, SSD_D_INNER), xs, u, prm)
    y_lru, h_lru = _lru_step(u, jnp.transpose(lconv_buf, (1, 0, 2)), lru_state, prm, pos0)
    u_xbc = u[:, SSD_D_INNER:2 * SSD_D_INNER + SSD_BC]
    u_lx = u[:, 2 * SSD_D_INNER + SSD_BC:2 * SSD_D_INNER + SSD_BC + LRU_WIDTH]
    sconv = jnp.concatenate([sconv_buf[:, 1:], u_xbc[:, None]], axis=1)
    lconv = jnp.concatenate([lconv_buf[:, 1:], u_lx[:, None]], axis=1)
    return ([y_ssd, y_lru], [prm["w_out_ssd"], prm["w_out_lru"]], sconv,
            h_new.reshape(m, SSD_HEADS, SSD_HEAD_DIM, SSD_STATE), lconv, h_lru)


def kernel(x_prompt, x_sample, cache_mla, state_pool, state_ssd_conv, state_ssd, state_lru_conv, state_lru, page_table, c_prompt, c_sample, g_norm1, g_norm2, w_mod, b_mod, w_mlp1, w_mlp2, w_in_e, w_pool, s_pool, g_qlat, w_qb, g_kvlat, w_kvb, g_q, g_k, w_out_e, w_in_o, w_conv_ssd, b_conv_ssd, dt_bias, a_log, d_skip, g_ssd_norm, w_conv_lru, b_conv_lru, w_lru_r, b_lru_r, w_lru_i, b_lru_i, lru_lambda, w_out_o):
    nb, t, _ = x_prompt.shape
    ns = x_sample.shape[0]
    assert x_sample.shape[1] == 1 and t >= POOL_BUF
    pos0_s = page_table.shape[1] * PAGE_SIZE

    pad_rows = (-(ns + nb)) % 8
    c_all = jnp.concatenate([c_sample, c_prompt, jnp.zeros((pad_rows, D_MODEL), F32)], axis=0)
    mod_all = _modulation(c_all, w_mod, b_mod)

    rows_p = _Rows(nb, t, 512)
    rows_s = _Rows(ns, 1, ns)
    xp = x_prompt.reshape(nb * t, D_MODEL)
    xs = x_sample.reshape(ns, D_MODEL)
    w2 = w_mlp2.astype(BF16)

    outs_p, outs_s = {}, {}
    for layer in range(DEPTH):
        mod_p = rows_p.mod_array(mod_all[layer, ns:ns + nb])
        mod_s = rows_s.mod_array(mod_all[layer, :ns])
        hn_p = _norm_mod(rows_p, xp, g_norm1[layer], mod_p, 1, 0)
        hn_s = _norm_mod(rows_s, xs, g_norm1[layer], mod_s, 1, 0)
        if layer % 2 == 0:
            e = layer // 2
            prm = _even_params(e, w_in_e, w_pool, s_pool, g_qlat, w_qb, g_kvlat, w_kvb, g_q, g_k, w_out_e)
            a_p, w_o, mla_p, pool_p = _even_layer_prompt(rows_p, hn_p, prm, nb, t)
            a_s, _, mla_s, pool_s = _even_layer_sample(rows_s, hn_s, prm, state_pool[e], cache_mla, e,
                                                       page_table, pos0_s)
            outs_p.setdefault("mla", []).append(mla_p)
            outs_p.setdefault("pool", []).append(pool_p)
            outs_s.setdefault("mla", []).append(mla_s)
            outs_s.setdefault("pool", []).append(pool_s)
        else:
            o = layer // 2
            prm = _odd_params(o, w_in_o, w_conv_ssd, b_conv_ssd, dt_bias, a_log, d_skip, g_ssd_norm, w_conv_lru,
                              b_conv_lru, w_lru_r, b_lru_r, w_lru_i, b_lru_i, lru_lambda, w_out_o)
            a_p, w_o, sconv_p, ssd_p, lconv_p, lru_p = _odd_layer_prompt(rows_p, hn_p, prm, nb, t)
            a_s, _, sconv_s, ssd_s, lconv_s, lru_s = _odd_layer_sample(
                rows_s, hn_s, prm, state_ssd_conv[o], state_ssd[o], state_lru_conv[o], state_lru[o], pos0_s)
            for d, vals in ((outs_p, (sconv_p, ssd_p, lconv_p, lru_p)), (outs_s, (sconv_s, ssd_s, lconv_s, lru_s))):
                for name, val in zip(("sconv", "ssd", "lconv", "lru"), vals):
                    d.setdefault(name, []).append(val)
        xp = _matmul(rows_p, a_p, w_o, name="out_proj", tn=1024, res=xp, mod=mod_p, gate_chunk=2)
        xs = _matmul(rows_s, a_s, w_o, name="out_proj_s", tn=1024, res=xs, mod=mod_s, gate_chunk=2)
        hn2_p = _norm_mod(rows_p, xp, g_norm2[layer], mod_p, 4, 3)
        hn2_s = _norm_mod(rows_s, xs, g_norm2[layer], mod_s, 4, 3)
        act_p = _matmul_ws(hn2_p, w_mlp1, layer, name="mlp1", tm=1024, tn=1024, act="relu2", out_dtype=BF16)
        act_s = _matmul(rows_s, [hn2_s], [(w_mlp1, layer)], name="mlp1_s", tn=1024, act="relu2", out_dtype=BF16)
        xp = _matmul(rows_p, [act_p], [(w2, layer)], name="mlp2", tn=512, res=xp, mod=mod_p, gate_chunk=5)
        xs = _matmul(rows_s, [act_s], [(w_mlp2, layer)], name="mlp2_s", tn=1024, tk=2048, res=xs, mod=mod_s,
                     gate_chunk=5)

    st = lambda d, name: jnp.stack(d[name])
    return (xp.reshape(nb, t, D_MODEL), xs.reshape(ns, 1, D_MODEL),
            st(outs_p, "mla"), st(outs_s, "mla"), st(outs_p, "pool"), st(outs_s, "pool"),
            st(outs_p, "sconv"), st(outs_s, "sconv"), st(outs_p, "ssd"), st(outs_s, "ssd"),
            st(outs_p, "lconv"), st(outs_s, "lconv"), st(outs_p, "lru"), st(outs_s, "lru"))
```

```python
import functools

import numpy as np
import jax
import jax.numpy as jnp
from jax import lax
from jax.experimental import pallas as pl
from jax.experimental.pallas import tpu as pltpu

F32 = jnp.float32
BF16 = jnp.bfloat16

VMEM_LIMIT_BYTES = 48 * 1024 * 1024
LANES = 128

D_MODEL = 2048
EPS = 1e-6
N_MOD = 6
DEPTH = 2

POOL_WIDTH = 1024
POOL_WINDOWS = (2, 4, 8, 16)
POOL_GROUP_DIM = 256
POOL_BUF = 15

MLA_HEADS = 8
MLA_NOPE = 128
MLA_ROPE = 64
MLA_V = 128
MLA_QK = 192
MLA_QKP = 256
MLA_Q_LORA = 512
MLA_KV_LORA = 256
ROPE_THETA = 10000.0
PAGE_SIZE = 128
EVEN_IN_PAD = 1920

SSD_D_INNER = 2048
SSD_HEAD_DIM = 64
SSD_HEADS = 32
SSD_GROUPS = 4
SSD_STATE = 128
SSD_CHUNK = 128
SSD_BC = 1024
CONV_WIDTH = 4

LRU_WIDTH = 1024
LRU_BLOCKS = 8
LRU_BLOCK_DIM = 128
LRU_C = 8.0

MLP_HIDDEN = 8192


def _cparams(*sem):
    return pltpu.CompilerParams(dimension_semantics=sem, vmem_limit_bytes=VMEM_LIMIT_BYTES)


def _sigmoid(x):
    return 1.0 / (1.0 + jnp.exp(-x))


def _silu(x):
    return x * _sigmoid(x)


def _softplus(x):
    return jnp.maximum(x, 0.0) + jnp.log1p(jnp.exp(-jnp.abs(x)))


def _gelu_tanh(x):
    return 0.5 * x * (1.0 + jnp.tanh(0.7978845608028654 * (x + 0.044715 * (x * x * x))))


def _dot(a, b):
    return jnp.dot(a, b, preferred_element_type=F32)


def _dot_nt(a, b):
    return lax.dot_general(a, b, (((1,), (1,)), ((), ())), preferred_element_type=F32)


def _split3(x):
    hi = x.astype(BF16)
    r = x - hi.astype(F32)
    mid = r.astype(BF16)
    lo = (r - mid.astype(F32)).astype(BF16)
    return hi, mid, lo


def _dot_sel(x, sel):
    hi, mid, lo = _split3(x)
    return _dot(hi, sel) + _dot(mid, sel) + _dot(lo, sel)


def _mod_kernel(c_ref, w_ref, b_ref, o_ref):
    c = c_ref[...]
    o_ref[...] = _dot(_silu(c).astype(BF16), w_ref[...].astype(BF16)) + b_ref[...]


def _modulation(c_all, w_mod, b_mod):
    mp = c_all.shape[0]
    n = N_MOD * D_MODEL
    tn = 1024
    return pl.pallas_call(
        _mod_kernel, name="modulation",
        grid=(DEPTH, n // tn),
        in_specs=[pl.BlockSpec((mp, D_MODEL), lambda l, j: (0, 0)),
                  pl.BlockSpec((None, D_MODEL, tn), lambda l, j: (l, 0, j)),
                  pl.BlockSpec((None, 1, tn), lambda l, j: (l, 0, j))],
        out_specs=pl.BlockSpec((None, mp, tn), lambda l, j: (l, 0, j)),
        out_shape=jax.ShapeDtypeStruct((DEPTH, mp, n), F32),
        compiler_params=_cparams("parallel", "parallel"),
    )(c_all, w_mod, b_mod.reshape(DEPTH, 1, n))


class _Rows:
    def __init__(self, nb, t, tm):
        self.nb, self.t, self.tm = nb, t, tm
        self.m = nb * t
        self.per_seq = t > 1
        if self.per_seq:
            assert t % tm == 0
            self.tiles_per_seq = t // tm

    def mod_array(self, mod_rows):
        return mod_rows.reshape(self.nb, 1, -1) if self.per_seq else mod_rows

    def mod_spec(self, chunk, width, ncol, col_of):
        per_chunk = D_MODEL // width
        if self.per_seq:
            tps = self.tiles_per_seq
            return pl.BlockSpec((None, 1, width),
                                lambda *g: (g[0] // tps, 0, chunk * per_chunk + col_of(*g)))
        return pl.BlockSpec((self.tm, width), lambda *g: (g[0], chunk * per_chunk + col_of(*g)))


def _norm_mod_kernel(x_ref, g_ref, sc_ref, sh_ref, o_ref):
    x = x_ref[...]
    y = x * lax.rsqrt(jnp.mean(x * x, axis=-1, keepdims=True) + EPS) * g_ref[...]
    o_ref[...] = (y * (1.0 + sc_ref[...]) + sh_ref[...]).astype(o_ref.dtype)


def _norm_mod(rows, x, g, mod, sc_chunk, sh_chunk):
    tm = min(rows.tm, 512)
    r = _Rows(rows.nb, rows.t, tm)
    return pl.pallas_call(
        _norm_mod_kernel, name="norm_mod",
        grid=(r.m // tm,),
        in_specs=[pl.BlockSpec((tm, D_MODEL), lambda i: (i, 0)),
                  pl.BlockSpec((1, D_MODEL), lambda i: (0, 0)),
                  r.mod_spec(sc_chunk, D_MODEL, 1, lambda i: 0),
                  r.mod_spec(sh_chunk, D_MODEL, 1, lambda i: 0)],
        out_specs=pl.BlockSpec((tm, D_MODEL), lambda i: (i, 0)),
        out_shape=jax.ShapeDtypeStruct((r.m, D_MODEL), BF16),
        compiler_params=_cparams("parallel"),
    )(x, g.reshape(1, D_MODEL), mod, mod)


def _mm_kernel(*refs, n_a, nk, act, has_res, w_t, has_norm=False):
    a_refs = refs[:n_a]
    w_refs = refs[n_a:2 * n_a]
    pos = 2 * n_a
    if has_res:
        x_ref, gt_ref = refs[pos], refs[pos + 1]
        pos += 2
    if has_norm:
        g_ref, sc_ref, sh_ref = refs[pos:pos + 3]
        pos += 3
    o_ref = refs[pos]
    pos += 1
    if has_norm:
        hn_ref = refs[pos]
        pos += 1
    acc_ref = refs[pos] if nk > 1 else None

    dot = _dot_nt if w_t else _dot
    part = dot(a_refs[0][...], w_refs[0][...].astype(BF16))
    for a_ref, w_ref in zip(a_refs[1:], w_refs[1:]):
        part = part + dot(a_ref[...], w_ref[...].astype(BF16))

    def finish(acc):
        if act == "relu2":
            acc = jnp.square(jnp.maximum(acc, 0.0))
        if has_res:
            acc = x_ref[...] + gt_ref[...] * acc
        o_ref[...] = acc.astype(o_ref.dtype)
        if has_norm:
            y = acc * lax.rsqrt(jnp.mean(acc * acc, axis=-1, keepdims=True) + EPS) * g_ref[...]
            hn_ref[...] = (y * (1.0 + sc_ref[...]) + sh_ref[...]).astype(hn_ref.dtype)

    if nk == 1:
        finish(part)
    else:
        k = pl.program_id(2)

        @pl.when(k == 0)
        def _():
            acc_ref[...] = part

        @pl.when(k > 0)
        def _():
            acc_ref[...] += part

        @pl.when(k == nk - 1)
        def _():
            finish(acc_ref[...])


def _matmul(rows, a_list, w_list, *, name, tn, tk=None, act=None, res=None, mod=None, gate_chunk=None,
            out_dtype=F32, tm=None, w_t=False, norm=None):
    assert tm is None or res is None
    tm, m = (tm or rows.tm), rows.m
    n_a = len(a_list)
    if w_t:
        assert n_a == 1 and tk is None and not isinstance(w_list[0], tuple)
        n, kfull = w_list[0].shape
        return pl.pallas_call(
            functools.partial(_mm_kernel, n_a=1, nk=1, act=act, has_res=False, w_t=True), name=name,
            grid=(m // tm, n // tn, 1),
            in_specs=[pl.BlockSpec((tm, kfull), lambda i, j, k: (i, 0)),
                      pl.BlockSpec((tn, kfull), lambda i, j, k: (j, 0))],
            out_specs=pl.BlockSpec((tm, tn), lambda i, j, k: (i, j)),
            out_shape=jax.ShapeDtypeStruct((m, n), out_dtype),
            compiler_params=_cparams("parallel", "parallel", "arbitrary"),
        )(a_list[0], w_list[0])
    n = (w_list[0][0] if isinstance(w_list[0], tuple) else w_list[0]).shape[-1]
    if tk is None or n_a > 1:
        nk = 1
    else:
        assert a_list[0].shape[1] % tk == 0
        nk = a_list[0].shape[1] // tk
    assert n % tn == 0 and m % tm == 0
    in_specs, args = [], []
    for a in a_list:
        kk = a.shape[1] if nk == 1 else tk
        in_specs.append(pl.BlockSpec((tm, kk), lambda i, j, k: (i, k)))
        args.append(a)
    for w in w_list:
        if isinstance(w, tuple):
            w, layer = w
            kk = w.shape[1] if nk == 1 else tk
            in_specs.append(pl.BlockSpec((None, kk, tn), lambda i, j, k, layer=layer: (layer, k, j)))
        else:
            kk = w.shape[0] if nk == 1 else tk
            in_specs.append(pl.BlockSpec((kk, tn), lambda i, j, k: (k, j)))
        args.append(w)
    has_res = res is not None
    if has_res:
        in_specs.append(pl.BlockSpec((tm, tn), lambda i, j, k: (i, j)))
        in_specs.append(rows.mod_spec(gate_chunk, tn, n // tn, lambda i, j, k: j))
        args += [res, mod]
    scratch = [pltpu.VMEM((tm, tn), F32)] if nk > 1 else []
    out_specs = [pl.BlockSpec((tm, tn), lambda i, j, k: (i, j))]
    out_shape = [jax.ShapeDtypeStruct((m, n), out_dtype)]
    if norm is not None:
        gain, sc_chunk, sh_chunk = norm
        assert tn == n and has_res
        in_specs += [pl.BlockSpec((1, n), lambda i, j, k: (0, 0)),
                     rows.mod_spec(sc_chunk, n, 1, lambda i, j, k: 0),
                     rows.mod_spec(sh_chunk, n, 1, lambda i, j, k: 0)]
        args += [gain.reshape(1, n), mod, mod]
        out_specs.append(pl.BlockSpec((tm, n), lambda i, j, k: (i, 0)))
        out_shape.append(jax.ShapeDtypeStruct((m, n), BF16))
    outs = pl.pallas_call(
        functools.partial(_mm_kernel, n_a=n_a, nk=nk, act=act, has_res=has_res, w_t=False,
                          has_norm=norm is not None), name=name,
        grid=(m // tm, n // tn, nk),
        in_specs=in_specs,
        out_specs=out_specs,
        out_shape=out_shape,
        scratch_shapes=scratch,
        compiler_params=_cparams("parallel", "parallel", "arbitrary"),
    )(*args)
    return outs if norm is not None else outs[0]


def _mm_ws_kernel(a_ref, w_ref, o_ref, wb_ref, *, act):
    @pl.when(pl.program_id(1) == 0)
    def _():
        wb_ref[...] = w_ref[...].astype(BF16)

    acc = _dot(a_ref[...], wb_ref[...])
    if act == "relu2":
        acc = jnp.square(jnp.maximum(acc, 0.0))
    o_ref[...] = acc.astype(o_ref.dtype)


def _matmul_ws(a, w_stack, layer, *, name, tm, tn, act=None, out_dtype=F32):
    m, k = a.shape
    n = w_stack.shape[-1]
    assert m % tm == 0 and n % tn == 0
    return pl.pallas_call(
        functools.partial(_mm_ws_kernel, act=act), name=name,
        grid=(n // tn, m // tm),
        in_specs=[pl.BlockSpec((tm, k), lambda j, i: (i, 0)),
                  pl.BlockSpec((None, k, tn), lambda j, i: (layer, 0, j))],
        out_specs=pl.BlockSpec((tm, tn), lambda j, i: (i, j)),
        out_shape=jax.ShapeDtypeStruct((m, n), out_dtype),
        scratch_shapes=[pltpu.VMEM((k, tn), BF16)],
        compiler_params=_cparams("parallel", "arbitrary"),
    )(a, w_stack)


def _bmm_kernel(a_ref, w_ref, o_ref):
    o_ref[...] = _dot(a_ref[...], w_ref[...]).astype(o_ref.dtype)


def _bmm_heads(a, w, out_dtype):
    h, m, k = a.shape
    n = w.shape[2]
    return pl.pallas_call(
        _bmm_kernel, name="bmm_heads",
        grid=(h,),
        in_specs=[pl.BlockSpec((None, m, k), lambda i: (i, 0, 0)),
                  pl.BlockSpec((None, k, n), lambda i: (i, 0, 0))],
        out_specs=pl.BlockSpec((None, m, n), lambda i: (i, 0, 0)),
        out_shape=jax.ShapeDtypeStruct((h, m, n), out_dtype),
        compiler_params=_cparams("parallel"),
    )(a, w)


CONV_HALO = 8


def _conv_chunk(ext_ref, u, buf_ref, w_ref, b_ref, first):
    tc = u.shape[0]
    k1 = CONV_WIDTH - 1

    @pl.when(first)
    def _():
        ext_ref[CONV_HALO - k1:CONV_HALO, :] = buf_ref[...]

    ext_ref[CONV_HALO:CONV_HALO + tc, :] = u
    y = b_ref[...] + w_ref[k1:k1 + 1, :] * u
    for k in range(k1):
        y = y + w_ref[k:k + 1, :] * ext_ref[CONV_HALO - k1 + k:CONV_HALO - k1 + k + tc, :]
    ext_ref[CONV_HALO - k1:CONV_HALO, :] = ext_ref[CONV_HALO + tc - k1:CONV_HALO + tc, :]
    return y


def _conv_step(buf_ref, u, w_ref, b_ref):
    y = b_ref[...] + w_ref[CONV_WIDTH - 1:CONV_WIDTH, :] * u
    for k in range(CONV_WIDTH - 1):
        y = y + w_ref[k:k + 1, :] * buf_ref[k]
    return y


def _pool_prompt_kernel(u_ref, buf_ref, w_ref, s_ref, o_ref, ext_ref, *, tc, pos0):
    c = pl.program_id(1)
    halo = 16

    @pl.when(c == 0)
    def _():
        ext_ref[0:1, :] = jnp.zeros((1, POOL_WIDTH), F32)
        ext_ref[1:halo, :] = buf_ref[...]

    u = u_ref[...]
    ext_ref[halo:halo + tc, :] = u
    pos = (pos0 + c * tc + lax.broadcasted_iota(jnp.int32, (tc, 1), 0)).astype(F32)
    outs = []
    for g, w in enumerate(POOL_WINDOWS):
        lo, hi = g * POOL_GROUP_DIM, (g + 1) * POOL_GROUP_DIM
        x = u[:, lo:hi]
        wsum = x
        for j in range(1, w):
            wsum = wsum + ext_ref[halo - j:halo - j + tc, lo:hi]
        cnt = jnp.minimum(pos + 1.0, float(w))
        d = (wsum / cnt - x).astype(BF16)
        outs.append(_dot(d, w_ref[g]))
    y = jnp.concatenate(outs, axis=1) * s_ref[...]
    o_ref[...] = y.astype(o_ref.dtype)
    ext_ref[0:halo, :] = ext_ref[tc:tc + halo, :]


def _pool_prompt(u, buf, w_pool, s_pool, nb, t, pos0):
    tc = 256
    nc = t // tc
    return pl.pallas_call(
        functools.partial(_pool_prompt_kernel, tc=tc, pos0=pos0), name="pool_prompt",
        grid=(nb, nc),
        in_specs=[pl.BlockSpec((tc, POOL_WIDTH), lambda b, c: (b * nc + c, 0)),
                  pl.BlockSpec((None, POOL_BUF, POOL_WIDTH), lambda b, c: (b, 0, 0)),
                  pl.BlockSpec((len(POOL_WINDOWS), POOL_GROUP_DIM, POOL_GROUP_DIM), lambda b, c: (0, 0, 0)),
                  pl.BlockSpec((1, POOL_WIDTH), lambda b, c: (0, 0))],
        out_specs=pl.BlockSpec((tc, POOL_WIDTH), lambda b, c: (b * nc + c, 0)),
        out_shape=jax.ShapeDtypeStruct((nb * t, POOL_WIDTH), BF16),
        scratch_shapes=[pltpu.VMEM((16 + tc, POOL_WIDTH), F32)],
        compiler_params=_cparams("parallel", "arbitrary"),
    )(u, buf, w_pool, s_pool)


def _pool_step_kernel(u_ref, buf_ref, w_ref, s_ref, o_ref, *, pos0):
    outs = []
    for g, w in enumerate(POOL_WINDOWS):
        lo, hi = g * POOL_GROUP_DIM, (g + 1) * POOL_GROUP_DIM
        x = u_ref[:, lo:hi]
        wsum = x
        for j in range(1, w):
            wsum = wsum + buf_ref[POOL_BUF - j, :, lo:hi]
        cnt = min(pos0 + 1.0, float(w))
        d = (wsum / cnt - x).astype(BF16)
        outs.append(_dot(d, w_ref[g]))
    o_ref[...] = (jnp.concatenate(outs, axis=1) * s_ref[...]).astype(o_ref.dtype)


def _pool_step(u, buf_t, w_pool, s_pool, pos0):
    m = buf_t.shape[1]
    return pl.pallas_call(
        functools.partial(_pool_step_kernel, pos0=pos0), name="pool_step",
        grid=(1,),
        in_specs=[pl.BlockSpec((m, POOL_WIDTH), lambda i: (0, 0)),
                  pl.BlockSpec((POOL_BUF, m, POOL_WIDTH), lambda i: (0, 0, 0)),
                  pl.BlockSpec((len(POOL_WINDOWS), POOL_GROUP_DIM, POOL_GROUP_DIM), lambda i: (0, 0, 0)),
                  pl.BlockSpec((1, POOL_WIDTH), lambda i: (0, 0))],
        out_specs=pl.BlockSpec((m, POOL_WIDTH), lambda i: (0, 0)),
        out_shape=jax.ShapeDtypeStruct((m, POOL_WIDTH), BF16),
        compiler_params=_cparams("arbitrary"),
    )(u, buf_t, w_pool, s_pool)


def _rope128(x, cc, ss):
    lane = lax.broadcasted_iota(jnp.int32, x.shape, 1)
    swapped = jnp.where(lane < MLA_ROPE // 2, pltpu.roll(x, LANES - MLA_ROPE // 2, 1),
                        pltpu.roll(x, MLA_ROPE // 2, 1))
    return x * cc + swapped * ss


def _mla_prep_kernel(uq_ref, ukv_ref, ukr_ref, gql_ref, gkvl_ref, wq_ref, wkv_ref, gq_ref, gk_ref,
                     cc_ref, ss_ref, *out_refs, emit_qg):
    if emit_qg:
        q_ref, k_ref, v_ref, lat_ref, qg_ref = out_refs
    else:
        q_ref, k_ref, v_ref, lat_ref = out_refs
    cc = cc_ref[...]
    ss = ss_ref[...]
    scale = MLA_QK ** -0.5

    uq = uq_ref[...]
    qn = uq * lax.rsqrt(jnp.mean(uq * uq, axis=-1, keepdims=True) + EPS) * gql_ref[...]
    qall = _dot(qn.astype(BF16), wq_ref[...])
    ukv = ukv_ref[...]
    lat = ukv * lax.rsqrt(jnp.mean(ukv * ukv, axis=-1, keepdims=True) + EPS) * gkvl_ref[...]
    lat_ref[...] = lat
    kvall = _dot(lat.astype(BF16), wkv_ref[...])
    kr = ukr_ref[...]
    kr_ss = jnp.sum(kr * kr, axis=-1, keepdims=True)

    gq_n, gq_r = gq_ref[:, :MLA_NOPE], gq_ref[:, MLA_NOPE:]
    gk_n, gk_r = gk_ref[:, :MLA_NOPE], gk_ref[:, MLA_NOPE:]
    nh = MLA_HEADS
    for h in range(nh):
        qnope = qall[:, h * 128:(h + 1) * 128]
        qrope = qall[:, (nh + h) * 128:(nh + h + 1) * 128]
        ssq = jnp.sum(qnope * qnope, axis=-1, keepdims=True) + jnp.sum(qrope * qrope, axis=-1, keepdims=True)
        rs = lax.rsqrt(ssq * (1.0 / MLA_QK) + EPS) * scale
        qn_h = qnope * rs * gq_n
        qr_h = _rope128(qrope * rs * gq_r, cc, ss)
        q_ref[h] = jnp.concatenate([qn_h, qr_h], axis=1).astype(q_ref.dtype)
        if emit_qg:
            qg_ref[h] = (qn_h * gk_n).astype(qg_ref.dtype)

        knope = kvall[:, h * 256:h * 256 + 128]
        ssk = jnp.sum(knope * knope, axis=-1, keepdims=True) + kr_ss
        rsk = lax.rsqrt(ssk * (1.0 / MLA_QK) + EPS)
        kr_h = _rope128(kr * rsk * gk_r, cc, ss)
        k_ref[h] = jnp.concatenate([knope * rsk * gk_n, kr_h], axis=1).astype(k_ref.dtype)
        v_ref[h] = kvall[:, h * 256 + 128:(h + 1) * 256].astype(v_ref.dtype)


def _mla_prep(u, m, tm, pos_of_tile, tabs, prm, emit_qg):
    cc_tab, ss_tab = tabs
    nh = MLA_HEADS
    out_shape = [jax.ShapeDtypeStruct((nh, m, MLA_QKP), BF16),
                 jax.ShapeDtypeStruct((nh, m, MLA_QKP), BF16),
                 jax.ShapeDtypeStruct((nh, m, MLA_V), BF16),
                 jax.ShapeDtypeStruct((m, MLA_KV_LORA), F32)]
    out_specs = [pl.BlockSpec((nh, tm, MLA_QKP), lambda i: (0, i, 0)),
                 pl.BlockSpec((nh, tm, MLA_QKP), lambda i: (0, i, 0)),
                 pl.BlockSpec((nh, tm, MLA_V), lambda i: (0, i, 0)),
                 pl.BlockSpec((tm, MLA_KV_LORA), lambda i: (i, 0))]
    if emit_qg:
        out_shape.append(jax.ShapeDtypeStruct((nh, m, MLA_NOPE), BF16))
        out_specs.append(pl.BlockSpec((nh, tm, MLA_NOPE), lambda i: (0, i, 0)))
    full = lambda a: pl.BlockSpec(a.shape, lambda i: (0,) * a.ndim)
    return pl.pallas_call(
        functools.partial(_mla_prep_kernel, emit_qg=emit_qg), name="mla_prep",
        grid=(m // tm,),
        in_specs=[pl.BlockSpec((tm, MLA_Q_LORA), lambda i: (i, POOL_WIDTH // MLA_Q_LORA)),
                  pl.BlockSpec((tm, MLA_KV_LORA), lambda i: (i, (POOL_WIDTH + MLA_Q_LORA) // MLA_KV_LORA)),
                  pl.BlockSpec((tm, LANES), lambda i: (i, (POOL_WIDTH + MLA_Q_LORA + MLA_KV_LORA) // LANES)),
                  full(prm["g_qlat"]), full(prm["g_kvlat"]), full(prm["w_qb"]), full(prm["w_kvb"]),
                  full(prm["g_q"]), full(prm["g_k"]),
                  pl.BlockSpec((tm, LANES), lambda i: (pos_of_tile(i), 0)),
                  pl.BlockSpec((tm, LANES), lambda i: (pos_of_tile(i), 0))],
        out_specs=out_specs,
        out_shape=out_shape,
        compiler_params=_cparams("parallel"),
    )(u, u, u, prm["g_qlat"], prm["g_kvlat"], prm["w_qb"], prm["w_kvb"], prm["g_q"], prm["g_k"],
      cc_tab, ss_tab)


def _flash_kernel(q_ref, k_ref, v_ref, o_ref, *, tq, tk):
    i = pl.program_id(2)
    q = q_ref[...]
    per_q = tq // tk

    def step(j, carry, band):
        m, l, acc = carry
        start = pl.multiple_of(j * tk, tk)
        kb = k_ref[pl.ds(start, tk), :]
        vb = v_ref[pl.ds(start, tk), :]
        s = _dot_nt(q, kb)
        if band is not None:
            r = lax.broadcasted_iota(jnp.int32, (tq, tk), 0)
            c = lax.broadcasted_iota(jnp.int32, (tq, tk), 1) + band * tk
            s = jnp.where(c <= r, s, -jnp.inf)
        m_new = jnp.maximum(m, jnp.max(s, axis=1, keepdims=True))
        alpha = jnp.exp(m - m_new)
        p = jnp.exp(s - m_new)
        l = alpha * l + jnp.sum(p, axis=1, keepdims=True)
        acc = alpha * acc + _dot(p.astype(BF16), vb)
        return m_new, l, acc

    carry = (jnp.full((tq, 1), -jnp.inf, F32), jnp.zeros((tq, 1), F32), jnp.zeros((tq, MLA_V), F32))
    carry = lax.fori_loop(0, i * per_q, lambda j, c: step(j, c, None), carry)
    for band in range(per_q):
        carry = step(i * per_q + band, carry, band)
    _, l, acc = carry
    o_ref[...] = (acc / l).astype(o_ref.dtype)


FLASH_TQ = 1024
FLASH_TK = 1024


def _flash_attention(q, k, v, nb, t):
    tq, tk = FLASH_TQ, FLASH_TK
    nq = t // tq
    nh = MLA_HEADS
    return pl.pallas_call(
        functools.partial(_flash_kernel, tq=tq, tk=tk), name="flash_prompt",
        grid=(nb, nh, nq),
        in_specs=[pl.BlockSpec((None, tq, MLA_QKP), lambda b, h, i: (h, b * nq + i, 0)),
                  pl.BlockSpec((None, t, MLA_QKP), lambda b, h, i: (h, b, 0)),
                  pl.BlockSpec((None, t, MLA_V), lambda b, h, i: (h, b, 0))],
        out_specs=pl.BlockSpec((tq, MLA_V), lambda b, h, i: (b * nq + i, h)),
        out_shape=jax.ShapeDtypeStruct((nb * t, nh * MLA_V), BF16),
        compiler_params=_cparams("parallel", "parallel", "arbitrary"),
    )(q, k, v)


DEC_CHUNK_PAGES = 8
DEC_SUB_PAGES = 8
DEC_QROWS = 16


def _page_copy(pt_ref, cache_ref, buf_ref, sem_ref, layer, b, i, slot):
    return pltpu.make_async_copy(cache_ref.at[layer, pt_ref[b, i]], buf_ref.at[slot, i], sem_ref.at[slot])


def _decode_kernel(pt_ref, cache_ref, wt_ref, qp_ref, qr_ref, gr_ref, cc_ref, ss_ref, qf_ref, kn_ref, latn_ref,
                   o_ref, buf_ref, sem_ref, lhs_ref, *latb_refs, n_pages, nsamp, layer):
    b = pl.program_id(0)
    nh = MLA_HEADS
    slot = lax.rem(b, 2)
    nw = nh * MLA_NOPE

    def fetch(bb, sl):
        for i in range(n_pages):
            _page_copy(pt_ref, cache_ref, buf_ref, sem_ref, layer, bb, i, sl).start()

    @pl.when(b == 0)
    def _():
        fetch(b, slot)
        lhs_ref[0:nw, :] = wt_ref[...]

    @pl.when(b + 1 < nsamp)
    def _():
        fetch(b + 1, 1 - slot)

    for i in range(n_pages):
        _page_copy(pt_ref, cache_ref, buf_ref, sem_ref, layer, b, i, slot).wait()
    pages = [buf_ref.at[slot, i] for i in range(n_pages)]

    lhs_ref[nw:nw + DEC_QROWS, :] = qp_ref[...]
    lhs = lhs_ref[...]
    gr = gr_ref[...]
    half = MLA_ROPE // 2
    sub = DEC_SUB_PAGES * PAGE_SIZE
    chunk = DEC_CHUNK_PAGES * PAGE_SIZE
    qr = qr_ref[...]

    def tile_scores(c, t):
        i0 = c * DEC_CHUNK_PAGES + t * DEC_SUB_PAGES
        k0 = i0 * PAGE_SIZE
        pg = pages[i0:i0 + DEC_SUB_PAGES]
        latb = jnp.concatenate([p[0:MLA_KV_LORA, :] for p in pg], axis=1).astype(BF16)
        kr = jnp.concatenate([p[MLA_KV_LORA:MLA_KV_LORA + MLA_ROPE, :] for p in pg], axis=1)
        latb_refs[c][:, t * sub:(t + 1) * sub] = latb
        kt = _dot(lhs, latb)
        ssn = jnp.concatenate(
            [jnp.sum(jnp.square(kt[h * 128:(h + 1) * 128, :]), axis=0, keepdims=True) for h in range(nh)],
            axis=0)
        sn = kt[nw:nw + nh, :]
        krg = kr * gr
        swapped = jnp.concatenate([krg[half:, :], krg[:half, :]], axis=0)
        rot = krg * cc_ref[:, k0:k0 + sub] + swapped * ss_ref[:, k0:k0 + sub]
        feat = jnp.concatenate([rot, kr * kr], axis=0).astype(BF16)
        rr = _dot(qr, feat)
        rs = lax.rsqrt((ssn + rr[nh:nh + 1]) * (1.0 / MLA_QK) + EPS)
        return rs * (sn + rr[:nh])

    def chunk_update(c, s, state):
        m_run, l_run, acc = state
        m_new = jnp.maximum(m_run, jnp.max(s, axis=1, keepdims=True))
        alpha = jnp.exp(m_run - m_new)
        p = jnp.exp(s - m_new)
        l_run = alpha * l_run + jnp.sum(p, axis=1, keepdims=True)
        acc = alpha * acc + _dot_nt(p.astype(BF16), latb_refs[c][...])
        return m_new, l_run, acc

    n_chunks = n_pages // DEC_CHUNK_PAGES
    n_tiles = DEC_CHUNK_PAGES // DEC_SUB_PAGES
    state = (jnp.full((nh, 1), -jnp.inf, F32), jnp.zeros((nh, 1), F32), jnp.zeros((nh, MLA_KV_LORA), F32))
    s_prev = None
    for c in range(n_chunks):
        s_cur = jnp.concatenate([tile_scores(c, t) for t in range(n_tiles)], axis=1)
        if s_prev is not None:
            state = chunk_update(c - 1, s_prev, state)
        s_prev = s_cur
    m_run, l_run, acc = chunk_update(n_chunks - 1, s_prev, state)

    s_new = jnp.sum(qf_ref[...] * kn_ref[...], axis=1, keepdims=True)
    m_fin = jnp.maximum(m_run, s_new)
    a_fin = jnp.exp(m_run - m_fin)
    p_new = jnp.exp(s_new - m_fin)
    l_fin = a_fin * l_run + p_new
    lat_new = latn_ref[...].astype(BF16).astype(F32)
    o_ref[...] = (a_fin * acc + p_new.astype(BF16).astype(F32) * lat_new) / l_fin


def _mla_decode(cache, layer, page_table, wt_nope, qp, qr, g_rope, cc_keys, ss_keys, qf, kn, lat_new):
    nsamp, n_pages = page_table.shape
    assert n_pages % DEC_CHUNK_PAGES == 0 and DEC_CHUNK_PAGES % DEC_SUB_PAGES == 0
    n_keys = n_pages * PAGE_SIZE
    kvd = cache.shape[-1]
    cache = jnp.transpose(cache, (0, 1, 3, 2))
    nh = MLA_HEADS
    grid_spec = pltpu.PrefetchScalarGridSpec(
        num_scalar_prefetch=1,
        grid=(nsamp,),
        in_specs=[
            pl.BlockSpec(memory_space=pl.ANY),
            pl.BlockSpec(wt_nope.shape, lambda b, pt: (0, 0)),
            pl.BlockSpec((None, DEC_QROWS, MLA_KV_LORA), lambda b, pt: (b, 0, 0)),
            pl.BlockSpec((None, DEC_QROWS, LANES), lambda b, pt: (b, 0, 0)),
            pl.BlockSpec((MLA_ROPE, 1), lambda b, pt: (0, 0)),
            pl.BlockSpec((MLA_ROPE, n_keys), lambda b, pt: (0, 0)),
            pl.BlockSpec((MLA_ROPE, n_keys), lambda b, pt: (0, 0)),
            pl.BlockSpec((None, nh, MLA_QKP), lambda b, pt: (b, 0, 0)),
            pl.BlockSpec((None, nh, MLA_QKP), lambda b, pt: (b, 0, 0)),
            pl.BlockSpec((None, 1, MLA_KV_LORA), lambda b, pt: (b, 0, 0)),
        ],
        out_specs=pl.BlockSpec((None, nh, MLA_KV_LORA), lambda b, pt: (b, 0, 0)),
        scratch_shapes=[pltpu.VMEM((2, n_pages, kvd, PAGE_SIZE), F32), pltpu.SemaphoreType.DMA((2,)),
                        pltpu.VMEM((nh * MLA_NOPE + DEC_QROWS, MLA_KV_LORA), BF16),
                        ] + [pltpu.VMEM((MLA_KV_LORA, DEC_CHUNK_PAGES * PAGE_SIZE), BF16)
                             for _ in range(n_pages // DEC_CHUNK_PAGES)],
    )
    return pl.pallas_call(
        functools.partial(_decode_kernel, n_pages=n_pages, nsamp=nsamp, layer=layer), name="mla_decode",
        grid_spec=grid_spec,
        out_shape=jax.ShapeDtypeStruct((nsamp, nh, MLA_KV_LORA), F32),
        compiler_params=_cparams("arbitrary"),
    )(page_table, cache, wt_nope, qp, qr, g_rope, cc_keys, ss_keys, qf, kn, lat_new)


def _ssd_gate_norm(y, xs, z, dskip, gnorm):
    y = (y + dskip * xs) * _silu(z)
    gw = SSD_D_INNER // SSD_GROUPS
    outs = []
    for g in range(SSD_GROUPS):
        yg = y[:, g * gw:(g + 1) * gw]
        outs.append(yg * lax.rsqrt(jnp.mean(yg * yg, axis=-1, keepdims=True) + EPS))
    return jnp.concatenate(outs, axis=1) * gnorm


def _ssd_prompt_kernel(z_ref, x_ref, bc_ref, dt_ref, bufx_ref, bufbc_ref, wx_ref, bx_ref, wbc_ref, bbc_ref,
                       dtb_ref, alog_ref, dskip_ref, gnorm_ref, h0_ref, tri_ref, exp_ref,
                       y_ref, hout_ref, extx_ref, extbc_ref, ht_ref, *, nc):
    c = pl.program_id(1)
    first = c == 0
    l = SSD_CHUNK
    xs = _silu(_conv_chunk(extx_ref, x_ref[...], bufx_ref, wx_ref, bx_ref, first))
    bcs = _silu(_conv_chunk(extbc_ref, bc_ref[...], bufbc_ref, wbc_ref, bbc_ref, first))
    gs = SSD_GROUPS * SSD_STATE
    bm, cm = bcs[:, :gs], bcs[:, gs:]

    @pl.when(first)
    def _():
        for r in range(SSD_D_INNER // LANES):
            ht_ref[:, r * LANES:(r + 1) * LANES] = h0_ref[r * LANES:(r + 1) * LANES, :].T

    dt = _softplus(dt_ref[...] + dtb_ref[...])
    a = -jnp.exp(alog_ref[...])
    acum = _dot_sel_left(tri_ref[...], dt * a)
    acum_t = acum.T
    dt_t = dt.T
    a_last = acum[l - 1:l, :]
    dend = jnp.exp(a_last - acum)
    sel = exp_ref[...]
    w_state = _dot_sel(dt * dend, sel)
    e_acc = _dot_sel(jnp.exp(acum), sel)
    e_last = e_acc[l - 1:l, :]

    rows = lax.broadcasted_iota(jnp.int32, (l, l), 0)
    cols = lax.broadcasted_iota(jnp.int32, (l, l), 1)
    causal = cols <= rows
    lane = lax.broadcasted_iota(jnp.int32, (l, LANES), 1)
    xs_b = xs.astype(BF16)
    hpg = SSD_HEADS // SSD_GROUPS
    gw = SSD_D_INNER // SSD_GROUPS
    y_groups = []
    for g in range(SSD_GROUPS):
        gcols = slice(g * gw, (g + 1) * gw)
        bg = bm[:, g * SSD_STATE:(g + 1) * SSD_STATE]
        cg_b = cm[:, g * SSD_STATE:(g + 1) * SSD_STATE].astype(BF16)
        cb = _dot_nt(cg_b, bg.astype(BF16))
        ht_g = ht_ref[:, gcols]
        y_off = _dot(cg_b, ht_g.astype(BF16))
        y_pairs = []
        for pr in range(hpg // 2):
            col = (g * hpg + pr * 2) * SSD_HEAD_DIM
            res = []
            for hh in range(2):
                h = g * hpg + pr * 2 + hh
                diff = acum[:, h:h + 1] - acum_t[h:h + 1, :]
                lmat = jnp.exp(jnp.where(causal, diff, -jnp.inf)) * dt_t[h:h + 1, :]
                res.append(_dot((cb * lmat).astype(BF16), xs_b[:, col:col + LANES]))
            y_pairs.append(jnp.where(lane < SSD_HEAD_DIM, res[0], res[1]))
        y_groups.append(jnp.concatenate(y_pairs, axis=1) + e_acc[:, gcols] * y_off)
        xw = (xs[:, gcols] * w_state[:, gcols]).astype(BF16)
        ht_ref[:, gcols] = e_last[:, gcols] * ht_g + _dot(bg.T.astype(BF16), xw)
    y = jnp.concatenate(y_groups, axis=1)
    y_ref[...] = _ssd_gate_norm(y, xs, z_ref[...], dskip_ref[...], gnorm_ref[...]).astype(y_ref.dtype)

    @pl.when(c == nc - 1)
    def _():
        for r in range(SSD_D_INNER // LANES):
            hout_ref[r * LANES:(r + 1) * LANES, :] = ht_ref[:, r * LANES:(r + 1) * LANES].T


def _dot_sel_left(sel, x):
    hi, mid, lo = _split3(x)
    return _dot(sel, hi) + _dot(sel, mid) + _dot(sel, lo)


def _ssd_consts():
    l = SSD_CHUNK
    tri = (jnp.arange(l)[:, None] >= jnp.arange(l)[None, :]).astype(BF16)
    sel = (jnp.arange(LANES)[:, None] == (jnp.arange(SSD_D_INNER)[None, :] // SSD_HEAD_DIM)).astype(BF16)
    return tri, sel


def _ssd_prompt(u, u_dt, bufx, bufbc, h0, prm, nb, t):
    l = SSD_CHUNK
    nc = t // l
    tri, sel = _ssd_consts()
    full = lambda a: pl.BlockSpec(a.shape, lambda b, c: (0,) * a.ndim)
    row = lambda width, colblk: pl.BlockSpec((l, width), lambda b, c: (b * nc + c, colblk))
    y, hout = pl.pallas_call(
        functools.partial(_ssd_prompt_kernel, nc=nc), name="ssd_prompt",
        grid=(nb, nc),
        in_specs=[row(SSD_D_INNER, 0), row(SSD_D_INNER, 1), row(SSD_BC, 4), row(LANES, 0),
                  pl.BlockSpec((None, CONV_WIDTH - 1, SSD_D_INNER), lambda b, c: (b, 0, 0)),
                  pl.BlockSpec((None, CONV_WIDTH - 1, SSD_BC), lambda b, c: (b, 0, 0)),
                  full(prm["w_conv_x"]), full(prm["b_conv_x"]), full(prm["w_conv_bc"]), full(prm["b_conv_bc"]),
                  full(prm["dt_bias"]), full(prm["a_log"]), full(prm["d_skip"]), full(prm["g_ssd_norm"]),
                  pl.BlockSpec((None, SSD_D_INNER, SSD_STATE), lambda b, c: (b, 0, 0)),
                  full(tri), full(sel)],
        out_specs=[pl.BlockSpec((l, SSD_D_INNER), lambda b, c: (b * nc + c, 0)),
                   pl.BlockSpec((None, SSD_D_INNER, SSD_STATE), lambda b, c: (b, 0, 0))],
        out_shape=[jax.ShapeDtypeStruct((nb * t, SSD_D_INNER), BF16),
                   jax.ShapeDtypeStruct((nb, SSD_D_INNER, SSD_STATE), F32)],
        scratch_shapes=[pltpu.VMEM((CONV_HALO + l, SSD_D_INNER), F32),
                        pltpu.VMEM((CONV_HALO + l, SSD_BC), F32),
                        pltpu.VMEM((SSD_STATE, SSD_D_INNER), F32)],
        compiler_params=_cparams("parallel", "arbitrary"),
    )(u, u, u, u_dt, bufx, bufbc, prm["w_conv_x"], prm["b_conv_x"], prm["w_conv_bc"], prm["b_conv_bc"],
      prm["dt_bias"], prm["a_log"], prm["d_skip"], prm["g_ssd_norm"], h0, tri, sel)
    return y, hout


def _ssd_step_prep_kernel(x_ref, bc_ref, dt_ref, bufx_ref, bufbc_ref, wx_ref, bx_ref, wbc_ref, bbc_ref,
                          dtb_ref, alog_ref, exp_ref, xs_ref, bcs_ref, dtx_ref, dae_ref):
    xs = _silu(_conv_step(bufx_ref, x_ref[...], wx_ref, bx_ref))
    xs_ref[...] = xs
    bcs_ref[...] = _silu(_conv_step(bufbc_ref, bc_ref[...], wbc_ref, bbc_ref))
    dt = _softplus(dt_ref[...] + dtb_ref[...])
    da = jnp.exp(dt * (-jnp.exp(alog_ref[...])))
    sel = exp_ref[...]
    dtx_ref[...] = _dot_sel(dt, sel) * xs
    dae_ref[...] = _dot_sel(da, sel)


def _ssd_step_prep(u, u_dt, bufx_t, bufbc_t, prm):
    m = u_dt.shape[0]
    _, sel = _ssd_consts()
    full = lambda a: pl.BlockSpec(a.shape, lambda i: (0,) * a.ndim)
    return pl.pallas_call(
        _ssd_step_prep_kernel, name="ssd_step_prep",
        grid=(1,),
        in_specs=[pl.BlockSpec((m, SSD_D_INNER), lambda i: (0, 1)),
                  pl.BlockSpec((m, SSD_BC), lambda i: (0, 4)),
                  full(u_dt), full(bufx_t), full(bufbc_t),
                  full(prm["w_conv_x"]), full(prm["b_conv_x"]), full(prm["w_conv_bc"]), full(prm["b_conv_bc"]),
                  full(prm["dt_bias"]), full(prm["a_log"]), full(sel)],
        out_specs=[pl.BlockSpec((m, SSD_D_INNER), lambda i: (0, 0)),
                   pl.BlockSpec((m, SSD_BC), lambda i: (0, 0)),
                   pl.BlockSpec((m, SSD_D_INNER), lambda i: (0, 0)),
                   pl.BlockSpec((m, SSD_D_INNER), lambda i: (0, 0))],
        out_shape=[jax.ShapeDtypeStruct((m, SSD_D_INNER), F32),
                   jax.ShapeDtypeStruct((m, SSD_BC), F32),
                   jax.ShapeDtypeStruct((m, SSD_D_INNER), F32),
                   jax.ShapeDtypeStruct((m, SSD_D_INNER), F32)],
        compiler_params=_cparams("arbitrary"),
    )(u, u, u_dt, bufx_t, bufbc_t, prm["w_conv_x"], prm["b_conv_x"], prm["w_conv_bc"], prm["b_conv_bc"],
      prm["dt_bias"], prm["a_log"], sel)


SSD_STEP_SAMPLES = 2


def _ssd_step_state_kernel(h0_ref, cols_ref, b_ref, c_ref, hn_ref, y_ref):
    nr = SSD_D_INNER // LANES
    rows_per_group = SSD_D_INNER // SSD_GROUPS // LANES
    gw = SSD_D_INNER // SSD_GROUPS
    for s in range(SSD_STEP_SAMPLES):
        pad = jnp.concatenate([cols_ref[s], jnp.zeros((LANES - 2 * nr, LANES), F32)], axis=0)
        ct = pad.T
        for r in range(nr):
            g = r // rows_per_group
            hn_ref[s, r * LANES:(r + 1) * LANES, :] = (
                ct[:, nr + r:nr + r + 1] * h0_ref[s, r * LANES:(r + 1) * LANES, :]
                + ct[:, r:r + 1] * b_ref[s, g:g + 1, :])
        res = _dot_nt(c_ref[s].astype(BF16), hn_ref[s].astype(BF16))
        y_ref[s] = jnp.concatenate([res[g:g + 1, g * gw:(g + 1) * gw] for g in range(SSD_GROUPS)], axis=1)


def _ssd_step_state(h0, cols, bmat, cmat):
    m = h0.shape[0]
    sps = SSD_STEP_SAMPLES
    assert m % sps == 0
    per = lambda shape: pl.BlockSpec((sps,) + shape, lambda i: (i, 0, 0))
    return pl.pallas_call(
        _ssd_step_state_kernel, name="ssd_step_state",
        grid=(m // sps,),
        in_specs=[per((SSD_D_INNER, SSD_STATE)), per((2 * SSD_D_INNER // LANES, LANES)),
                  per((8, SSD_STATE)), per((8, SSD_STATE))],
        out_specs=[per((SSD_D_INNER, SSD_STATE)), per((1, SSD_D_INNER))],
        out_shape=[jax.ShapeDtypeStruct((m, SSD_D_INNER, SSD_STATE), F32),
                   jax.ShapeDtypeStruct((m, 1, SSD_D_INNER), F32)],
        compiler_params=_cparams("parallel"),
    )(h0, cols, bmat, cmat)


def _ssd_step_post_kernel(y_ref, xs_ref, z_ref, dskip_ref, gnorm_ref, o_ref):
    o_ref[...] = _ssd_gate_norm(y_ref[...], xs_ref[...], z_ref[...], dskip_ref[...],
                                gnorm_ref[...]).astype(o_ref.dtype)


def _ssd_step_post(y, xs, u, prm):
    m = y.shape[0]
    full = lambda a: pl.BlockSpec(a.shape, lambda i: (0,) * a.ndim)
    return pl.pallas_call(
        _ssd_step_post_kernel, name="ssd_step_post",
        grid=(1,),
        in_specs=[full(y), full(xs), pl.BlockSpec((m, SSD_D_INNER), lambda i: (0, 0)),
                  full(prm["d_skip"]), full(prm["g_ssd_norm"])],
        out_specs=pl.BlockSpec((m, SSD_D_INNER), lambda i: (0, 0)),
        out_shape=jax.ShapeDtypeStruct((m, SSD_D_INNER), BF16),
        compiler_params=_cparams("arbitrary"),
    )(y, xs, u, prm["d_skip"], prm["g_ssd_norm"])


def _lru_gates(xc, wr_ref, br_ref, wi_ref, bi_ref, lam_ref, is_pos0):
    xb = xc.astype(BF16)
    rs, is_ = [], []
    for n in range(LRU_BLOCKS):
        blk = xb[:, n * LRU_BLOCK_DIM:(n + 1) * LRU_BLOCK_DIM]
        rs.append(_dot(blk, wr_ref[n]))
        is_.append(_dot(blk, wi_ref[n]))
    r = _sigmoid(jnp.concatenate(rs, axis=1) + br_ref[...])
    i = _sigmoid(jnp.concatenate(is_, axis=1) + bi_ref[...])
    log_a = -LRU_C * r * _softplus(-lam_ref[...])
    a = jnp.exp(log_a)
    th = jnp.tanh(log_a)
    mult = jnp.sqrt(-2.0 * th / (1.0 - th))
    if is_pos0 is not False:
        mult = jnp.where(is_pos0, 1.0, mult)
    return a, mult * i * xc


SUBLANES = 8


def _scan_rows(a, b, h_in):
    tc, w = a.shape
    ng = tc // SUBLANES
    a3 = a.reshape(ng, SUBLANES, w)
    b3 = b.reshape(ng, SUBLANES, w)
    sub = lax.broadcasted_iota(jnp.int32, a3.shape, 1)
    d = 1
    while d < SUBLANES:
        keep = sub >= d
        b3 = b3 + a3 * jnp.where(keep, pltpu.roll(b3, d, 1), 0.0)
        a3 = a3 * jnp.where(keep, pltpu.roll(a3, d, 1), 1.0)
        d *= 2
    carry = h_in
    outs = []
    for g in range(ng):
        hg = b3[g] + a3[g] * carry
        outs.append(hg)
        carry = hg[SUBLANES - 1:SUBLANES, :]
    return jnp.concatenate(outs, axis=0), carry


def _lru_prompt_kernel(lx_ref, lg_ref, buf_ref, wc_ref, bc_ref, wr_ref, br_ref, wi_ref, bi_ref, lam_ref, h0_ref,
                       y_ref, hout_ref, ext_ref, h_ref, *, tc, nc, pos0):
    c = pl.program_id(1)
    first = c == 0

    @pl.when(first)
    def _():
        h_ref[...] = h0_ref[...]

    xc = _conv_chunk(ext_ref, lx_ref[...], buf_ref, wc_ref, bc_ref, first)
    pos = pos0 + c * tc + lax.broadcasted_iota(jnp.int32, (tc, 1), 0)
    a, b = _lru_gates(xc, wr_ref, br_ref, wi_ref, bi_ref, lam_ref, pos == 0)
    h, h_last = _scan_rows(a, b, h_ref[...])
    h_ref[...] = h_last
    y_ref[...] = (h * _gelu_tanh(lg_ref[...])).astype(y_ref.dtype)

    @pl.when(c == nc - 1)
    def _():
        hout_ref[...] = h_last


def _lru_prompt(u, buf, h0, prm, nb, t, pos0):
    tc = 256
    nc = t // tc
    full = lambda a: pl.BlockSpec(a.shape, lambda b, c: (0,) * a.ndim)
    y, hout = pl.pallas_call(
        functools.partial(_lru_prompt_kernel, tc=tc, nc=nc, pos0=pos0), name="lru_prompt",
        grid=(nb, nc),
        in_specs=[pl.BlockSpec((tc, LRU_WIDTH), lambda b, c: (b * nc + c, 5)),
                  pl.BlockSpec((tc, LRU_WIDTH), lambda b, c: (b * nc + c, 6)),
                  pl.BlockSpec((None, CONV_WIDTH - 1, LRU_WIDTH), lambda b, c: (b, 0, 0)),
                  full(prm["w_conv_lru"]), full(prm["b_conv_lru"]), full(prm["w_lru_r"]), full(prm["b_lru_r"]),
                  full(prm["w_lru_i"]), full(prm["b_lru_i"]), full(prm["lru_lambda"]),
                  pl.BlockSpec((None, 1, LRU_WIDTH), lambda b, c: (b, 0, 0))],
        out_specs=[pl.BlockSpec((tc, LRU_WIDTH), lambda b, c: (b * nc + c, 0)),
                   pl.BlockSpec((None, 1, LRU_WIDTH), lambda b, c: (b, 0, 0))],
        out_shape=[jax.ShapeDtypeStruct((nb * t, LRU_WIDTH), BF16),
                   jax.ShapeDtypeStruct((nb, 1, LRU_WIDTH), F32)],
        scratch_shapes=[pltpu.VMEM((CONV_HALO + tc, LRU_WIDTH), F32), pltpu.VMEM((1, LRU_WIDTH), F32)],
        compiler_params=_cparams("parallel", "arbitrary"),
    )(u, u, buf, prm["w_conv_lru"], prm["b_conv_lru"], prm["w_lru_r"], prm["b_lru_r"], prm["w_lru_i"],
      prm["b_lru_i"], prm["lru_lambda"], h0)
    return y, hout


def _lru_step_kernel(lx_ref, lg_ref, buf_ref, wc_ref, bc_ref, wr_ref, br_ref, wi_ref, bi_ref, lam_ref, h0_ref,
                     y_ref, hout_ref, *, pos0):
    xc = _conv_step(buf_ref, lx_ref[...], wc_ref, bc_ref)
    a, b = _lru_gates(xc, wr_ref, br_ref, wi_ref, bi_ref, lam_ref, pos0 == 0)
    h = b + a * h0_ref[...]
    hout_ref[...] = h
    y_ref[...] = (h * _gelu_tanh(lg_ref[...])).astype(y_ref.dtype)


def _lru_step(u, buf_t, h0, prm, pos0):
    m = h0.shape[0]
    full = lambda a: pl.BlockSpec(a.shape, lambda i: (0,) * a.ndim)
    return pl.pallas_call(
        functools.partial(_lru_step_kernel, pos0=pos0), name="lru_step",
        grid=(1,),
        in_specs=[pl.BlockSpec((m, LRU_WIDTH), lambda i: (0, 5)),
                  pl.BlockSpec((m, LRU_WIDTH), lambda i: (0, 6)),
                  full(buf_t), full(prm["w_conv_lru"]), full(prm["b_conv_lru"]), full(prm["w_lru_r"]),
                  full(prm["b_lru_r"]), full(prm["w_lru_i"]), full(prm["b_lru_i"]), full(prm["lru_lambda"]),
                  full(h0)],
        out_specs=[pl.BlockSpec((m, LRU_WIDTH), lambda i: (0, 0)),
                   pl.BlockSpec((m, LRU_WIDTH), lambda i: (0, 0))],
        out_shape=[jax.ShapeDtypeStruct((m, LRU_WIDTH), BF16),
                   jax.ShapeDtypeStruct((m, LRU_WIDTH), F32)],
        compiler_params=_cparams("arbitrary"),
    )(u, u, buf_t, prm["w_conv_lru"], prm["b_conv_lru"], prm["w_lru_r"], prm["b_lru_r"], prm["w_lru_i"],
      prm["b_lru_i"], prm["lru_lambda"], h0)


def _rope_tables(n_pos):
    half = MLA_ROPE // 2
    inv = ROPE_THETA ** (-np.arange(half, dtype=np.float64) * (2.0 / MLA_ROPE))
    ang = np.arange(n_pos, dtype=np.float64)[:, None] * inv[None, :]
    cos, sin = np.cos(ang), np.sin(ang)
    cc = jnp.asarray(np.concatenate([cos, cos], axis=1), F32)
    ss = jnp.asarray(np.concatenate([-sin, sin], axis=1), F32)
    return cc, ss


def _even_params(e, w_in_e, w_pool, s_pool, g_qlat, w_qb, g_kvlat, w_kvb, g_q, g_k, w_out_e):
    nh = MLA_HEADS
    w_in = jnp.pad(jnp.transpose(w_in_e[e]).astype(BF16), ((0, EVEN_IN_PAD - w_in_e.shape[2]), (0, 0)))
    wq = w_qb[e].reshape(MLA_Q_LORA, nh, MLA_QK)
    wq_nope = wq[:, :, :MLA_NOPE].reshape(MLA_Q_LORA, nh * MLA_NOPE)
    wq_rope = jnp.pad(wq[:, :, MLA_NOPE:], ((0, 0), (0, 0), (0, LANES - MLA_ROPE))).reshape(MLA_Q_LORA, nh * LANES)
    wkv = w_kvb[e].reshape(MLA_KV_LORA, nh, MLA_NOPE + MLA_V)
    w_nope_t = jnp.transpose(wkv[:, :, :MLA_NOPE], (1, 2, 0))
    pad_g = lambda g: jnp.pad(g, (0, MLA_QKP - MLA_QK)).reshape(1, MLA_QKP)
    return dict(
        w_in=w_in,
        w_pool=w_pool[e].astype(BF16),
        s_pool=s_pool[e].reshape(1, POOL_WIDTH),
        g_qlat=g_qlat[e].reshape(1, MLA_Q_LORA),
        g_kvlat=g_kvlat[e].reshape(1, MLA_KV_LORA),
        w_qb=jnp.concatenate([wq_nope, wq_rope], axis=1).astype(BF16),
        w_kvb=w_kvb[e].astype(BF16),
        g_q=pad_g(g_q[e]), g_k=pad_g(g_k[e]),
        g_k_rope=g_k[e][MLA_NOPE:].reshape(MLA_ROPE, 1),
        w_nope_t=w_nope_t.astype(BF16),
        w_v=jnp.transpose(wkv[:, :, MLA_NOPE:], (1, 0, 2)).astype(BF16),
        w_out_pool=w_out_e[e][:POOL_WIDTH].astype(BF16),
        w_out_att=w_out_e[e][POOL_WIDTH:].astype(BF16),
    )


ODD_J2 = SSD_D_INNER + SSD_D_INNER + SSD_BC
ODD_J3 = ODD_J2 + SSD_HEADS
ODD_MAIN = ODD_J2 + 2 * LRU_WIDTH


def _prep_w_in_odd_kernel(w_ref, main_ref, dt_ref):
    kb = w_ref.shape[1]
    main_ref[0:ODD_J2, :] = w_ref[0:ODD_J2, :].astype(BF16)
    main_ref[ODD_J2:ODD_MAIN, :] = w_ref[ODD_J3:ODD_J3 + 2 * LRU_WIDTH, :].astype(BF16)
    dt_ref[...] = jnp.concatenate([w_ref[ODD_J2:ODD_J3, :], jnp.zeros((LANES - SSD_HEADS, kb), F32)],
                                  axis=0).astype(BF16)


def _prep_w_in_odd(w_in_o, o):
    _, k, n = w_in_o.shape
    kb = 256
    return pl.pallas_call(
        _prep_w_in_odd_kernel, name="prep_w_in_odd",
        grid=(k // kb,),
        in_specs=[pl.BlockSpec((None, n, kb), lambda i: (o, 0, i))],
        out_specs=[pl.BlockSpec((ODD_MAIN, kb), lambda i: (0, i)), pl.BlockSpec((LANES, kb), lambda i: (0, i))],
        out_shape=[jax.ShapeDtypeStruct((ODD_MAIN, k), BF16), jax.ShapeDtypeStruct((LANES, k), BF16)],
        compiler_params=_cparams("parallel"),
    )(jnp.transpose(w_in_o, (0, 2, 1)))


def _odd_params(o, w_in_o, w_conv_ssd, b_conv_ssd, dt_bias, a_log, d_skip, g_ssd_norm, w_conv_lru, b_conv_lru,
                w_lru_r, b_lru_r, w_lru_i, b_lru_i, lru_lambda, w_out_o):
    pad_heads = lambda v: jnp.pad(v, (0, LANES - SSD_HEADS)).reshape(1, LANES)
    w_in, w_in_dt = _prep_w_in_odd(w_in_o, o)
    return dict(
        w_in=w_in,
        w_in_dt=w_in_dt,
        w_conv_x=w_conv_ssd[o][:, :SSD_D_INNER], w_conv_bc=w_conv_ssd[o][:, SSD_D_INNER:],
        b_conv_x=b_conv_ssd[o][:SSD_D_INNER].reshape(1, -1), b_conv_bc=b_conv_ssd[o][SSD_D_INNER:].reshape(1, -1),
        dt_bias=pad_heads(dt_bias[o]), a_log=pad_heads(a_log[o]),
        d_skip=jnp.repeat(d_skip[o], SSD_HEAD_DIM).reshape(1, SSD_D_INNER),
        g_ssd_norm=g_ssd_norm[o].reshape(1, SSD_D_INNER),
        w_conv_lru=w_conv_lru[o], b_conv_lru=b_conv_lru[o].reshape(1, LRU_WIDTH),
        w_lru_r=w_lru_r[o].astype(BF16), b_lru_r=b_lru_r[o].reshape(1, LRU_WIDTH),
        w_lru_i=w_lru_i[o].astype(BF16), b_lru_i=b_lru_i[o].reshape(1, LRU_WIDTH),
        lru_lambda=lru_lambda[o].reshape(1, LRU_WIDTH),
        w_out_ssd=w_out_o[o][:SSD_D_INNER].astype(BF16),
        w_out_lru=w_out_o[o][SSD_D_INNER:].astype(BF16),
    )


def _pad_tab(tab):
    return jnp.pad(tab, ((0, 0), (0, LANES - tab.shape[1])))


def _even_layer_prompt(rows, hn, prm, nb, t):
    u = _matmul(rows, [hn], [prm["w_in"]], name="in_even", tn=EVEN_IN_PAD, w_t=True)
    zero_buf = jnp.zeros((nb, POOL_BUF, POOL_WIDTH), F32)
    y_pool = _pool_prompt(u, zero_buf, prm["w_pool"], prm["s_pool"], nb, t, 0)
    cc, ss = _rope_tables(t)
    tm = 256
    tiles = t // tm
    q, k, v, lat = _mla_prep(u, rows.m, tm, lambda i: i % tiles, (_pad_tab(cc), _pad_tab(ss)), prm, False)
    y_att = _flash_attention(q, k, v, nb, t)
    u3 = u.reshape(nb, t, EVEN_IN_PAD)
    kr = u3[:, :, POOL_WIDTH + MLA_Q_LORA + MLA_KV_LORA:POOL_WIDTH + MLA_Q_LORA + MLA_KV_LORA + MLA_ROPE]
    mla_rows = jnp.concatenate([lat.reshape(nb, t, MLA_KV_LORA), kr], axis=-1)
    pool_new = u3[:, t - POOL_BUF:, :POOL_WIDTH]
    return [y_pool, y_att], [prm["w_out_pool"], prm["w_out_att"]], mla_rows, pool_new


def _even_layer_sample(rows, hn, prm, pool_buf, cache, layer, page_table, pos0):
    m = rows.m
    nh = MLA_HEADS
    u = _matmul(rows, [hn], [prm["w_in"]], name="in_even", tn=640, w_t=True)
    y_pool = _pool_step(u, jnp.transpose(pool_buf, (1, 0, 2)), prm["w_pool"], prm["s_pool"], pos0)
    cc, ss = _rope_tables(pos0 + 1)
    cc_new = jnp.broadcast_to(_pad_tab(cc[pos0:]), (m, LANES))
    ss_new = jnp.broadcast_to(_pad_tab(ss[pos0:]), (m, LANES))
    q, k, v, lat, qg = _mla_prep(u, m, m, lambda i: 0, (cc_new, ss_new), prm, True)
    qp = _bmm_heads(qg, prm["w_nope_t"], BF16)
    qp = jnp.pad(jnp.transpose(qp, (1, 0, 2)), ((0, 0), (0, DEC_QROWS - nh), (0, 0)))
    q_m = jnp.transpose(q, (1, 0, 2))
    qr = jnp.pad(q_m[:, :, MLA_NOPE:MLA_NOPE + MLA_ROPE], ((0, 0), (0, 0), (0, LANES - MLA_ROPE)))
    ones_row = jnp.concatenate([jnp.zeros((m, 1, MLA_ROPE), BF16), jnp.ones((m, 1, LANES - MLA_ROPE), BF16)], axis=2)
    qr = jnp.concatenate([qr, ones_row, jnp.zeros((m, DEC_QROWS - nh - 1, LANES), BF16)], axis=1)
    o_lat = _mla_decode(cache, layer, page_table, prm["w_nope_t"].reshape(nh * MLA_NOPE, MLA_KV_LORA), qp, qr,
                        prm["g_k_rope"], cc[:pos0].T, ss[:pos0].T, q_m.astype(F32),
                        jnp.transpose(k, (1, 0, 2)).astype(F32), lat.reshape(m, 1, MLA_KV_LORA))
    y_att = _bmm_heads(jnp.transpose(o_lat, (1, 0, 2)).astype(BF16), prm["w_v"], BF16)
    y_att = jnp.transpose(y_att, (1, 0, 2)).reshape(m, nh * MLA_V)
    kr = u[:, POOL_WIDTH + MLA_Q_LORA + MLA_KV_LORA:POOL_WIDTH + MLA_Q_LORA + MLA_KV_LORA + MLA_ROPE]
    mla_rows = jnp.concatenate([lat, kr], axis=-1).reshape(m, 1, MLA_KV_LORA + MLA_ROPE)
    pool_new = jnp.concatenate([pool_buf[:, 1:], u[:, None, :POOL_WIDTH]], axis=1)
    return [y_pool, y_att], [prm["w_out_pool"], prm["w_out_att"]], mla_rows, pool_new


def _odd_layer_prompt(rows, hn, prm, nb, t):
    u = _matmul(rows, [hn], [prm["w_in"]], name="in_odd", tm=1024, tn=1792, w_t=True)
    u_dt = _matmul(rows, [hn], [prm["w_in_dt"]], name="in_odd_dt", tn=LANES, w_t=True)
    k1 = CONV_WIDTH - 1
    y_ssd, h_ssd = _ssd_prompt(u, u_dt, jnp.zeros((nb, k1, SSD_D_INNER), F32), jnp.zeros((nb, k1, SSD_BC), F32),
                               jnp.zeros((nb, SSD_D_INNER, SSD_STATE), F32), prm, nb, t)
    y_lru, h_lru = _lru_prompt(u, jnp.zeros((nb, k1, LRU_WIDTH), F32), jnp.zeros((nb, 1, LRU_WIDTH), F32),
                               prm, nb, t, 0)
    u3 = u.reshape(nb, t, -1)
    sconv = u3[:, t - k1:, SSD_D_INNER:2 * SSD_D_INNER + SSD_BC]
    lconv = u3[:, t - k1:, 2 * SSD_D_INNER + SSD_BC:2 * SSD_D_INNER + SSD_BC + LRU_WIDTH]
    return ([y_ssd, y_lru], [prm["w_out_ssd"], prm["w_out_lru"]], sconv,
            h_ssd.reshape(nb, SSD_HEADS, SSD_HEAD_DIM, SSD_STATE), lconv, h_lru.reshape(nb, LRU_WIDTH))


def _odd_layer_sample(rows, hn, prm, sconv_buf, ssd_state, lconv_buf, lru_state, pos0):
    m = rows.m
    u = _matmul(rows, [hn], [prm["w_in"]], name="in_odd", tn=1024, w_t=True)
    u_dt = _matmul(rows, [hn], [prm["w_in_dt"]], name="in_odd_dt", tn=LANES, w_t=True)
    sconv_t = jnp.transpose(sconv_buf, (1, 0, 2))
    xs, bcs, dtx, dae = _ssd_step_prep(u, u_dt, sconv_t[:, :, :SSD_D_INNER], sconv_t[:, :, SSD_D_INNER:], prm)
    nr = SSD_D_INNER // LANES
    cols = jnp.concatenate([dtx.reshape(m, nr, LANES), dae.reshape(m, nr, LANES)], axis=1)
    gs = SSD_GROUPS * SSD_STATE
    pad8 = lambda a: jnp.pad(a.reshape(m, SSD_GROUPS, SSD_STATE), ((0, 0), (0, 8 - SSD_GROUPS), (0, 0)))
    h_new, y = _ssd_step_state(ssd_state.reshape(m, SSD_D_INNER, SSD_STATE), cols,
                               pad8(bcs[:, :gs]), pad8(bcs[:, gs:]))
    y_ssd = _ssd_step_post(y.reshape(m, SSD_D_INNER), xs, u, prm)
    y_lru, h_lru = _lru_step(u, jnp.transpose(lconv_buf, (1, 0, 2)), lru_state, prm, pos0)
    u_xbc = u[:, SSD_D_INNER:2 * SSD_D_INNER + SSD_BC]
    u_lx = u[:, 2 * SSD_D_INNER + SSD_BC:2 * SSD_D_INNER + SSD_BC + LRU_WIDTH]
    sconv = jnp.concatenate([sconv_buf[:, 1:], u_xbc[:, None]], axis=1)
    lconv = jnp.concatenate([lconv_buf[:, 1:], u_lx[:, None]], axis=1)
    return ([y_ssd, y_lru], [prm["w_out_ssd"], prm["w_out_lru"]], sconv,
            h_new.reshape(m, SSD_HEADS, SSD_HEAD_DIM, SSD_STATE), lconv, h_lru)


def kernel(x_prompt, x_sample, cache_mla, state_pool, state_ssd_conv, state_ssd, state_lru_conv, state_lru, page_table, c_prompt, c_sample, g_norm1, g_norm2, w_mod, b_mod, w_mlp1, w_mlp2, w_in_e, w_pool, s_pool, g_qlat, w_qb, g_kvlat, w_kvb, g_q, g_k, w_out_e, w_in_o, w_conv_ssd, b_conv_ssd, dt_bias, a_log, d_skip, g_ssd_norm, w_conv_lru, b_conv_lru, w_lru_r, b_lru_r, w_lru_i, b_lru_i, lru_lambda, w_out_o):
    nb, t, _ = x_prompt.shape
    ns = x_sample.shape[0]
    assert x_sample.shape[1] == 1 and t >= POOL_BUF
    pos0_s = page_table.shape[1] * PAGE_SIZE

    pad_rows = (-(ns + nb)) % 8
    c_all = jnp.concatenate([c_sample, c_prompt, jnp.zeros((pad_rows, D_MODEL), F32)], axis=0)
    mod_all = _modulation(c_all, w_mod, b_mod)

    rows_p = _Rows(nb, t, 512)
    rows_po = _Rows(nb, t, 256)
    rows_s = _Rows(ns, 1, ns)
    xp = x_prompt.reshape(nb * t, D_MODEL)
    xs = x_sample.reshape(ns, D_MODEL)
    w2 = w_mlp2.astype(BF16)

    outs_p, outs_s = {}, {}
    for layer in range(DEPTH):
        mod_p = rows_p.mod_array(mod_all[layer, ns:ns + nb])
        mod_s = rows_s.mod_array(mod_all[layer, :ns])
        hn_p = _norm_mod(rows_p, xp, g_norm1[layer], mod_p, 1, 0)
        hn_s = _norm_mod(rows_s, xs, g_norm1[layer], mod_s, 1, 0)
        if layer % 2 == 0:
            e = layer // 2
            prm = _even_params(e, w_in_e, w_pool, s_pool, g_qlat, w_qb, g_kvlat, w_kvb, g_q, g_k, w_out_e)
            a_p, w_o, mla_p, pool_p = _even_layer_prompt(rows_p, hn_p, prm, nb, t)
            a_s, _, mla_s, pool_s = _even_layer_sample(rows_s, hn_s, prm, state_pool[e], cache_mla, e,
                                                       page_table, pos0_s)
            outs_p.setdefault("mla", []).append(mla_p)
            outs_p.setdefault("pool", []).append(pool_p)
            outs_s.setdefault("mla", []).append(mla_s)
            outs_s.setdefault("pool", []).append(pool_s)
        else:
            o = layer // 2
            prm = _odd_params(o, w_in_o, w_conv_ssd, b_conv_ssd, dt_bias, a_log, d_skip, g_ssd_norm, w_conv_lru,
                              b_conv_lru, w_lru_r, b_lru_r, w_lru_i, b_lru_i, lru_lambda, w_out_o)
            a_p, w_o, sconv_p, ssd_p, lconv_p, lru_p = _odd_layer_prompt(rows_p, hn_p, prm, nb, t)
            a_s, _, sconv_s, ssd_s, lconv_s, lru_s = _odd_layer_sample(
                rows_s, hn_s, prm, state_ssd_conv[o], state_ssd[o], state_lru_conv[o], state_lru[o], pos0_s)
            for d, vals in ((outs_p, (sconv_p, ssd_p, lconv_p, lru_p)), (outs_s, (sconv_s, ssd_s, lconv_s, lru_s))):
                for name, val in zip(("sconv", "ssd", "lconv", "lru"), vals):
                    d.setdefault(name, []).append(val)
        xp, hn2_p = _matmul(rows_po, a_p, w_o, name="out_proj", tn=D_MODEL, res=xp, mod=mod_p, gate_chunk=2,
                            norm=(g_norm2[layer], 4, 3))
        xs, hn2_s = _matmul(rows_s, a_s, w_o, name="out_proj_s", tn=D_MODEL, res=xs, mod=mod_s, gate_chunk=2,
                            norm=(g_norm2[layer], 4, 3))
        act_p = _matmul_ws(hn2_p, w_mlp1, layer, name="mlp1", tm=1024, tn=1024, act="relu2", out_dtype=BF16)
        act_s = _matmul(rows_s, [hn2_s], [(w_mlp1, layer)], name="mlp1_s", tn=1024, act="relu2", out_dtype=BF16)
        xp = _matmul(rows_p, [act_p], [(w2, layer)], name="mlp2", tn=512, res=xp, mod=mod_p, gate_chunk=5)
        xs = _matmul(rows_s, [act_s], [(w_mlp2, layer)], name="mlp2_s", tn=1024, tk=2048, res=xs, mod=mod_s,
                     gate_chunk=5)

    st = lambda d, name: jnp.stack(d[name])
    return (xp.reshape(nb, t, D_MODEL), xs.reshape(ns, 1, D_MODEL),
            st(outs_p, "mla"), st(outs_s, "mla"), st(outs_p, "pool"), st(outs_s, "pool"),
            st(outs_p, "sconv"), st(outs_s, "sconv"), st(outs_p, "ssd"), st(outs_s, "ssd"),
            st(outs_p, "lconv"), st(outs_s, "lconv"), st(outs_p, "lru"), st(outs_s, "lru"))
```

```python
import functools

import numpy as np
import jax
import jax.numpy as jnp
from jax import lax
from jax.experimental import pallas as pl
from jax.experimental.pallas import tpu as pltpu

F32 = jnp.float32
BF16 = jnp.bfloat16

VMEM_LIMIT_BYTES = 48 * 1024 * 1024
LANES = 128

D_MODEL = 2048
EPS = 1e-6
N_MOD = 6
DEPTH = 2

POOL_WIDTH = 1024
POOL_WINDOWS = (2, 4, 8, 16)
POOL_GROUP_DIM = 256
POOL_BUF = 15

MLA_HEADS = 8
MLA_NOPE = 128
MLA_ROPE = 64
MLA_V = 128
MLA_QK = 192
MLA_QKP = 256
MLA_Q_LORA = 512
MLA_KV_LORA = 256
ROPE_THETA = 10000.0
PAGE_SIZE = 128
EVEN_IN_PAD = 1920

SSD_D_INNER = 2048
SSD_HEAD_DIM = 64
SSD_HEADS = 32
SSD_GROUPS = 4
SSD_STATE = 128
SSD_CHUNK = 128
SSD_BC = 1024
CONV_WIDTH = 4

LRU_WIDTH = 1024
LRU_BLOCKS = 8
LRU_BLOCK_DIM = 128
LRU_C = 8.0

MLP_HIDDEN = 8192


def _cparams(*sem):
    return pltpu.CompilerParams(dimension_semantics=sem, vmem_limit_bytes=VMEM_LIMIT_BYTES)


def _sigmoid(x):
    return 1.0 / (1.0 + jnp.exp(-x))


def _silu(x):
    return x * _sigmoid(x)


def _softplus(x):
    return jnp.maximum(x, 0.0) + jnp.log1p(jnp.exp(-jnp.abs(x)))


def _gelu_tanh(x):
    return 0.5 * x * (1.0 + jnp.tanh(0.7978845608028654 * (x + 0.044715 * (x * x * x))))


def _dot(a, b):
    return jnp.dot(a, b, preferred_element_type=F32)


def _dot_nt(a, b):
    return lax.dot_general(a, b, (((1,), (1,)), ((), ())), preferred_element_type=F32)


def _split3(x):
    hi = x.astype(BF16)
    r = x - hi.astype(F32)
    mid = r.astype(BF16)
    lo = (r - mid.astype(F32)).astype(BF16)
    return hi, mid, lo


def _dot_sel(x, sel):
    hi, mid, lo = _split3(x)
    return _dot(hi, sel) + _dot(mid, sel) + _dot(lo, sel)


def _mod_kernel(c_ref, w_ref, b_ref, o_ref):
    c = c_ref[...]
    o_ref[...] = _dot(_silu(c).astype(BF16), w_ref[...].astype(BF16)) + b_ref[...]


def _modulation(c_all, w_mod, b_mod):
    mp = c_all.shape[0]
    n = N_MOD * D_MODEL
    tn = 1024
    return pl.pallas_call(
        _mod_kernel, name="modulation",
        grid=(DEPTH, n // tn),
        in_specs=[pl.BlockSpec((mp, D_MODEL), lambda l, j: (0, 0)),
                  pl.BlockSpec((None, D_MODEL, tn), lambda l, j: (l, 0, j)),
                  pl.BlockSpec((None, 1, tn), lambda l, j: (l, 0, j))],
        out_specs=pl.BlockSpec((None, mp, tn), lambda l, j: (l, 0, j)),
        out_shape=jax.ShapeDtypeStruct((DEPTH, mp, n), F32),
        compiler_params=_cparams("parallel", "parallel"),
    )(c_all, w_mod, b_mod.reshape(DEPTH, 1, n))


class _Rows:
    def __init__(self, nb, t, tm):
        self.nb, self.t, self.tm = nb, t, tm
        self.m = nb * t
        self.per_seq = t > 1
        if self.per_seq:
            assert t % tm == 0
            self.tiles_per_seq = t // tm

    def mod_array(self, mod_rows):
        return mod_rows.reshape(self.nb, 1, -1) if self.per_seq else mod_rows

    def mod_spec(self, chunk, width, ncol, col_of):
        per_chunk = D_MODEL // width
        if self.per_seq:
            tps = self.tiles_per_seq
            return pl.BlockSpec((None, 1, width),
                                lambda *g: (g[0] // tps, 0, chunk * per_chunk + col_of(*g)))
        return pl.BlockSpec((self.tm, width), lambda *g: (g[0], chunk * per_chunk + col_of(*g)))


def _norm_mod_kernel(x_ref, g_ref, sc_ref, sh_ref, o_ref):
    x = x_ref[...]
    y = x * lax.rsqrt(jnp.mean(x * x, axis=-1, keepdims=True) + EPS) * g_ref[...]
    o_ref[...] = (y * (1.0 + sc_ref[...]) + sh_ref[...]).astype(o_ref.dtype)


def _norm_mod(rows, x, g, mod, sc_chunk, sh_chunk):
    tm = min(rows.tm, 512)
    r = _Rows(rows.nb, rows.t, tm)
    return pl.pallas_call(
        _norm_mod_kernel, name="norm_mod",
        grid=(r.m // tm,),
        in_specs=[pl.BlockSpec((tm, D_MODEL), lambda i: (i, 0)),
                  pl.BlockSpec((1, D_MODEL), lambda i: (0, 0)),
                  r.mod_spec(sc_chunk, D_MODEL, 1, lambda i: 0),
                  r.mod_spec(sh_chunk, D_MODEL, 1, lambda i: 0)],
        out_specs=pl.BlockSpec((tm, D_MODEL), lambda i: (i, 0)),
        out_shape=jax.ShapeDtypeStruct((r.m, D_MODEL), BF16),
        compiler_params=_cparams("parallel"),
    )(x, g.reshape(1, D_MODEL), mod, mod)


def _mm_kernel(*refs, n_a, nk, act, has_res, w_t, has_norm=False):
    a_refs = refs[:n_a]
    w_refs = refs[n_a:2 * n_a]
    pos = 2 * n_a
    if has_res:
        x_ref, gt_ref = refs[pos], refs[pos + 1]
        pos += 2
    if has_norm:
        g_ref, sc_ref, sh_ref = refs[pos:pos + 3]
        pos += 3
    o_ref = refs[pos]
    pos += 1
    if has_norm:
        hn_ref = refs[pos]
        pos += 1
    acc_ref = refs[pos] if nk > 1 else None

    dot = _dot_nt if w_t else _dot
    part = dot(a_refs[0][...], w_refs[0][...].astype(BF16))
    for a_ref, w_ref in zip(a_refs[1:], w_refs[1:]):
        part = part + dot(a_ref[...], w_ref[...].astype(BF16))

    def finish(acc):
        if act == "relu2":
            acc = jnp.square(jnp.maximum(acc, 0.0))
        if has_res:
            acc = x_ref[...] + gt_ref[...] * acc
        o_ref[...] = acc.astype(o_ref.dtype)
        if has_norm:
            y = acc * lax.rsqrt(jnp.mean(acc * acc, axis=-1, keepdims=True) + EPS) * g_ref[...]
            hn_ref[...] = (y * (1.0 + sc_ref[...]) + sh_ref[...]).astype(hn_ref.dtype)

    if nk == 1:
        finish(part)
    else:
        k = pl.program_id(2)

        @pl.when(k == 0)
        def _():
            acc_ref[...] = part

        @pl.when(k > 0)
        def _():
            acc_ref[...] += part

        @pl.when(k == nk - 1)
        def _():
            finish(acc_ref[...])


def _matmul(rows, a_list, w_list, *, name, tn, tk=None, act=None, res=None, mod=None, gate_chunk=None,
            out_dtype=F32, tm=None, w_t=False, norm=None):
    assert tm is None or res is None
    tm, m = (tm or rows.tm), rows.m
    n_a = len(a_list)
    if w_t:
        assert n_a == 1 and tk is None and not isinstance(w_list[0], tuple)
        n, kfull = w_list[0].shape
        return pl.pallas_call(
            functools.partial(_mm_kernel, n_a=1, nk=1, act=act, has_res=False, w_t=True), name=name,
            grid=(m // tm, n // tn, 1),
            in_specs=[pl.BlockSpec((tm, kfull), lambda i, j, k: (i, 0)),
                      pl.BlockSpec((tn, kfull), lambda i, j, k: (j, 0))],
            out_specs=pl.BlockSpec((tm, tn), lambda i, j, k: (i, j)),
            out_shape=jax.ShapeDtypeStruct((m, n), out_dtype),
            compiler_params=_cparams("parallel", "parallel", "arbitrary"),
        )(a_list[0], w_list[0])
    n = (w_list[0][0] if isinstance(w_list[0], tuple) else w_list[0]).shape[-1]
    if tk is None or n_a > 1:
        nk = 1
    else:
        assert a_list[0].shape[1] % tk == 0
        nk = a_list[0].shape[1] // tk
    assert n % tn == 0 and m % tm == 0
    in_specs, args = [], []
    for a in a_list:
        kk = a.shape[1] if nk == 1 else tk
        in_specs.append(pl.BlockSpec((tm, kk), lambda i, j, k: (i, k)))
        args.append(a)
    for w in w_list:
        if isinstance(w, tuple):
            w, layer = w
            kk = w.shape[1] if nk == 1 else tk
            in_specs.append(pl.BlockSpec((None, kk, tn), lambda i, j, k, layer=layer: (layer, k, j)))
        else:
            kk = w.shape[0] if nk == 1 else tk
            in_specs.append(pl.BlockSpec((kk, tn), lambda i, j, k: (k, j)))
        args.append(w)
    has_res = res is not None
    if has_res:
        in_specs.append(pl.BlockSpec((tm, tn), lambda i, j, k: (i, j)))
        in_specs.append(rows.mod_spec(gate_chunk, tn, n // tn, lambda i, j, k: j))
        args += [res, mod]
    scratch = [pltpu.VMEM((tm, tn), F32)] if nk > 1 else []
    out_specs = [pl.BlockSpec((tm, tn), lambda i, j, k: (i, j))]
    out_shape = [jax.ShapeDtypeStruct((m, n), out_dtype)]
    if norm is not None:
        gain, sc_chunk, sh_chunk = norm
        assert tn == n and has_res
        in_specs += [pl.BlockSpec((1, n), lambda i, j, k: (0, 0)),
                     rows.mod_spec(sc_chunk, n, 1, lambda i, j, k: 0),
                     rows.mod_spec(sh_chunk, n, 1, lambda i, j, k: 0)]
        args += [gain.reshape(1, n), mod, mod]
        out_specs.append(pl.BlockSpec((tm, n), lambda i, j, k: (i, 0)))
        out_shape.append(jax.ShapeDtypeStruct((m, n), BF16))
    outs = pl.pallas_call(
        functools.partial(_mm_kernel, n_a=n_a, nk=nk, act=act, has_res=has_res, w_t=False,
                          has_norm=norm is not None), name=name,
        grid=(m // tm, n // tn, nk),
        in_specs=in_specs,
        out_specs=out_specs,
        out_shape=out_shape,
        scratch_shapes=scratch,
        compiler_params=_cparams("parallel", "parallel", "arbitrary"),
    )(*args)
    return outs if norm is not None else outs[0]


def _norm_mm_nt_kernel(x_ref, g_ref, sc_ref, sh_ref, w_ref, o_ref):
    x = x_ref[...]
    y = x * lax.rsqrt(jnp.mean(x * x, axis=-1, keepdims=True) + EPS) * g_ref[...]
    hn = (y * (1.0 + sc_ref[...]) + sh_ref[...]).astype(BF16)
    o_ref[...] = _dot_nt(hn, w_ref[...]).astype(o_ref.dtype)


def _norm_matmul_nt(rows, x, g, mod, sc_chunk, sh_chunk, w_t, *, name, tn):
    tm, m = rows.tm, rows.m
    n, k = w_t.shape
    assert n % tn == 0 and m % tm == 0
    return pl.pallas_call(
        _norm_mm_nt_kernel, name=name,
        grid=(m // tm, n // tn),
        in_specs=[pl.BlockSpec((tm, k), lambda i, j: (i, 0)),
                  pl.BlockSpec((1, k), lambda i, j: (0, 0)),
                  rows.mod_spec(sc_chunk, k, 1, lambda i, j: 0),
                  rows.mod_spec(sh_chunk, k, 1, lambda i, j: 0),
                  pl.BlockSpec((tn, k), lambda i, j: (j, 0))],
        out_specs=pl.BlockSpec((tm, tn), lambda i, j: (i, j)),
        out_shape=jax.ShapeDtypeStruct((m, n), F32),
        compiler_params=_cparams("parallel", "parallel"),
    )(x, g.reshape(1, k), mod, mod, w_t)


def _mm_ws_kernel(a_ref, w_ref, o_ref, wb_ref, *, act):
    @pl.when(pl.program_id(1) == 0)
    def _():
        wb_ref[...] = w_ref[...].astype(BF16)

    acc = _dot(a_ref[...], wb_ref[...])
    if act == "relu2":
        acc = jnp.square(jnp.maximum(acc, 0.0))
    o_ref[...] = acc.astype(o_ref.dtype)


def _matmul_ws(a, w_stack, layer, *, name, tm, tn, act=None, out_dtype=F32):
    m, k = a.shape
    n = w_stack.shape[-1]
    assert m % tm == 0 and n % tn == 0
    return pl.pallas_call(
        functools.partial(_mm_ws_kernel, act=act), name=name,
        grid=(n // tn, m // tm),
        in_specs=[pl.BlockSpec((tm, k), lambda j, i: (i, 0)),
                  pl.BlockSpec((None, k, tn), lambda j, i: (layer, 0, j))],
        out_specs=pl.BlockSpec((tm, tn), lambda j, i: (i, j)),
        out_shape=jax.ShapeDtypeStruct((m, n), out_dtype),
        scratch_shapes=[pltpu.VMEM((k, tn), BF16)],
        compiler_params=_cparams("parallel", "arbitrary"),
    )(a, w_stack)


def _bmm_kernel(a_ref, w_ref, o_ref):
    o_ref[...] = _dot(a_ref[...], w_ref[...]).astype(o_ref.dtype)


def _bmm_heads(a, w, out_dtype):
    h, m, k = a.shape
    n = w.shape[2]
    return pl.pallas_call(
        _bmm_kernel, name="bmm_heads",
        grid=(h,),
        in_specs=[pl.BlockSpec((None, m, k), lambda i: (i, 0, 0)),
                  pl.BlockSpec((None, k, n), lambda i: (i, 0, 0))],
        out_specs=pl.BlockSpec((None, m, n), lambda i: (i, 0, 0)),
        out_shape=jax.ShapeDtypeStruct((h, m, n), out_dtype),
        compiler_params=_cparams("parallel"),
    )(a, w)


CONV_HALO = 8


def _conv_chunk(ext_ref, u, buf_ref, w_ref, b_ref, first):
    tc = u.shape[0]
    k1 = CONV_WIDTH - 1

    @pl.when(first)
    def _():
        ext_ref[CONV_HALO - k1:CONV_HALO, :] = buf_ref[...]

    ext_ref[CONV_HALO:CONV_HALO + tc, :] = u
    y = b_ref[...] + w_ref[k1:k1 + 1, :] * u
    for k in range(k1):
        y = y + w_ref[k:k + 1, :] * ext_ref[CONV_HALO - k1 + k:CONV_HALO - k1 + k + tc, :]
    ext_ref[CONV_HALO - k1:CONV_HALO, :] = ext_ref[CONV_HALO + tc - k1:CONV_HALO + tc, :]
    return y


def _conv_step(buf_ref, u, w_ref, b_ref):
    y = b_ref[...] + w_ref[CONV_WIDTH - 1:CONV_WIDTH, :] * u
    for k in range(CONV_WIDTH - 1):
        y = y + w_ref[k:k + 1, :] * buf_ref[k]
    return y


def _pool_prompt_kernel(u_ref, buf_ref, w_ref, s_ref, o_ref, ext_ref, *, tc, pos0):
    c = pl.program_id(1)
    halo = 16

    @pl.when(c == 0)
    def _():
        ext_ref[0:1, :] = jnp.zeros((1, POOL_WIDTH), F32)
        ext_ref[1:halo, :] = buf_ref[...]

    u = u_ref[...]
    ext_ref[halo:halo + tc, :] = u
    pos = (pos0 + c * tc + lax.broadcasted_iota(jnp.int32, (tc, 1), 0)).astype(F32)
    outs = []
    for g, w in enumerate(POOL_WINDOWS):
        lo, hi = g * POOL_GROUP_DIM, (g + 1) * POOL_GROUP_DIM
        x = u[:, lo:hi]
        wsum = x
        for j in range(1, w):
            wsum = wsum + ext_ref[halo - j:halo - j + tc, lo:hi]
        cnt = jnp.minimum(pos + 1.0, float(w))
        d = (wsum / cnt - x).astype(BF16)
        outs.append(_dot(d, w_ref[g]))
    y = jnp.concatenate(outs, axis=1) * s_ref[...]
    o_ref[...] = y.astype(o_ref.dtype)
    ext_ref[0:halo, :] = ext_ref[tc:tc + halo, :]


def _pool_prompt(u, buf, w_pool, s_pool, nb, t, pos0):
    tc = 256
    nc = t // tc
    return pl.pallas_call(
        functools.partial(_pool_prompt_kernel, tc=tc, pos0=pos0), name="pool_prompt",
        grid=(nb, nc),
        in_specs=[pl.BlockSpec((tc, POOL_WIDTH), lambda b, c: (b * nc + c, 0)),
                  pl.BlockSpec((None, POOL_BUF, POOL_WIDTH), lambda b, c: (b, 0, 0)),
                  pl.BlockSpec((len(POOL_WINDOWS), POOL_GROUP_DIM, POOL_GROUP_DIM), lambda b, c: (0, 0, 0)),
                  pl.BlockSpec((1, POOL_WIDTH), lambda b, c: (0, 0))],
        out_specs=pl.BlockSpec((tc, POOL_WIDTH), lambda b, c: (b * nc + c, 0)),
        out_shape=jax.ShapeDtypeStruct((nb * t, POOL_WIDTH), BF16),
        scratch_shapes=[pltpu.VMEM((16 + tc, POOL_WIDTH), F32)],
        compiler_params=_cparams("parallel", "arbitrary"),
    )(u, buf, w_pool, s_pool)


def _pool_step_kernel(u_ref, buf_ref, w_ref, s_ref, o_ref, *, pos0):
    outs = []
    for g, w in enumerate(POOL_WINDOWS):
        lo, hi = g * POOL_GROUP_DIM, (g + 1) * POOL_GROUP_DIM
        x = u_ref[:, lo:hi]
        wsum = x
        for j in range(1, w):
            wsum = wsum + buf_ref[POOL_BUF - j, :, lo:hi]
        cnt = min(pos0 + 1.0, float(w))
        d = (wsum / cnt - x).astype(BF16)
        outs.append(_dot(d, w_ref[g]))
    o_ref[...] = (jnp.concatenate(outs, axis=1) * s_ref[...]).astype(o_ref.dtype)


def _pool_step(u, buf_t, w_pool, s_pool, pos0):
    m = buf_t.shape[1]
    return pl.pallas_call(
        functools.partial(_pool_step_kernel, pos0=pos0), name="pool_step",
        grid=(1,),
        in_specs=[pl.BlockSpec((m, POOL_WIDTH), lambda i: (0, 0)),
                  pl.BlockSpec((POOL_BUF, m, POOL_WIDTH), lambda i: (0, 0, 0)),
                  pl.BlockSpec((len(POOL_WINDOWS), POOL_GROUP_DIM, POOL_GROUP_DIM), lambda i: (0, 0, 0)),
                  pl.BlockSpec((1, POOL_WIDTH), lambda i: (0, 0))],
        out_specs=pl.BlockSpec((m, POOL_WIDTH), lambda i: (0, 0)),
        out_shape=jax.ShapeDtypeStruct((m, POOL_WIDTH), BF16),
        compiler_params=_cparams("arbitrary"),
    )(u, buf_t, w_pool, s_pool)


def _rope128(x, cc, ss):
    lane = lax.broadcasted_iota(jnp.int32, x.shape, 1)
    swapped = jnp.where(lane < MLA_ROPE // 2, pltpu.roll(x, LANES - MLA_ROPE // 2, 1),
                        pltpu.roll(x, MLA_ROPE // 2, 1))
    return x * cc + swapped * ss


def _mla_prep_kernel(uq_ref, ukv_ref, ukr_ref, gql_ref, gkvl_ref, wq_ref, wkv_ref, gq_ref, gk_ref,
                     cc_ref, ss_ref, *out_refs, emit_qg):
    if emit_qg:
        q_ref, k_ref, v_ref, lat_ref, qg_ref = out_refs
    else:
        q_ref, k_ref, v_ref, lat_ref = out_refs
    cc = cc_ref[...]
    ss = ss_ref[...]
    scale = MLA_QK ** -0.5

    uq = uq_ref[...]
    qn = uq * lax.rsqrt(jnp.mean(uq * uq, axis=-1, keepdims=True) + EPS) * gql_ref[...]
    qall = _dot(qn.astype(BF16), wq_ref[...])
    ukv = ukv_ref[...]
    lat = ukv * lax.rsqrt(jnp.mean(ukv * ukv, axis=-1, keepdims=True) + EPS) * gkvl_ref[...]
    lat_ref[...] = lat
    kvall = _dot(lat.astype(BF16), wkv_ref[...])
    kr = ukr_ref[...]
    kr_ss = jnp.sum(kr * kr, axis=-1, keepdims=True)

    gq_n, gq_r = gq_ref[:, :MLA_NOPE], gq_ref[:, MLA_NOPE:]
    gk_n, gk_r = gk_ref[:, :MLA_NOPE], gk_ref[:, MLA_NOPE:]
    nh = MLA_HEADS
    for h in range(nh):
        qnope = qall[:, h * 128:(h + 1) * 128]
        qrope = qall[:, (nh + h) * 128:(nh + h + 1) * 128]
        ssq = jnp.sum(qnope * qnope, axis=-1, keepdims=True) + jnp.sum(qrope * qrope, axis=-1, keepdims=True)
        rs = lax.rsqrt(ssq * (1.0 / MLA_QK) + EPS) * scale
        qn_h = qnope * rs * gq_n
        qr_h = _rope128(qrope * rs * gq_r, cc, ss)
        q_ref[h] = jnp.concatenate([qn_h, qr_h], axis=1).astype(q_ref.dtype)
        if emit_qg:
            qg_ref[h] = (qn_h * gk_n).astype(qg_ref.dtype)

        knope = kvall[:, h * 256:h * 256 + 128]
        ssk = jnp.sum(knope * knope, axis=-1, keepdims=True) + kr_ss
        rsk = lax.rsqrt(ssk * (1.0 / MLA_QK) + EPS)
        kr_h = _rope128(kr * rsk * gk_r, cc, ss)
        k_ref[h] = jnp.concatenate([knope * rsk * gk_n, kr_h], axis=1).astype(k_ref.dtype)
        v_ref[h] = kvall[:, h * 256 + 128:(h + 1) * 256].astype(v_ref.dtype)


def _mla_prep(u, m, tm, pos_of_tile, tabs, prm, emit_qg):
    cc_tab, ss_tab = tabs
    nh = MLA_HEADS
    out_shape = [jax.ShapeDtypeStruct((nh, m, MLA_QKP), BF16),
                 jax.ShapeDtypeStruct((nh, m, MLA_QKP), BF16),
                 jax.ShapeDtypeStruct((nh, m, MLA_V), BF16),
                 jax.ShapeDtypeStruct((m, MLA_KV_LORA), F32)]
    out_specs = [pl.BlockSpec((nh, tm, MLA_QKP), lambda i: (0, i, 0)),
                 pl.BlockSpec((nh, tm, MLA_QKP), lambda i: (0, i, 0)),
                 pl.BlockSpec((nh, tm, MLA_V), lambda i: (0, i, 0)),
                 pl.BlockSpec((tm, MLA_KV_LORA), lambda i: (i, 0))]
    if emit_qg:
        out_shape.append(jax.ShapeDtypeStruct((nh, m, MLA_NOPE), BF16))
        out_specs.append(pl.BlockSpec((nh, tm, MLA_NOPE), lambda i: (0, i, 0)))
    full = lambda a: pl.BlockSpec(a.shape, lambda i: (0,) * a.ndim)
    return pl.pallas_call(
        functools.partial(_mla_prep_kernel, emit_qg=emit_qg), name="mla_prep",
        grid=(m // tm,),
        in_specs=[pl.BlockSpec((tm, MLA_Q_LORA), lambda i: (i, POOL_WIDTH // MLA_Q_LORA)),
                  pl.BlockSpec((tm, MLA_KV_LORA), lambda i: (i, (POOL_WIDTH + MLA_Q_LORA) // MLA_KV_LORA)),
                  pl.BlockSpec((tm, LANES), lambda i: (i, (POOL_WIDTH + MLA_Q_LORA + MLA_KV_LORA) // LANES)),
                  full(prm["g_qlat"]), full(prm["g_kvlat"]), full(prm["w_qb"]), full(prm["w_kvb"]),
                  full(prm["g_q"]), full(prm["g_k"]),
                  pl.BlockSpec((tm, LANES), lambda i: (pos_of_tile(i), 0)),
                  pl.BlockSpec((tm, LANES), lambda i: (pos_of_tile(i), 0))],
        out_specs=out_specs,
        out_shape=out_shape,
        compiler_params=_cparams("parallel"),
    )(u, u, u, prm["g_qlat"], prm["g_kvlat"], prm["w_qb"], prm["w_kvb"], prm["g_q"], prm["g_k"],
      cc_tab, ss_tab)


def _flash_kernel(q_ref, k_ref, v_ref, o_ref, *, tq, tk):
    i = pl.program_id(2)
    q = q_ref[...]
    per_q = tq // tk

    def step(j, carry, band):
        m, l, acc = carry
        start = pl.multiple_of(j * tk, tk)
        kb = k_ref[pl.ds(start, tk), :]
        vb = v_ref[pl.ds(start, tk), :]
        s = _dot_nt(q, kb)
        if band is not None:
            r = lax.broadcasted_iota(jnp.int32, (tq, tk), 0)
            c = lax.broadcasted_iota(jnp.int32, (tq, tk), 1) + band * tk
            s = jnp.where(c <= r, s, -jnp.inf)
        m_new = jnp.maximum(m, jnp.max(s, axis=1, keepdims=True))
        alpha = jnp.exp(m - m_new)
        p = jnp.exp(s - m_new)
        l = alpha * l + jnp.sum(p, axis=1, keepdims=True)
        acc = alpha * acc + _dot(p.astype(BF16), vb)
        return m_new, l, acc

    carry = (jnp.full((tq, 1), -jnp.inf, F32), jnp.zeros((tq, 1), F32), jnp.zeros((tq, MLA_V), F32))
    carry = lax.fori_loop(0, i * per_q, lambda j, c: step(j, c, None), carry)
    for band in range(per_q):
        carry = step(i * per_q + band, carry, band)
    _, l, acc = carry
    o_ref[...] = (acc / l).astype(o_ref.dtype)


FLASH_TQ = 1024
FLASH_TK = 1024


def _flash_attention(q, k, v, nb, t):
    tq, tk = FLASH_TQ, FLASH_TK
    nq = t // tq
    nh = MLA_HEADS
    return pl.pallas_call(
        functools.partial(_flash_kernel, tq=tq, tk=tk), name="flash_prompt",
        grid=(nb, nh, nq),
        in_specs=[pl.BlockSpec((None, tq, MLA_QKP), lambda b, h, i: (h, b * nq + i, 0)),
                  pl.BlockSpec((None, t, MLA_QKP), lambda b, h, i: (h, b, 0)),
                  pl.BlockSpec((None, t, MLA_V), lambda b, h, i: (h, b, 0))],
        out_specs=pl.BlockSpec((tq, MLA_V), lambda b, h, i: (b * nq + i, h)),
        out_shape=jax.ShapeDtypeStruct((nb * t, nh * MLA_V), BF16),
        compiler_params=_cparams("parallel", "parallel", "arbitrary"),
    )(q, k, v)


DEC_CHUNK_PAGES = 8
DEC_SUB_PAGES = 8
DEC_QROWS = 16


def _page_copy(pt_ref, cache_ref, buf_ref, sem_ref, layer, b, i, slot):
    return pltpu.make_async_copy(cache_ref.at[layer, pt_ref[b, i]], buf_ref.at[slot, i], sem_ref.at[slot])


def _decode_kernel(pt_ref, cache_ref, wt_ref, qp_ref, qr_ref, gr_ref, cc_ref, ss_ref, qf_ref, kn_ref, latn_ref,
                   o_ref, buf_ref, sem_ref, lhs_ref, *latb_refs, n_pages, nsamp, layer):
    b = pl.program_id(0)
    nh = MLA_HEADS
    slot = lax.rem(b, 2)
    nw = nh * MLA_NOPE

    def fetch(bb, sl):
        for i in range(n_pages):
            _page_copy(pt_ref, cache_ref, buf_ref, sem_ref, layer, bb, i, sl).start()

    @pl.when(b == 0)
    def _():
        fetch(b, slot)
        lhs_ref[0:nw, :] = wt_ref[...]

    @pl.when(b + 1 < nsamp)
    def _():
        fetch(b + 1, 1 - slot)

    for i in range(n_pages):
        _page_copy(pt_ref, cache_ref, buf_ref, sem_ref, layer, b, i, slot).wait()
    pages = [buf_ref.at[slot, i] for i in range(n_pages)]

    lhs_ref[nw:nw + DEC_QROWS, :] = qp_ref[...]
    lhs = lhs_ref[...]
    gr = gr_ref[...]
    half = MLA_ROPE // 2
    sub = DEC_SUB_PAGES * PAGE_SIZE
    chunk = DEC_CHUNK_PAGES * PAGE_SIZE
    qr = qr_ref[...]

    def tile_scores(c, t):
        i0 = c * DEC_CHUNK_PAGES + t * DEC_SUB_PAGES
        k0 = i0 * PAGE_SIZE
        pg = pages[i0:i0 + DEC_SUB_PAGES]
        latb = jnp.concatenate([p[0:MLA_KV_LORA, :] for p in pg], axis=1).astype(BF16)
        kr = jnp.concatenate([p[MLA_KV_LORA:MLA_KV_LORA + MLA_ROPE, :] for p in pg], axis=1)
        latb_refs[c][:, t * sub:(t + 1) * sub] = latb
        kt = _dot(lhs, latb)
        ssn = jnp.concatenate(
            [jnp.sum(jnp.square(kt[h * 128:(h + 1) * 128, :]), axis=0, keepdims=True) for h in range(nh)],
            axis=0)
        sn = kt[nw:nw + nh, :]
        krg = kr * gr
        swapped = jnp.concatenate([krg[half:, :], krg[:half, :]], axis=0)
        rot = krg * cc_ref[:, k0:k0 + sub] + swapped * ss_ref[:, k0:k0 + sub]
        feat = jnp.concatenate([rot, kr * kr], axis=0).astype(BF16)
        rr = _dot(qr, feat)
        rs = lax.rsqrt((ssn + rr[nh:nh + 1]) * (1.0 / MLA_QK) + EPS)
        return rs * (sn + rr[:nh])

    def chunk_update(c, s, state):
        m_run, l_run, acc = state
        m_new = jnp.maximum(m_run, jnp.max(s, axis=1, keepdims=True))
        alpha = jnp.exp(m_run - m_new)
        p = jnp.exp(s - m_new)
        l_run = alpha * l_run + jnp.sum(p, axis=1, keepdims=True)
        acc = alpha * acc + _dot_nt(p.astype(BF16), latb_refs[c][...])
        return m_new, l_run, acc

    n_chunks = n_pages // DEC_CHUNK_PAGES
    n_tiles = DEC_CHUNK_PAGES // DEC_SUB_PAGES
    state = (jnp.full((nh, 1), -jnp.inf, F32), jnp.zeros((nh, 1), F32), jnp.zeros((nh, MLA_KV_LORA), F32))
    s_prev = None
    for c in range(n_chunks):
        s_cur = jnp.concatenate([tile_scores(c, t) for t in range(n_tiles)], axis=1)
        if s_prev is not None:
            state = chunk_update(c - 1, s_prev, state)
        s_prev = s_cur
    m_run, l_run, acc = chunk_update(n_chunks - 1, s_prev, state)

    s_new = jnp.sum(qf_ref[...] * kn_ref[...], axis=1, keepdims=True)
    m_fin = jnp.maximum(m_run, s_new)
    a_fin = jnp.exp(m_run - m_fin)
    p_new = jnp.exp(s_new - m_fin)
    l_fin = a_fin * l_run + p_new
    lat_new = latn_ref[...].astype(BF16).astype(F32)
    o_ref[...] = (a_fin * acc + p_new.astype(BF16).astype(F32) * lat_new) / l_fin


def _mla_decode(cache, layer, page_table, wt_nope, qp, qr, g_rope, cc_keys, ss_keys, qf, kn, lat_new):
    nsamp, n_pages = page_table.shape
    assert n_pages % DEC_CHUNK_PAGES == 0 and DEC_CHUNK_PAGES % DEC_SUB_PAGES == 0
    n_keys = n_pages * PAGE_SIZE
    kvd = cache.shape[-1]
    cache = jnp.transpose(cache, (0, 1, 3, 2))
    nh = MLA_HEADS
    grid_spec = pltpu.PrefetchScalarGridSpec(
        num_scalar_prefetch=1,
        grid=(nsamp,),
        in_specs=[
            pl.BlockSpec(memory_space=pl.ANY),
            pl.BlockSpec(wt_nope.shape, lambda b, pt: (0, 0)),
            pl.BlockSpec((None, DEC_QROWS, MLA_KV_LORA), lambda b, pt: (b, 0, 0)),
            pl.BlockSpec((None, DEC_QROWS, LANES), lambda b, pt: (b, 0, 0)),
            pl.BlockSpec((MLA_ROPE, 1), lambda b, pt: (0, 0)),
            pl.BlockSpec((MLA_ROPE, n_keys), lambda b, pt: (0, 0)),
            pl.BlockSpec((MLA_ROPE, n_keys), lambda b, pt: (0, 0)),
            pl.BlockSpec((None, nh, MLA_QKP), lambda b, pt: (b, 0, 0)),
            pl.BlockSpec((None, nh, MLA_QKP), lambda b, pt: (b, 0, 0)),
            pl.BlockSpec((None, 1, MLA_KV_LORA), lambda b, pt: (b, 0, 0)),
        ],
        out_specs=pl.BlockSpec((None, nh, MLA_KV_LORA), lambda b, pt: (b, 0, 0)),
        scratch_shapes=[pltpu.VMEM((2, n_pages, kvd, PAGE_SIZE), F32), pltpu.SemaphoreType.DMA((2,)),
                        pltpu.VMEM((nh * MLA_NOPE + DEC_QROWS, MLA_KV_LORA), BF16),
                        ] + [pltpu.VMEM((MLA_KV_LORA, DEC_CHUNK_PAGES * PAGE_SIZE), BF16)
                             for _ in range(n_pages // DEC_CHUNK_PAGES)],
    )
    return pl.pallas_call(
        functools.partial(_decode_kernel, n_pages=n_pages, nsamp=nsamp, layer=layer), name="mla_decode",
        grid_spec=grid_spec,
        out_shape=jax.ShapeDtypeStruct((nsamp, nh, MLA_KV_LORA), F32),
        compiler_params=_cparams("arbitrary"),
    )(page_table, cache, wt_nope, qp, qr, g_rope, cc_keys, ss_keys, qf, kn, lat_new)


def _ssd_gate_norm(y, xs, z, dskip, gnorm):
    y = (y + dskip * xs) * _silu(z)
    gw = SSD_D_INNER // SSD_GROUPS
    outs = []
    for g in range(SSD_GROUPS):
        yg = y[:, g * gw:(g + 1) * gw]
        outs.append(yg * lax.rsqrt(jnp.mean(yg * yg, axis=-1, keepdims=True) + EPS))
    return jnp.concatenate(outs, axis=1) * gnorm


def _ssd_prompt_kernel(z_ref, x_ref, bc_ref, dt_ref, bufx_ref, bufbc_ref, wx_ref, bx_ref, wbc_ref, bbc_ref,
                       dtb_ref, alog_ref, dskip_ref, gnorm_ref, h0_ref, tri_ref, exp_ref,
                       y_ref, hout_ref, extx_ref, extbc_ref, ht_ref, *, nc):
    c = pl.program_id(1)
    first = c == 0
    l = SSD_CHUNK
    xs = _silu(_conv_chunk(extx_ref, x_ref[...], bufx_ref, wx_ref, bx_ref, first))
    bcs = _silu(_conv_chunk(extbc_ref, bc_ref[...], bufbc_ref, wbc_ref, bbc_ref, first))
    gs = SSD_GROUPS * SSD_STATE
    bm, cm = bcs[:, :gs], bcs[:, gs:]

    @pl.when(first)
    def _():
        for r in range(SSD_D_INNER // LANES):
            ht_ref[:, r * LANES:(r + 1) * LANES] = h0_ref[r * LANES:(r + 1) * LANES, :].T

    dt = _softplus(dt_ref[...] + dtb_ref[...])
    a = -jnp.exp(alog_ref[...])
    acum = _dot_sel_left(tri_ref[...], dt * a)
    acum_t = acum.T
    dt_t = dt.T
    a_last = acum[l - 1:l, :]
    dend = jnp.exp(a_last - acum)
    sel = exp_ref[...]
    w_state = _dot_sel(dt * dend, sel)
    e_acc = _dot_sel(jnp.exp(acum), sel)
    e_last = e_acc[l - 1:l, :]

    rows = lax.broadcasted_iota(jnp.int32, (l, l), 0)
    cols = lax.broadcasted_iota(jnp.int32, (l, l), 1)
    causal = cols <= rows
    lane = lax.broadcasted_iota(jnp.int32, (l, LANES), 1)
    xs_b = xs.astype(BF16)
    hpg = SSD_HEADS // SSD_GROUPS
    gw = SSD_D_INNER // SSD_GROUPS
    y_groups = []
    for g in range(SSD_GROUPS):
        gcols = slice(g * gw, (g + 1) * gw)
        bg = bm[:, g * SSD_STATE:(g + 1) * SSD_STATE]
        cg_b = cm[:, g * SSD_STATE:(g + 1) * SSD_STATE].astype(BF16)
        cb = _dot_nt(cg_b, bg.astype(BF16))
        ht_g = ht_ref[:, gcols]
        y_off = _dot(cg_b, ht_g.astype(BF16))
        y_pairs = []
        for pr in range(hpg // 2):
            col = (g * hpg + pr * 2) * SSD_HEAD_DIM
            res = []
            for hh in range(2):
                h = g * hpg + pr * 2 + hh
                diff = acum[:, h:h + 1] - acum_t[h:h + 1, :]
                lmat = jnp.exp(jnp.where(causal, diff, -jnp.inf)) * dt_t[h:h + 1, :]
                res.append(_dot((cb * lmat).astype(BF16), xs_b[:, col:col + LANES]))
            y_pairs.append(jnp.where(lane < SSD_HEAD_DIM, res[0], res[1]))
        y_groups.append(jnp.concatenate(y_pairs, axis=1) + e_acc[:, gcols] * y_off)
        xw = (xs[:, gcols] * w_state[:, gcols]).astype(BF16)
        ht_ref[:, gcols] = e_last[:, gcols] * ht_g + _dot(bg.T.astype(BF16), xw)
    y = jnp.concatenate(y_groups, axis=1)
    y_ref[...] = _ssd_gate_norm(y, xs, z_ref[...], dskip_ref[...], gnorm_ref[...]).astype(y_ref.dtype)

    @pl.when(c == nc - 1)
    def _():
        for r in range(SSD_D_INNER // LANES):
            hout_ref[r * LANES:(r + 1) * LANES, :] = ht_ref[:, r * LANES:(r + 1) * LANES].T


def _dot_sel_left(sel, x):
    hi, mid, lo = _split3(x)
    return _dot(sel, hi) + _dot(sel, mid) + _dot(sel, lo)


def _ssd_consts():
    l = SSD_CHUNK
    tri = (jnp.arange(l)[:, None] >= jnp.arange(l)[None, :]).astype(BF16)
    sel = (jnp.arange(LANES)[:, None] == (jnp.arange(SSD_D_INNER)[None, :] // SSD_HEAD_DIM)).astype(BF16)
    return tri, sel


def _ssd_prompt(u, u_dt, bufx, bufbc, h0, prm, nb, t):
    l = SSD_CHUNK
    nc = t // l
    tri, sel = _ssd_consts()
    full = lambda a: pl.BlockSpec(a.shape, lambda b, c: (0,) * a.ndim)
    row = lambda width, colblk: pl.BlockSpec((l, width), lambda b, c: (b * nc + c, colblk))
    y, hout = pl.pallas_call(
        functools.partial(_ssd_prompt_kernel, nc=nc), name="ssd_prompt",
        grid=(nb, nc),
        in_specs=[row(SSD_D_INNER, 0), row(SSD_D_INNER, 1), row(SSD_BC, 4), row(LANES, 0),
                  pl.BlockSpec((None, CONV_WIDTH - 1, SSD_D_INNER), lambda b, c: (b, 0, 0)),
                  pl.BlockSpec((None, CONV_WIDTH - 1, SSD_BC), lambda b, c: (b, 0, 0)),
                  full(prm["w_conv_x"]), full(prm["b_conv_x"]), full(prm["w_conv_bc"]), full(prm["b_conv_bc"]),
                  full(prm["dt_bias"]), full(prm["a_log"]), full(prm["d_skip"]), full(prm["g_ssd_norm"]),
                  pl.BlockSpec((None, SSD_D_INNER, SSD_STATE), lambda b, c: (b, 0, 0)),
                  full(tri), full(sel)],
        out_specs=[pl.BlockSpec((l, SSD_D_INNER), lambda b, c: (b * nc + c, 0)),
                   pl.BlockSpec((None, SSD_D_INNER, SSD_STATE), lambda b, c: (b, 0, 0))],
        out_shape=[jax.ShapeDtypeStruct((nb * t, SSD_D_INNER), BF16),
                   jax.ShapeDtypeStruct((nb, SSD_D_INNER, SSD_STATE), F32)],
        scratch_shapes=[pltpu.VMEM((CONV_HALO + l, SSD_D_INNER), F32),
                        pltpu.VMEM((CONV_HALO + l, SSD_BC), F32),
                        pltpu.VMEM((SSD_STATE, SSD_D_INNER), F32)],
        compiler_params=_cparams("parallel", "arbitrary"),
    )(u, u, u, u_dt, bufx, bufbc, prm["w_conv_x"], prm["b_conv_x"], prm["w_conv_bc"], prm["b_conv_bc"],
      prm["dt_bias"], prm["a_log"], prm["d_skip"], prm["g_ssd_norm"], h0, tri, sel)
    return y, hout


def _ssd_step_prep_kernel(x_ref, bc_ref, dt_ref, bufx_ref, bufbc_ref, wx_ref, bx_ref, wbc_ref, bbc_ref,
                          dtb_ref, alog_ref, exp_ref, xs_ref, bcs_ref, dtx_ref, dae_ref):
    xs = _silu(_conv_step(bufx_ref, x_ref[...], wx_ref, bx_ref))
    xs_ref[...] = xs
    bcs_ref[...] = _silu(_conv_step(bufbc_ref, bc_ref[...], wbc_ref, bbc_ref))
    dt = _softplus(dt_ref[...] + dtb_ref[...])
    da = jnp.exp(dt * (-jnp.exp(alog_ref[...])))
    sel = exp_ref[...]
    dtx_ref[...] = _dot_sel(dt, sel) * xs
    dae_ref[...] = _dot_sel(da, sel)


def _ssd_step_prep(u, u_dt, bufx_t, bufbc_t, prm):
    m = u_dt.shape[0]
    _, sel = _ssd_consts()
    full = lambda a: pl.BlockSpec(a.shape, lambda i: (0,) * a.ndim)
    return pl.pallas_call(
        _ssd_step_prep_kernel, name="ssd_step_prep",
        grid=(1,),
        in_specs=[pl.BlockSpec((m, SSD_D_INNER), lambda i: (0, 1)),
                  pl.BlockSpec((m, SSD_BC), lambda i: (0, 4)),
                  full(u_dt), full(bufx_t), full(bufbc_t),
                  full(prm["w_conv_x"]), full(prm["b_conv_x"]), full(prm["w_conv_bc"]), full(prm["b_conv_bc"]),
                  full(prm["dt_bias"]), full(prm["a_log"]), full(sel)],
        out_specs=[pl.BlockSpec((m, SSD_D_INNER), lambda i: (0, 0)),
                   pl.BlockSpec((m, SSD_BC), lambda i: (0, 0)),
                   pl.BlockSpec((m, SSD_D_INNER), lambda i: (0, 0)),
                   pl.BlockSpec((m, SSD_D_INNER), lambda i: (0, 0))],
        out_shape=[jax.ShapeDtypeStruct((m, SSD_D_INNER), F32),
                   jax.ShapeDtypeStruct((m, SSD_BC), F32),
                   jax.ShapeDtypeStruct((m, SSD_D_INNER), F32),
                   jax.ShapeDtypeStruct((m, SSD_D_INNER), F32)],
        compiler_params=_cparams("arbitrary"),
    )(u, u, u_dt, bufx_t, bufbc_t, prm["w_conv_x"], prm["b_conv_x"], prm["w_conv_bc"], prm["b_conv_bc"],
      prm["dt_bias"], prm["a_log"], sel)


SSD_STEP_SAMPLES = 4


def _ssd_step_state_kernel(h0_ref, cols_ref, b_ref, c_ref, hn_ref, y_ref):
    nr = SSD_D_INNER // LANES
    rows_per_group = SSD_D_INNER // SSD_GROUPS // LANES
    gw = SSD_D_INNER // SSD_GROUPS
    for s in range(SSD_STEP_SAMPLES):
        pad = jnp.concatenate([cols_ref[s], jnp.zeros((LANES - 2 * nr, LANES), F32)], axis=0)
        ct = pad.T
        for r in range(nr):
            g = r // rows_per_group
            hn_ref[s, r * LANES:(r + 1) * LANES, :] = (
                ct[:, nr + r:nr + r + 1] * h0_ref[s, r * LANES:(r + 1) * LANES, :]
                + ct[:, r:r + 1] * b_ref[s, g:g + 1, :])
        res = _dot_nt(c_ref[s].astype(BF16), hn_ref[s].astype(BF16))
        y_ref[s] = jnp.concatenate([res[g:g + 1, g * gw:(g + 1) * gw] for g in range(SSD_GROUPS)], axis=1)


def _ssd_step_state(h0, cols, bmat, cmat):
    m = h0.shape[0]
    sps = SSD_STEP_SAMPLES
    assert m % sps == 0
    per = lambda shape: pl.BlockSpec((sps,) + shape, lambda i: (i, 0, 0))
    return pl.pallas_call(
        _ssd_step_state_kernel, name="ssd_step_state",
        grid=(m // sps,),
        in_specs=[per((SSD_D_INNER, SSD_STATE)), per((2 * SSD_D_INNER // LANES, LANES)),
                  per((8, SSD_STATE)), per((8, SSD_STATE))],
        out_specs=[per((SSD_D_INNER, SSD_STATE)), per((1, SSD_D_INNER))],
        out_shape=[jax.ShapeDtypeStruct((m, SSD_D_INNER, SSD_STATE), F32),
                   jax.ShapeDtypeStruct((m, 1, SSD_D_INNER), F32)],
        compiler_params=_cparams("parallel"),
    )(h0, cols, bmat, cmat)


def _ssd_step_post_kernel(y_ref, xs_ref, z_ref, dskip_ref, gnorm_ref, o_ref):
    o_ref[...] = _ssd_gate_norm(y_ref[...], xs_ref[...], z_ref[...], dskip_ref[...],
                                gnorm_ref[...]).astype(o_ref.dtype)


def _ssd_step_post(y, xs, u, prm):
    m = y.shape[0]
    full = lambda a: pl.BlockSpec(a.shape, lambda i: (0,) * a.ndim)
    return pl.pallas_call(
        _ssd_step_post_kernel, name="ssd_step_post",
        grid=(1,),
        in_specs=[full(y), full(xs), pl.BlockSpec((m, SSD_D_INNER), lambda i: (0, 0)),
                  full(prm["d_skip"]), full(prm["g_ssd_norm"])],
        out_specs=pl.BlockSpec((m, SSD_D_INNER), lambda i: (0, 0)),
        out_shape=jax.ShapeDtypeStruct((m, SSD_D_INNER), BF16),
        compiler_params=_cparams("arbitrary"),
    )(y, xs, u, prm["d_skip"], prm["g_ssd_norm"])


def _lru_gates(xc, wr_ref, br_ref, wi_ref, bi_ref, lam_ref, is_pos0):
    xb = xc.astype(BF16)
    rs, is_ = [], []
    for n in range(LRU_BLOCKS):
        blk = xb[:, n * LRU_BLOCK_DIM:(n + 1) * LRU_BLOCK_DIM]
        rs.append(_dot(blk, wr_ref[n]))
        is_.append(_dot(blk, wi_ref[n]))
    r = _sigmoid(jnp.concatenate(rs, axis=1) + br_ref[...])
    i = _sigmoid(jnp.concatenate(is_, axis=1) + bi_ref[...])
    log_a = -LRU_C * r * _softplus(-lam_ref[...])
    a = jnp.exp(log_a)
    th = jnp.tanh(log_a)
    mult = jnp.sqrt(-2.0 * th / (1.0 - th))
    if is_pos0 is not False:
        mult = jnp.where(is_pos0, 1.0, mult)
    return a, mult * i * xc


SUBLANES = 8


def _scan_rows(a, b, h_in):
    tc, w = a.shape
    ng = tc // SUBLANES
    a3 = a.reshape(ng, SUBLANES, w)
    b3 = b.reshape(ng, SUBLANES, w)
    sub = lax.broadcasted_iota(jnp.int32, a3.shape, 1)
    d = 1
    while d < SUBLANES:
        keep = sub >= d
        b3 = b3 + a3 * jnp.where(keep, pltpu.roll(b3, d, 1), 0.0)
        a3 = a3 * jnp.where(keep, pltpu.roll(a3, d, 1), 1.0)
        d *= 2
    carry = h_in
    outs = []
    for g in range(ng):
        hg = b3[g] + a3[g] * carry
        outs.append(hg)
        carry = hg[SUBLANES - 1:SUBLANES, :]
    return jnp.concatenate(outs, axis=0), carry


def _lru_prompt_kernel(lx_ref, lg_ref, buf_ref, wc_ref, bc_ref, wr_ref, br_ref, wi_ref, bi_ref, lam_ref, h0_ref,
                       y_ref, hout_ref, ext_ref, h_ref, *, tc, nc, pos0):
    c = pl.program_id(1)
    first = c == 0

    @pl.when(first)
    def _():
        h_ref[...] = h0_ref[...]

    xc = _conv_chunk(ext_ref, lx_ref[...], buf_ref, wc_ref, bc_ref, first)
    pos = pos0 + c * tc + lax.broadcasted_iota(jnp.int32, (tc, 1), 0)
    a, b = _lru_gates(xc, wr_ref, br_ref, wi_ref, bi_ref, lam_ref, pos == 0)
    h, h_last = _scan_rows(a, b, h_ref[...])
    h_ref[...] = h_last
    y_ref[...] = (h * _gelu_tanh(lg_ref[...])).astype(y_ref.dtype)

    @pl.when(c == nc - 1)
    def _():
        hout_ref[...] = h_last


def _lru_prompt(u, buf, h0, prm, nb, t, pos0):
    tc = 256
    nc = t // tc
    full = lambda a: pl.BlockSpec(a.shape, lambda b, c: (0,) * a.ndim)
    y, hout = pl.pallas_call(
        functools.partial(_lru_prompt_kernel, tc=tc, nc=nc, pos0=pos0), name="lru_prompt",
        grid=(nb, nc),
        in_specs=[pl.BlockSpec((tc, LRU_WIDTH), lambda b, c: (b * nc + c, 5)),
                  pl.BlockSpec((tc, LRU_WIDTH), lambda b, c: (b * nc + c, 6)),
                  pl.BlockSpec((None, CONV_WIDTH - 1, LRU_WIDTH), lambda b, c: (b, 0, 0)),
                  full(prm["w_conv_lru"]), full(prm["b_conv_lru"]), full(prm["w_lru_r"]), full(prm["b_lru_r"]),
                  full(prm["w_lru_i"]), full(prm["b_lru_i"]), full(prm["lru_lambda"]),
                  pl.BlockSpec((None, 1, LRU_WIDTH), lambda b, c: (b, 0, 0))],
        out_specs=[pl.BlockSpec((tc, LRU_WIDTH), lambda b, c: (b * nc + c, 0)),
                   pl.BlockSpec((None, 1, LRU_WIDTH), lambda b, c: (b, 0, 0))],
        out_shape=[jax.ShapeDtypeStruct((nb * t, LRU_WIDTH), BF16),
                   jax.ShapeDtypeStruct((nb, 1, LRU_WIDTH), F32)],
        scratch_shapes=[pltpu.VMEM((CONV_HALO + tc, LRU_WIDTH), F32), pltpu.VMEM((1, LRU_WIDTH), F32)],
        compiler_params=_cparams("parallel", "arbitrary"),
    )(u, u, buf, prm["w_conv_lru"], prm["b_conv_lru"], prm["w_lru_r"], prm["b_lru_r"], prm["w_lru_i"],
      prm["b_lru_i"], prm["lru_lambda"], h0)
    return y, hout


def _lru_step_kernel(lx_ref, lg_ref, buf_ref, wc_ref, bc_ref, wr_ref, br_ref, wi_ref, bi_ref, lam_ref, h0_ref,
                     y_ref, hout_ref, *, pos0):
    xc = _conv_step(buf_ref, lx_ref[...], wc_ref, bc_ref)
    a, b = _lru_gates(xc, wr_ref, br_ref, wi_ref, bi_ref, lam_ref, pos0 == 0)
    h = b + a * h0_ref[...]
    hout_ref[...] = h
    y_ref[...] = (h * _gelu_tanh(lg_ref[...])).astype(y_ref.dtype)


def _lru_step(u, buf_t, h0, prm, pos0):
    m = h0.shape[0]
    full = lambda a: pl.BlockSpec(a.shape, lambda i: (0,) * a.ndim)
    return pl.pallas_call(
        functools.partial(_lru_step_kernel, pos0=pos0), name="lru_step",
        grid=(1,),
        in_specs=[pl.BlockSpec((m, LRU_WIDTH), lambda i: (0, 5)),
                  pl.BlockSpec((m, LRU_WIDTH), lambda i: (0, 6)),
                  full(buf_t), full(prm["w_conv_lru"]), full(prm["b_conv_lru"]), full(prm["w_lru_r"]),
                  full(prm["b_lru_r"]), full(prm["w_lru_i"]), full(prm["b_lru_i"]), full(prm["lru_lambda"]),
                  full(h0)],
        out_specs=[pl.BlockSpec((m, LRU_WIDTH), lambda i: (0, 0)),
                   pl.BlockSpec((m, LRU_WIDTH), lambda i: (0, 0))],
        out_shape=[jax.ShapeDtypeStruct((m, LRU_WIDTH), BF16),
                   jax.ShapeDtypeStruct((m, LRU_WIDTH), F32)],
        compiler_params=_cparams("arbitrary"),
    )(u, u, buf_t, prm["w_conv_lru"], prm["b_conv_lru"], prm["w_lru_r"], prm["b_lru_r"], prm["w_lru_i"],
      prm["b_lru_i"], prm["lru_lambda"], h0)


def _rope_tables(n_pos):
    half = MLA_ROPE // 2
    inv = ROPE_THETA ** (-np.arange(half, dtype=np.float64) * (2.0 / MLA_ROPE))
    ang = np.arange(n_pos, dtype=np.float64)[:, None] * inv[None, :]
    cos, sin = np.cos(ang), np.sin(ang)
    cc = jnp.asarray(np.concatenate([cos, cos], axis=1), F32)
    ss = jnp.asarray(np.concatenate([-sin, sin], axis=1), F32)
    return cc, ss


def _even_params(e, w_in_e, w_pool, s_pool, g_qlat, w_qb, g_kvlat, w_kvb, g_q, g_k, w_out_e):
    nh = MLA_HEADS
    w_in = jnp.pad(jnp.transpose(w_in_e[e]).astype(BF16), ((0, EVEN_IN_PAD - w_in_e.shape[2]), (0, 0)))
    wq = w_qb[e].reshape(MLA_Q_LORA, nh, MLA_QK)
    wq_nope = wq[:, :, :MLA_NOPE].reshape(MLA_Q_LORA, nh * MLA_NOPE)
    wq_rope = jnp.pad(wq[:, :, MLA_NOPE:], ((0, 0), (0, 0), (0, LANES - MLA_ROPE))).reshape(MLA_Q_LORA, nh * LANES)
    wkv = w_kvb[e].reshape(MLA_KV_LORA, nh, MLA_NOPE + MLA_V)
    w_nope_t = jnp.transpose(wkv[:, :, :MLA_NOPE], (1, 2, 0))
    pad_g = lambda g: jnp.pad(g, (0, MLA_QKP - MLA_QK)).reshape(1, MLA_QKP)
    return dict(
        w_in=w_in,
        w_pool=w_pool[e].astype(BF16),
        s_pool=s_pool[e].reshape(1, POOL_WIDTH),
        g_qlat=g_qlat[e].reshape(1, MLA_Q_LORA),
        g_kvlat=g_kvlat[e].reshape(1, MLA_KV_LORA),
        w_qb=jnp.concatenate([wq_nope, wq_rope], axis=1).astype(BF16),
        w_kvb=w_kvb[e].astype(BF16),
        g_q=pad_g(g_q[e]), g_k=pad_g(g_k[e]),
        g_k_rope=g_k[e][MLA_NOPE:].reshape(MLA_ROPE, 1),
        w_nope_t=w_nope_t.astype(BF16),
        w_v=jnp.transpose(wkv[:, :, MLA_NOPE:], (1, 0, 2)).astype(BF16),
        w_out_pool=w_out_e[e][:POOL_WIDTH].astype(BF16),
        w_out_att=w_out_e[e][POOL_WIDTH:].astype(BF16),
    )


ODD_J2 = SSD_D_INNER + SSD_D_INNER + SSD_BC
ODD_J3 = ODD_J2 + SSD_HEADS
ODD_MAIN = ODD_J2 + 2 * LRU_WIDTH


def _prep_w_in_odd_kernel(w_ref, main_ref, dt_ref):
    kb = w_ref.shape[1]
    main_ref[0:ODD_J2, :] = w_ref[0:ODD_J2, :].astype(BF16)
    main_ref[ODD_J2:ODD_MAIN, :] = w_ref[ODD_J3:ODD_J3 + 2 * LRU_WIDTH, :].astype(BF16)
    dt_ref[...] = jnp.concatenate([w_ref[ODD_J2:ODD_J3, :], jnp.zeros((LANES - SSD_HEADS, kb), F32)],
                                  axis=0).astype(BF16)


def _prep_w_in_odd(w_in_o, o):
    _, k, n = w_in_o.shape
    kb = 256
    return pl.pallas_call(
        _prep_w_in_odd_kernel, name="prep_w_in_odd",
        grid=(k // kb,),
        in_specs=[pl.BlockSpec((None, n, kb), lambda i: (o, 0, i))],
        out_specs=[pl.BlockSpec((ODD_MAIN, kb), lambda i: (0, i)), pl.BlockSpec((LANES, kb), lambda i: (0, i))],
        out_shape=[jax.ShapeDtypeStruct((ODD_MAIN, k), BF16), jax.ShapeDtypeStruct((LANES, k), BF16)],
        compiler_params=_cparams("parallel"),
    )(jnp.transpose(w_in_o, (0, 2, 1)))


def _odd_params(o, w_in_o, w_conv_ssd, b_conv_ssd, dt_bias, a_log, d_skip, g_ssd_norm, w_conv_lru, b_conv_lru,
                w_lru_r, b_lru_r, w_lru_i, b_lru_i, lru_lambda, w_out_o):
    pad_heads = lambda v: jnp.pad(v, (0, LANES - SSD_HEADS)).reshape(1, LANES)
    w_in, w_in_dt = _prep_w_in_odd(w_in_o, o)
    return dict(
        w_in=w_in,
        w_in_dt=w_in_dt,
        w_conv_x=w_conv_ssd[o][:, :SSD_D_INNER], w_conv_bc=w_conv_ssd[o][:, SSD_D_INNER:],
        b_conv_x=b_conv_ssd[o][:SSD_D_INNER].reshape(1, -1), b_conv_bc=b_conv_ssd[o][SSD_D_INNER:].reshape(1, -1),
        dt_bias=pad_heads(dt_bias[o]), a_log=pad_heads(a_log[o]),
        d_skip=jnp.repeat(d_skip[o], SSD_HEAD_DIM).reshape(1, SSD_D_INNER),
        g_ssd_norm=g_ssd_norm[o].reshape(1, SSD_D_INNER),
        w_conv_lru=w_conv_lru[o], b_conv_lru=b_conv_lru[o].reshape(1, LRU_WIDTH),
        w_lru_r=w_lru_r[o].astype(BF16), b_lru_r=b_lru_r[o].reshape(1, LRU_WIDTH),
        w_lru_i=w_lru_i[o].astype(BF16), b_lru_i=b_lru_i[o].reshape(1, LRU_WIDTH),
        lru_lambda=lru_lambda[o].reshape(1, LRU_WIDTH),
        w_out_ssd=w_out_o[o][:SSD_D_INNER].astype(BF16),
        w_out_lru=w_out_o[o][SSD_D_INNER:].astype(BF16),
    )


def _pad_tab(tab):
    return jnp.pad(tab, ((0, 0), (0, LANES - tab.shape[1])))


def _even_layer_prompt(rows, x, g, mod, prm, nb, t):
    u = _norm_matmul_nt(rows, x, g, mod, 1, 0, prm["w_in"], name="in_even", tn=EVEN_IN_PAD)
    zero_buf = jnp.zeros((nb, POOL_BUF, POOL_WIDTH), F32)
    y_pool = _pool_prompt(u, zero_buf, prm["w_pool"], prm["s_pool"], nb, t, 0)
    cc, ss = _rope_tables(t)
    tm = 256
    tiles = t // tm
    q, k, v, lat = _mla_prep(u, rows.m, tm, lambda i: i % tiles, (_pad_tab(cc), _pad_tab(ss)), prm, False)
    y_att = _flash_attention(q, k, v, nb, t)
    u3 = u.reshape(nb, t, EVEN_IN_PAD)
    kr = u3[:, :, POOL_WIDTH + MLA_Q_LORA + MLA_KV_LORA:POOL_WIDTH + MLA_Q_LORA + MLA_KV_LORA + MLA_ROPE]
    mla_rows = jnp.concatenate([lat.reshape(nb, t, MLA_KV_LORA), kr], axis=-1)
    pool_new = u3[:, t - POOL_BUF:, :POOL_WIDTH]
    return [y_pool, y_att], [prm["w_out_pool"], prm["w_out_att"]], mla_rows, pool_new


def _even_layer_sample(rows, x, g, mod, prm, pool_buf, cache, layer, page_table, pos0):
    m = rows.m
    nh = MLA_HEADS
    u = _norm_matmul_nt(rows, x, g, mod, 1, 0, prm["w_in"], name="in_even", tn=640)
    y_pool = _pool_step(u, jnp.transpose(pool_buf, (1, 0, 2)), prm["w_pool"], prm["s_pool"], pos0)
    cc, ss = _rope_tables(pos0 + 1)
    cc_new = jnp.broadcast_to(_pad_tab(cc[pos0:]), (m, LANES))
    ss_new = jnp.broadcast_to(_pad_tab(ss[pos0:]), (m, LANES))
    q, k, v, lat, qg = _mla_prep(u, m, m, lambda i: 0, (cc_new, ss_new), prm, True)
    qp = _bmm_heads(qg, prm["w_nope_t"], BF16)
    qp = jnp.pad(jnp.transpose(qp, (1, 0, 2)), ((0, 0), (0, DEC_QROWS - nh), (0, 0)))
    q_m = jnp.transpose(q, (1, 0, 2))
    qr = jnp.pad(q_m[:, :, MLA_NOPE:MLA_NOPE + MLA_ROPE], ((0, 0), (0, 0), (0, LANES - MLA_ROPE)))
    ones_row = jnp.concatenate([jnp.zeros((m, 1, MLA_ROPE), BF16), jnp.ones((m, 1, LANES - MLA_ROPE), BF16)], axis=2)
    qr = jnp.concatenate([qr, ones_row, jnp.zeros((m, DEC_QROWS - nh - 1, LANES), BF16)], axis=1)
    o_lat = _mla_decode(cache, layer, page_table, prm["w_nope_t"].reshape(nh * MLA_NOPE, MLA_KV_LORA), qp, qr,
                        prm["g_k_rope"], cc[:pos0].T, ss[:pos0].T, q_m.astype(F32),
                        jnp.transpose(k, (1, 0, 2)).astype(F32), lat.reshape(m, 1, MLA_KV_LORA))
    y_att = _bmm_heads(jnp.transpose(o_lat, (1, 0, 2)).astype(BF16), prm["w_v"], BF16)
    y_att = jnp.transpose(y_att, (1, 0, 2)).reshape(m, nh * MLA_V)
    kr = u[:, POOL_WIDTH + MLA_Q_LORA + MLA_KV_LORA:POOL_WIDTH + MLA_Q_LORA + MLA_KV_LORA + MLA_ROPE]
    mla_rows = jnp.concatenate([lat, kr], axis=-1).reshape(m, 1, MLA_KV_LORA + MLA_ROPE)
    pool_new = jnp.concatenate([pool_buf[:, 1:], u[:, None, :POOL_WIDTH]], axis=1)
    return [y_pool, y_att], [prm["w_out_pool"], prm["w_out_att"]], mla_rows, pool_new


def _odd_layer_prompt(rows, hn, prm, nb, t):
    u = _matmul(rows, [hn], [prm["w_in"]], name="in_odd", tm=1024, tn=1792, w_t=True)
    u_dt = _matmul(rows, [hn], [prm["w_in_dt"]], name="in_odd_dt", tn=LANES, w_t=True)
    k1 = CONV_WIDTH - 1
    y_ssd, h_ssd = _ssd_prompt(u, u_dt, jnp.zeros((nb, k1, SSD_D_INNER), F32), jnp.zeros((nb, k1, SSD_BC), F32),
                               jnp.zeros((nb, SSD_D_INNER, SSD_STATE), F32), prm, nb, t)
    y_lru, h_lru = _lru_prompt(u, jnp.zeros((nb, k1, LRU_WIDTH), F32), jnp.zeros((nb, 1, LRU_WIDTH), F32),
                               prm, nb, t, 0)
    u3 = u.reshape(nb, t, -1)
    sconv = u3[:, t - k1:, SSD_D_INNER:2 * SSD_D_INNER + SSD_BC]
    lconv = u3[:, t - k1:, 2 * SSD_D_INNER + SSD_BC:2 * SSD_D_INNER + SSD_BC + LRU_WIDTH]
    return ([y_ssd, y_lru], [prm["w_out_ssd"], prm["w_out_lru"]], sconv,
            h_ssd.reshape(nb, SSD_HEADS, SSD_HEAD_DIM, SSD_STATE), lconv, h_lru.reshape(nb, LRU_WIDTH))


def _odd_layer_sample(rows, hn, prm, sconv_buf, ssd_state, lconv_buf, lru_state, pos0):
    m = rows.m
    u = _matmul(rows, [hn], [prm["w_in"]], name="in_odd", tn=1024, w_t=True)
    u_dt = _matmul(rows, [hn], [prm["w_in_dt"]], name="in_odd_dt", tn=LANES, w_t=True)
    sconv_t = jnp.transpose(sconv_buf, (1, 0, 2))
    xs, bcs, dtx, dae = _ssd_step_prep(u, u_dt, sconv_t[:, :, :SSD_D_INNER], sconv_t[:, :, SSD_D_INNER:], prm)
    nr = SSD_D_INNER // LANES
    cols = jnp.concatenate([dtx.reshape(m, nr, LANES), dae.reshape(m, nr, LANES)], axis=1)
    gs = SSD_GROUPS * SSD_STATE
    pad8 = lambda a: jnp.pad(a.reshape(m, SSD_GROUPS, SSD_STATE), ((0, 0), (0, 8 - SSD_GROUPS), (0, 0)))
    h_new, y = _ssd_step_state(ssd_state.reshape(m, SSD_D_INNER, SSD_STATE), cols,
                               pad8(bcs[:, :gs]), pad8(bcs[:, gs:]))
    y_ssd = _ssd_step_post(y.reshape(m, SSD_D_INNER), xs, u, prm)
    y_lru, h_lru = _lru_step(u, jnp.transpose(lconv_buf, (1, 0, 2)), lru_state, prm, pos0)
    u_xbc = u[:, SSD_D_INNER:2 * SSD_D_INNER + SSD_BC]
    u_lx = u[:, 2 * SSD_D_INNER + SSD_BC:2 * SSD_D_INNER + SSD_BC + LRU_WIDTH]
    sconv = jnp.concatenate([sconv_buf[:, 1:], u_xbc[:, None]], axis=1)
    lconv = jnp.concatenate([lconv_buf[:, 1:], u_lx[:, None]], axis=1)
    return ([y_ssd, y_lru], [prm["w_out_ssd"], prm["w_out_lru"]], sconv,
            h_new.reshape(m, SSD_HEADS, SSD_HEAD_DIM, SSD_STATE), lconv, h_lru)


def kernel(x_prompt, x_sample, cache_mla, state_pool, state_ssd_conv, state_ssd, state_lru_conv, state_lru, page_table, c_prompt, c_sample, g_norm1, g_norm2, w_mod, b_mod, w_mlp1, w_mlp2, w_in_e, w_pool, s_pool, g_qlat, w_qb, g_kvlat, w_kvb, g_q, g_k, w_out_e, w_in_o, w_conv_ssd, b_conv_ssd, dt_bias, a_log, d_skip, g_ssd_norm, w_conv_lru, b_conv_lru, w_lru_r, b_lru_r, w_lru_i, b_lru_i, lru_lambda, w_out_o):
    nb, t, _ = x_prompt.shape
    ns = x_sample.shape[0]
    assert x_sample.shape[1] == 1 and t >= POOL_BUF
    pos0_s = page_table.shape[1] * PAGE_SIZE

    pad_rows = (-(ns + nb)) % 8
    c_all = jnp.concatenate([c_sample, c_prompt, jnp.zeros((pad_rows, D_MODEL), F32)], axis=0)
    mod_all = _modulation(c_all, w_mod, b_mod)

    rows_p = _Rows(nb, t, 512)
    rows_po = _Rows(nb, t, 256)
    rows_s = _Rows(ns, 1, ns)
    xp = x_prompt.reshape(nb * t, D_MODEL)
    xs = x_sample.reshape(ns, D_MODEL)
    w2 = w_mlp2.astype(BF16)

    outs_p, outs_s = {}, {}
    for layer in range(DEPTH):
        mod_p = rows_p.mod_array(mod_all[layer, ns:ns + nb])
        mod_s = rows_s.mod_array(mod_all[layer, :ns])
        if layer % 2 == 0:
            e = layer // 2
            prm = _even_params(e, w_in_e, w_pool, s_pool, g_qlat, w_qb, g_kvlat, w_kvb, g_q, g_k, w_out_e)
            a_p, w_o, mla_p, pool_p = _even_layer_prompt(rows_p, xp, g_norm1[layer], mod_p, prm, nb, t)
            a_s, _, mla_s, pool_s = _even_layer_sample(rows_s, xs, g_norm1[layer], mod_s, prm, state_pool[e],
                                                       cache_mla, e, page_table, pos0_s)
            outs_p.setdefault("mla", []).append(mla_p)
            outs_p.setdefault("pool", []).append(pool_p)
            outs_s.setdefault("mla", []).append(mla_s)
            outs_s.setdefault("pool", []).append(pool_s)
        else:
            o = layer // 2
            prm = _odd_params(o, w_in_o, w_conv_ssd, b_conv_ssd, dt_bias, a_log, d_skip, g_ssd_norm, w_conv_lru,
                              b_conv_lru, w_lru_r, b_lru_r, w_lru_i, b_lru_i, lru_lambda, w_out_o)
            hn_p = _norm_mod(rows_p, xp, g_norm1[layer], mod_p, 1, 0)
            hn_s = _norm_mod(rows_s, xs, g_norm1[layer], mod_s, 1, 0)
            a_p, w_o, sconv_p, ssd_p, lconv_p, lru_p = _odd_layer_prompt(rows_p, hn_p, prm, nb, t)
            a_s, _, sconv_s, ssd_s, lconv_s, lru_s = _odd_layer_sample(
                rows_s, hn_s, prm, state_ssd_conv[o], state_ssd[o], state_lru_conv[o], state_lru[o], pos0_s)
            for d, vals in ((outs_p, (sconv_p, ssd_p, lconv_p, lru_p)), (outs_s, (sconv_s, ssd_s, lconv_s, lru_s))):
                for name, val in zip(("sconv", "ssd", "lconv", "lru"), vals):
                    d.setdefault(name, []).append(val)
        xp, hn2_p = _matmul(rows_po, a_p, w_o, name="out_proj", tn=D_MODEL, res=xp, mod=mod_p, gate_chunk=2,
                            norm=(g_norm2[layer], 4, 3))
        xs, hn2_s = _matmul(rows_s, a_s, w_o, name="out_proj_s", tn=D_MODEL, res=xs, mod=mod_s, gate_chunk=2,
                            norm=(g_norm2[layer], 4, 3))
        act_p = _matmul_ws(hn2_p, w_mlp1, layer, name="mlp1", tm=1024, tn=1024, act="relu2", out_dtype=BF16)
        act_s = _matmul(rows_s, [hn2_s], [(w_mlp1, layer)], name="mlp1_s", tn=1024, act="relu2", out_dtype=BF16)
        xp = _matmul(rows_p, [act_p], [(w2, layer)], name="mlp2", tn=512, res=xp, mod=mod_p, gate_chunk=5)
        xs = _matmul(rows_s, [act_s], [(w_mlp2, layer)], name="mlp2_s", tn=1024, tk=2048, res=xs, mod=mod_s,
                     gate_chunk=5)

    st = lambda d, name: jnp.stack(d[name])
    return (xp.reshape(nb, t, D_MODEL), xs.reshape(ns, 1, D_MODEL),
            st(outs_p, "mla"), st(outs_s, "mla"), st(outs_p, "pool"), st(outs_s, "pool"),
            st(outs_p, "sconv"), st(outs_s, "sconv"), st(outs_p, "ssd"), st(outs_s, "ssd"),
            st(outs_p, "lconv"), st(outs_s, "lconv"), st(outs_p, "lru"), st(outs_s, "lru"))
```

```python
import functools

import numpy as np
import jax
import jax.numpy as jnp
from jax import lax
from jax.experimental import pallas as pl
from jax.experimental.pallas import tpu as pltpu

F32 = jnp.float32
BF16 = jnp.bfloat16

VMEM_LIMIT_BYTES = 48 * 1024 * 1024
LANES = 128

D_MODEL = 2048
EPS = 1e-6
N_MOD = 6
DEPTH = 2

POOL_WIDTH = 1024
POOL_WINDOWS = (2, 4, 8, 16)
POOL_GROUP_DIM = 256
POOL_BUF = 15

MLA_HEADS = 8
MLA_NOPE = 128
MLA_ROPE = 64
MLA_V = 128
MLA_QK = 192
MLA_QKP = 256
MLA_Q_LORA = 512
MLA_KV_LORA = 256
ROPE_THETA = 10000.0
PAGE_SIZE = 128
EVEN_IN_PAD = 1920

SSD_D_INNER = 2048
SSD_HEAD_DIM = 64
SSD_HEADS = 32
SSD_GROUPS = 4
SSD_STATE = 128
SSD_CHUNK = 128
SSD_BC = 1024
CONV_WIDTH = 4

LRU_WIDTH = 1024
LRU_BLOCKS = 8
LRU_BLOCK_DIM = 128
LRU_C = 8.0

MLP_HIDDEN = 8192


def _cparams(*sem):
    return pltpu.CompilerParams(dimension_semantics=sem, vmem_limit_bytes=VMEM_LIMIT_BYTES)


def _sigmoid(x):
    return 1.0 / (1.0 + jnp.exp(-x))


def _silu(x):
    return x * _sigmoid(x)


def _softplus(x):
    return jnp.maximum(x, 0.0) + jnp.log1p(jnp.exp(-jnp.abs(x)))


def _gelu_tanh(x):
    return 0.5 * x * (1.0 + jnp.tanh(0.7978845608028654 * (x + 0.044715 * (x * x * x))))


def _dot(a, b):
    return jnp.dot(a, b, preferred_element_type=F32)


def _dot_nt(a, b):
    return lax.dot_general(a, b, (((1,), (1,)), ((), ())), preferred_element_type=F32)


def _split3(x):
    hi = x.astype(BF16)
    r = x - hi.astype(F32)
    mid = r.astype(BF16)
    lo = (r - mid.astype(F32)).astype(BF16)
    return hi, mid, lo


def _dot_sel(x, sel):
    hi, mid, lo = _split3(x)
    return _dot(hi, sel) + _dot(mid, sel) + _dot(lo, sel)


def _mod_kernel(c_ref, w_ref, b_ref, o_ref):
    c = c_ref[...]
    o_ref[...] = _dot(_silu(c).astype(BF16), w_ref[...].astype(BF16)) + b_ref[...]


def _modulation(c_all, w_mod, b_mod):
    mp = c_all.shape[0]
    n = N_MOD * D_MODEL
    tn = 1024
    return pl.pallas_call(
        _mod_kernel, name="modulation",
        grid=(DEPTH, n // tn),
        in_specs=[pl.BlockSpec((mp, D_MODEL), lambda l, j: (0, 0)),
                  pl.BlockSpec((None, D_MODEL, tn), lambda l, j: (l, 0, j)),
                  pl.BlockSpec((None, 1, tn), lambda l, j: (l, 0, j))],
        out_specs=pl.BlockSpec((None, mp, tn), lambda l, j: (l, 0, j)),
        out_shape=jax.ShapeDtypeStruct((DEPTH, mp, n), F32),
        compiler_params=_cparams("parallel", "parallel"),
    )(c_all, w_mod, b_mod.reshape(DEPTH, 1, n))


class _Rows:
    def __init__(self, nb, t, tm):
        self.nb, self.t, self.tm = nb, t, tm
        self.m = nb * t
        self.per_seq = t > 1
        if self.per_seq:
            assert t % tm == 0
            self.tiles_per_seq = t // tm

    def mod_array(self, mod_rows):
        return mod_rows.reshape(self.nb, 1, -1) if self.per_seq else mod_rows

    def mod_spec(self, chunk, width, ncol, col_of):
        per_chunk = D_MODEL // width
        if self.per_seq:
            tps = self.tiles_per_seq
            return pl.BlockSpec((None, 1, width),
                                lambda *g: (g[0] // tps, 0, chunk * per_chunk + col_of(*g)))
        return pl.BlockSpec((self.tm, width), lambda *g: (g[0], chunk * per_chunk + col_of(*g)))


def _norm_mod_kernel(x_ref, g_ref, sc_ref, sh_ref, o_ref):
    x = x_ref[...]
    y = x * lax.rsqrt(jnp.mean(x * x, axis=-1, keepdims=True) + EPS) * g_ref[...]
    o_ref[...] = (y * (1.0 + sc_ref[...]) + sh_ref[...]).astype(o_ref.dtype)


def _norm_mod(rows, x, g, mod, sc_chunk, sh_chunk):
    tm = min(rows.tm, 512)
    r = _Rows(rows.nb, rows.t, tm)
    return pl.pallas_call(
        _norm_mod_kernel, name="norm_mod",
        grid=(r.m // tm,),
        in_specs=[pl.BlockSpec((tm, D_MODEL), lambda i: (i, 0)),
                  pl.BlockSpec((1, D_MODEL), lambda i: (0, 0)),
                  r.mod_spec(sc_chunk, D_MODEL, 1, lambda i: 0),
                  r.mod_spec(sh_chunk, D_MODEL, 1, lambda i: 0)],
        out_specs=pl.BlockSpec((tm, D_MODEL), lambda i: (i, 0)),
        out_shape=jax.ShapeDtypeStruct((r.m, D_MODEL), BF16),
        compiler_params=_cparams("parallel"),
    )(x, g.reshape(1, D_MODEL), mod, mod)


def _mm_kernel(*refs, n_a, nk, act, has_res, w_t, has_norm=False):
    a_refs = refs[:n_a]
    w_refs = refs[n_a:2 * n_a]
    pos = 2 * n_a
    if has_res:
        x_ref, gt_ref = refs[pos], refs[pos + 1]
        pos += 2
    if has_norm:
        g_ref, sc_ref, sh_ref = refs[pos:pos + 3]
        pos += 3
    o_ref = refs[pos]
    pos += 1
    if has_norm:
        hn_ref = refs[pos]
        pos += 1
    acc_ref = refs[pos] if nk > 1 else None

    dot = _dot_nt if w_t else _dot
    part = dot(a_refs[0][...], w_refs[0][...].astype(BF16))
    for a_ref, w_ref in zip(a_refs[1:], w_refs[1:]):
        part = part + dot(a_ref[...], w_ref[...].astype(BF16))

    def finish(acc):
        if act == "relu2":
            acc = jnp.square(jnp.maximum(acc, 0.0))
        if has_res:
            acc = x_ref[...] + gt_ref[...] * acc
        o_ref[...] = acc.astype(o_ref.dtype)
        if has_norm:
            y = acc * lax.rsqrt(jnp.mean(acc * acc, axis=-1, keepdims=True) + EPS) * g_ref[...]
            hn_ref[...] = (y * (1.0 + sc_ref[...]) + sh_ref[...]).astype(hn_ref.dtype)

    if nk == 1:
        finish(part)
    else:
        k = pl.program_id(2)

        @pl.when(k == 0)
        def _():
            acc_ref[...] = part

        @pl.when(k > 0)
        def _():
            acc_ref[...] += part

        @pl.when(k == nk - 1)
        def _():
            finish(acc_ref[...])


def _matmul(rows, a_list, w_list, *, name, tn, tk=None, act=None, res=None, mod=None, gate_chunk=None,
            out_dtype=F32, tm=None, w_t=False, norm=None):
    assert tm is None or res is None
    tm, m = (tm or rows.tm), rows.m
    n_a = len(a_list)
    if w_t:
        assert n_a == 1 and tk is None and not isinstance(w_list[0], tuple)
        n, kfull = w_list[0].shape
        return pl.pallas_call(
            functools.partial(_mm_kernel, n_a=1, nk=1, act=act, has_res=False, w_t=True), name=name,
            grid=(m // tm, n // tn, 1),
            in_specs=[pl.BlockSpec((tm, kfull), lambda i, j, k: (i, 0)),
                      pl.BlockSpec((tn, kfull), lambda i, j, k: (j, 0))],
            out_specs=pl.BlockSpec((tm, tn), lambda i, j, k: (i, j)),
            out_shape=jax.ShapeDtypeStruct((m, n), out_dtype),
            compiler_params=_cparams("parallel", "parallel", "arbitrary"),
        )(a_list[0], w_list[0])
    n = (w_list[0][0] if isinstance(w_list[0], tuple) else w_list[0]).shape[-1]
    if tk is None or n_a > 1:
        nk = 1
    else:
        assert a_list[0].shape[1] % tk == 0
        nk = a_list[0].shape[1] // tk
    assert n % tn == 0 and m % tm == 0
    in_specs, args = [], []
    for a in a_list:
        kk = a.shape[1] if nk == 1 else tk
        in_specs.append(pl.BlockSpec((tm, kk), lambda i, j, k: (i, k)))
        args.append(a)
    for w in w_list:
        if isinstance(w, tuple):
            w, layer = w
            kk = w.shape[1] if nk == 1 else tk
            in_specs.append(pl.BlockSpec((None, kk, tn), lambda i, j, k, layer=layer: (layer, k, j)))
        else:
            kk = w.shape[0] if nk == 1 else tk
            in_specs.append(pl.BlockSpec((kk, tn), lambda i, j, k: (k, j)))
        args.append(w)
    has_res = res is not None
    if has_res:
        in_specs.append(pl.BlockSpec((tm, tn), lambda i, j, k: (i, j)))
        in_specs.append(rows.mod_spec(gate_chunk, tn, n // tn, lambda i, j, k: j))
        args += [res, mod]
    scratch = [pltpu.VMEM((tm, tn), F32)] if nk > 1 else []
    out_specs = [pl.BlockSpec((tm, tn), lambda i, j, k: (i, j))]
    out_shape = [jax.ShapeDtypeStruct((m, n), out_dtype)]
    if norm is not None:
        gain, sc_chunk, sh_chunk = norm
        assert tn == n and has_res
        in_specs += [pl.BlockSpec((1, n), lambda i, j, k: (0, 0)),
                     rows.mod_spec(sc_chunk, n, 1, lambda i, j, k: 0),
                     rows.mod_spec(sh_chunk, n, 1, lambda i, j, k: 0)]
        args += [gain.reshape(1, n), mod, mod]
        out_specs.append(pl.BlockSpec((tm, n), lambda i, j, k: (i, 0)))
        out_shape.append(jax.ShapeDtypeStruct((m, n), BF16))
    outs = pl.pallas_call(
        functools.partial(_mm_kernel, n_a=n_a, nk=nk, act=act, has_res=has_res, w_t=False,
                          has_norm=norm is not None), name=name,
        grid=(m // tm, n // tn, nk),
        in_specs=in_specs,
        out_specs=out_specs,
        out_shape=out_shape,
        scratch_shapes=scratch,
        compiler_params=_cparams("parallel", "parallel", "arbitrary"),
    )(*args)
    return outs if norm is not None else outs[0]


def _norm_mm_nt_kernel(x_ref, g_ref, sc_ref, sh_ref, w_ref, o_ref):
    x = x_ref[...]
    y = x * lax.rsqrt(jnp.mean(x * x, axis=-1, keepdims=True) + EPS) * g_ref[...]
    hn = (y * (1.0 + sc_ref[...]) + sh_ref[...]).astype(BF16)
    o_ref[...] = _dot_nt(hn, w_ref[...]).astype(o_ref.dtype)


def _norm_matmul_nt(rows, x, g, mod, sc_chunk, sh_chunk, w_t, *, name, tn):
    tm, m = rows.tm, rows.m
    n, k = w_t.shape
    assert n % tn == 0 and m % tm == 0
    return pl.pallas_call(
        _norm_mm_nt_kernel, name=name,
        grid=(m // tm, n // tn),
        in_specs=[pl.BlockSpec((tm, k), lambda i, j: (i, 0)),
                  pl.BlockSpec((1, k), lambda i, j: (0, 0)),
                  rows.mod_spec(sc_chunk, k, 1, lambda i, j: 0),
                  rows.mod_spec(sh_chunk, k, 1, lambda i, j: 0),
                  pl.BlockSpec((tn, k), lambda i, j: (j, 0))],
        out_specs=pl.BlockSpec((tm, tn), lambda i, j: (i, j)),
        out_shape=jax.ShapeDtypeStruct((m, n), F32),
        compiler_params=_cparams("parallel", "parallel"),
    )(x, g.reshape(1, k), mod, mod, w_t)


def _mm_ws_kernel(a_ref, w_ref, o_ref, wb_ref, *, act):
    @pl.when(pl.program_id(1) == 0)
    def _():
        wb_ref[...] = w_ref[...].astype(BF16)

    acc = _dot(a_ref[...], wb_ref[...])
    if act == "relu2":
        acc = jnp.square(jnp.maximum(acc, 0.0))
    o_ref[...] = acc.astype(o_ref.dtype)


def _matmul_ws(a, w_stack, layer, *, name, tm, tn, act=None, out_dtype=F32):
    m, k = a.shape
    n = w_stack.shape[-1]
    assert m % tm == 0 and n % tn == 0
    return pl.pallas_call(
        functools.partial(_mm_ws_kernel, act=act), name=name,
        grid=(n // tn, m // tm),
        in_specs=[pl.BlockSpec((tm, k), lambda j, i: (i, 0)),
                  pl.BlockSpec((None, k, tn), lambda j, i: (layer, 0, j))],
        out_specs=pl.BlockSpec((tm, tn), lambda j, i: (i, j)),
        out_shape=jax.ShapeDtypeStruct((m, n), out_dtype),
        scratch_shapes=[pltpu.VMEM((k, tn), BF16)],
        compiler_params=_cparams("parallel", "arbitrary"),
    )(a, w_stack)


def _bmm_kernel(a_ref, w_ref, o_ref):
    o_ref[...] = _dot(a_ref[...], w_ref[...]).astype(o_ref.dtype)


def _bmm_heads(a, w, out_dtype):
    h, m, k = a.shape
    n = w.shape[2]
    return pl.pallas_call(
        _bmm_kernel, name="bmm_heads",
        grid=(h,),
        in_specs=[pl.BlockSpec((None, m, k), lambda i: (i, 0, 0)),
                  pl.BlockSpec((None, k, n), lambda i: (i, 0, 0))],
        out_specs=pl.BlockSpec((None, m, n), lambda i: (i, 0, 0)),
        out_shape=jax.ShapeDtypeStruct((h, m, n), out_dtype),
        compiler_params=_cparams("parallel"),
    )(a, w)


CONV_HALO = 8


def _conv_chunk(ext_ref, u, buf_ref, w_ref, b_ref, first):
    tc = u.shape[0]
    k1 = CONV_WIDTH - 1

    @pl.when(first)
    def _():
        ext_ref[CONV_HALO - k1:CONV_HALO, :] = buf_ref[...]

    ext_ref[CONV_HALO:CONV_HALO + tc, :] = u
    y = b_ref[...] + w_ref[k1:k1 + 1, :] * u
    for k in range(k1):
        y = y + w_ref[k:k + 1, :] * ext_ref[CONV_HALO - k1 + k:CONV_HALO - k1 + k + tc, :]
    ext_ref[CONV_HALO - k1:CONV_HALO, :] = ext_ref[CONV_HALO + tc - k1:CONV_HALO + tc, :]
    return y


def _conv_step(buf_ref, u, w_ref, b_ref):
    y = b_ref[...] + w_ref[CONV_WIDTH - 1:CONV_WIDTH, :] * u
    for k in range(CONV_WIDTH - 1):
        y = y + w_ref[k:k + 1, :] * buf_ref[k]
    return y


def _pool_prompt_kernel(u_ref, buf_ref, w_ref, s_ref, o_ref, ext_ref, *, tc, pos0):
    c = pl.program_id(1)
    halo = 16

    @pl.when(c == 0)
    def _():
        ext_ref[0:1, :] = jnp.zeros((1, POOL_WIDTH), F32)
        ext_ref[1:halo, :] = buf_ref[...]

    u = u_ref[...]
    ext_ref[halo:halo + tc, :] = u
    pos = (pos0 + c * tc + lax.broadcasted_iota(jnp.int32, (tc, 1), 0)).astype(F32)
    outs = []
    for g, w in enumerate(POOL_WINDOWS):
        lo, hi = g * POOL_GROUP_DIM, (g + 1) * POOL_GROUP_DIM
        x = u[:, lo:hi]
        wsum = x
        for j in range(1, w):
            wsum = wsum + ext_ref[halo - j:halo - j + tc, lo:hi]
        cnt = jnp.minimum(pos + 1.0, float(w))
        d = (wsum / cnt - x).astype(BF16)
        outs.append(_dot(d, w_ref[g]))
    y = jnp.concatenate(outs, axis=1) * s_ref[...]
    o_ref[...] = y.astype(o_ref.dtype)
    ext_ref[0:halo, :] = ext_ref[tc:tc + halo, :]


def _pool_prompt(u, buf, w_pool, s_pool, nb, t, pos0):
    tc = 512
    nc = t // tc
    return pl.pallas_call(
        functools.partial(_pool_prompt_kernel, tc=tc, pos0=pos0), name="pool_prompt",
        grid=(nb, nc),
        in_specs=[pl.BlockSpec((tc, POOL_WIDTH), lambda b, c: (b * nc + c, 0)),
                  pl.BlockSpec((None, POOL_BUF, POOL_WIDTH), lambda b, c: (b, 0, 0)),
                  pl.BlockSpec((len(POOL_WINDOWS), POOL_GROUP_DIM, POOL_GROUP_DIM), lambda b, c: (0, 0, 0)),
                  pl.BlockSpec((1, POOL_WIDTH), lambda b, c: (0, 0))],
        out_specs=pl.BlockSpec((tc, POOL_WIDTH), lambda b, c: (b * nc + c, 0)),
        out_shape=jax.ShapeDtypeStruct((nb * t, POOL_WIDTH), BF16),
        scratch_shapes=[pltpu.VMEM((16 + tc, POOL_WIDTH), F32)],
        compiler_params=_cparams("parallel", "arbitrary"),
    )(u, buf, w_pool, s_pool)


def _pool_step_kernel(u_ref, buf_ref, w_ref, s_ref, o_ref, *, pos0):
    outs = []
    for g, w in enumerate(POOL_WINDOWS):
        lo, hi = g * POOL_GROUP_DIM, (g + 1) * POOL_GROUP_DIM
        x = u_ref[:, lo:hi]
        wsum = x
        for j in range(1, w):
            wsum = wsum + buf_ref[POOL_BUF - j, :, lo:hi]
        cnt = min(pos0 + 1.0, float(w))
        d = (wsum / cnt - x).astype(BF16)
        outs.append(_dot(d, w_ref[g]))
    o_ref[...] = (jnp.concatenate(outs, axis=1) * s_ref[...]).astype(o_ref.dtype)


def _pool_step(u, buf_t, w_pool, s_pool, pos0):
    m = buf_t.shape[1]
    return pl.pallas_call(
        functools.partial(_pool_step_kernel, pos0=pos0), name="pool_step",
        grid=(1,),
        in_specs=[pl.BlockSpec((m, POOL_WIDTH), lambda i: (0, 0)),
                  pl.BlockSpec((POOL_BUF, m, POOL_WIDTH), lambda i: (0, 0, 0)),
                  pl.BlockSpec((len(POOL_WINDOWS), POOL_GROUP_DIM, POOL_GROUP_DIM), lambda i: (0, 0, 0)),
                  pl.BlockSpec((1, POOL_WIDTH), lambda i: (0, 0))],
        out_specs=pl.BlockSpec((m, POOL_WIDTH), lambda i: (0, 0)),
        out_shape=jax.ShapeDtypeStruct((m, POOL_WIDTH), BF16),
        compiler_params=_cparams("arbitrary"),
    )(u, buf_t, w_pool, s_pool)


def _rope128(x, cc, ss):
    lane = lax.broadcasted_iota(jnp.int32, x.shape, 1)
    swapped = jnp.where(lane < MLA_ROPE // 2, pltpu.roll(x, LANES - MLA_ROPE // 2, 1),
                        pltpu.roll(x, MLA_ROPE // 2, 1))
    return x * cc + swapped * ss


def _mla_prep_kernel(uq_ref, ukv_ref, ukr_ref, gql_ref, gkvl_ref, wq_ref, wkv_ref, gq_ref, gk_ref,
                     cc_ref, ss_ref, *out_refs, emit_qg):
    if emit_qg:
        q_ref, k_ref, v_ref, lat_ref, qg_ref = out_refs
    else:
        q_ref, k_ref, v_ref, lat_ref = out_refs
    cc = cc_ref[...]
    ss = ss_ref[...]
    scale = MLA_QK ** -0.5

    uq = uq_ref[...]
    qn = uq * lax.rsqrt(jnp.mean(uq * uq, axis=-1, keepdims=True) + EPS) * gql_ref[...]
    qall = _dot(qn.astype(BF16), wq_ref[...])
    ukv = ukv_ref[...]
    lat = ukv * lax.rsqrt(jnp.mean(ukv * ukv, axis=-1, keepdims=True) + EPS) * gkvl_ref[...]
    lat_ref[...] = lat
    kvall = _dot(lat.astype(BF16), wkv_ref[...])
    kr = ukr_ref[...]
    kr_ss = jnp.sum(kr * kr, axis=-1, keepdims=True)

    gq_n, gq_r = gq_ref[:, :MLA_NOPE], gq_ref[:, MLA_NOPE:]
    gk_n, gk_r = gk_ref[:, :MLA_NOPE], gk_ref[:, MLA_NOPE:]
    nh = MLA_HEADS
    for h in range(nh):
        qnope = qall[:, h * 128:(h + 1) * 128]
        qrope = qall[:, (nh + h) * 128:(nh + h + 1) * 128]
        ssq = jnp.sum(qnope * qnope, axis=-1, keepdims=True) + jnp.sum(qrope * qrope, axis=-1, keepdims=True)
        rs = lax.rsqrt(ssq * (1.0 / MLA_QK) + EPS) * scale
        qn_h = qnope * rs * gq_n
        qr_h = _rope128(qrope * rs * gq_r, cc, ss)
        q_ref[h] = jnp.concatenate([qn_h, qr_h], axis=1).astype(q_ref.dtype)
        if emit_qg:
            qg_ref[h] = (qn_h * gk_n).astype(qg_ref.dtype)

        knope = kvall[:, h * 256:h * 256 + 128]
        ssk = jnp.sum(knope * knope, axis=-1, keepdims=True) + kr_ss
        rsk = lax.rsqrt(ssk * (1.0 / MLA_QK) + EPS)
        kr_h = _rope128(kr * rsk * gk_r, cc, ss)
        k_ref[h] = jnp.concatenate([knope * rsk * gk_n, kr_h], axis=1).astype(k_ref.dtype)
        v_ref[h] = kvall[:, h * 256 + 128:(h + 1) * 256].astype(v_ref.dtype)


def _mla_prep(u, m, tm, pos_of_tile, tabs, prm, emit_qg):
    cc_tab, ss_tab = tabs
    nh = MLA_HEADS
    out_shape = [jax.ShapeDtypeStruct((nh, m, MLA_QKP), BF16),
                 jax.ShapeDtypeStruct((nh, m, MLA_QKP), BF16),
                 jax.ShapeDtypeStruct((nh, m, MLA_V), BF16),
                 jax.ShapeDtypeStruct((m, MLA_KV_LORA), F32)]
    out_specs = [pl.BlockSpec((nh, tm, MLA_QKP), lambda i: (0, i, 0)),
                 pl.BlockSpec((nh, tm, MLA_QKP), lambda i: (0, i, 0)),
                 pl.BlockSpec((nh, tm, MLA_V), lambda i: (0, i, 0)),
                 pl.BlockSpec((tm, MLA_KV_LORA), lambda i: (i, 0))]
    if emit_qg:
        out_shape.append(jax.ShapeDtypeStruct((nh, m, MLA_NOPE), BF16))
        out_specs.append(pl.BlockSpec((nh, tm, MLA_NOPE), lambda i: (0, i, 0)))
    full = lambda a: pl.BlockSpec(a.shape, lambda i: (0,) * a.ndim)
    return pl.pallas_call(
        functools.partial(_mla_prep_kernel, emit_qg=emit_qg), name="mla_prep",
        grid=(m // tm,),
        in_specs=[pl.BlockSpec((tm, MLA_Q_LORA), lambda i: (i, POOL_WIDTH // MLA_Q_LORA)),
                  pl.BlockSpec((tm, MLA_KV_LORA), lambda i: (i, (POOL_WIDTH + MLA_Q_LORA) // MLA_KV_LORA)),
                  pl.BlockSpec((tm, LANES), lambda i: (i, (POOL_WIDTH + MLA_Q_LORA + MLA_KV_LORA) // LANES)),
                  full(prm["g_qlat"]), full(prm["g_kvlat"]), full(prm["w_qb"]), full(prm["w_kvb"]),
                  full(prm["g_q"]), full(prm["g_k"]),
                  pl.BlockSpec((tm, LANES), lambda i: (pos_of_tile(i), 0)),
                  pl.BlockSpec((tm, LANES), lambda i: (pos_of_tile(i), 0))],
        out_specs=out_specs,
        out_shape=out_shape,
        compiler_params=_cparams("parallel"),
    )(u, u, u, prm["g_qlat"], prm["g_kvlat"], prm["w_qb"], prm["w_kvb"], prm["g_q"], prm["g_k"],
      cc_tab, ss_tab)


def _flash_kernel(q_ref, k_ref, v_ref, o_ref, *, tq, tk):
    i = pl.program_id(2)
    q = q_ref[...]
    per_q = tq // tk

    def step(j, carry, band):
        m, l, acc = carry
        start = pl.multiple_of(j * tk, tk)
        kb = k_ref[pl.ds(start, tk), :]
        vb = v_ref[pl.ds(start, tk), :]
        s = _dot_nt(q, kb)
        if band is not None:
            r = lax.broadcasted_iota(jnp.int32, (tq, tk), 0)
            c = lax.broadcasted_iota(jnp.int32, (tq, tk), 1) + band * tk
            s = jnp.where(c <= r, s, -jnp.inf)
        m_new = jnp.maximum(m, jnp.max(s, axis=1, keepdims=True))
        alpha = jnp.exp(m - m_new)
        p = jnp.exp(s - m_new)
        l = alpha * l + jnp.sum(p, axis=1, keepdims=True)
        acc = alpha * acc + _dot(p.astype(BF16), vb)
        return m_new, l, acc

    carry = (jnp.full((tq, 1), -jnp.inf, F32), jnp.zeros((tq, 1), F32), jnp.zeros((tq, MLA_V), F32))
    carry = lax.fori_loop(0, i * per_q, lambda j, c: step(j, c, None), carry)
    for band in range(per_q):
        carry = step(i * per_q + band, carry, band)
    _, l, acc = carry
    o_ref[...] = (acc / l).astype(o_ref.dtype)


FLASH_TQ = 1024
FLASH_TK = 1024


def _flash_attention(q, k, v, nb, t):
    tq, tk = FLASH_TQ, FLASH_TK
    nq = t // tq
    nh = MLA_HEADS
    return pl.pallas_call(
        functools.partial(_flash_kernel, tq=tq, tk=tk), name="flash_prompt",
        grid=(nb, nh, nq),
        in_specs=[pl.BlockSpec((None, tq, MLA_QKP), lambda b, h, i: (h, b * nq + i, 0)),
                  pl.BlockSpec((None, t, MLA_QKP), lambda b, h, i: (h, b, 0)),
                  pl.BlockSpec((None, t, MLA_V), lambda b, h, i: (h, b, 0))],
        out_specs=pl.BlockSpec((tq, MLA_V), lambda b, h, i: (b * nq + i, h)),
        out_shape=jax.ShapeDtypeStruct((nb * t, nh * MLA_V), BF16),
        compiler_params=_cparams("parallel", "parallel", "arbitrary"),
    )(q, k, v)


DEC_CHUNK_PAGES = 8
DEC_SUB_PAGES = 8
DEC_QROWS = 16


def _page_copy(pt_ref, cache_ref, buf_ref, sem_ref, layer, b, i, slot):
    return pltpu.make_async_copy(cache_ref.at[layer, pt_ref[b, i]], buf_ref.at[slot, i], sem_ref.at[slot])


def _decode_kernel(pt_ref, cache_ref, wt_ref, qp_ref, qr_ref, gr_ref, cc_ref, ss_ref, qf_ref, kn_ref, latn_ref,
                   o_ref, buf_ref, sem_ref, lhs_ref, *latb_refs, n_pages, nsamp, layer):
    b = pl.program_id(0)
    nh = MLA_HEADS
    slot = lax.rem(b, 2)
    nw = nh * MLA_NOPE

    def fetch(bb, sl):
        for i in range(n_pages):
            _page_copy(pt_ref, cache_ref, buf_ref, sem_ref, layer, bb, i, sl).start()

    @pl.when(b == 0)
    def _():
        fetch(b, slot)
        lhs_ref[0:nw, :] = wt_ref[...]

    @pl.when(b + 1 < nsamp)
    def _():
        fetch(b + 1, 1 - slot)

    for i in range(n_pages):
        _page_copy(pt_ref, cache_ref, buf_ref, sem_ref, layer, b, i, slot).wait()
    pages = [buf_ref.at[slot, i] for i in range(n_pages)]

    lhs_ref[nw:nw + DEC_QROWS, :] = qp_ref[...]
    lhs = lhs_ref[...]
    gr = gr_ref[...]
    half = MLA_ROPE // 2
    sub = DEC_SUB_PAGES * PAGE_SIZE
    chunk = DEC_CHUNK_PAGES * PAGE_SIZE
    qr = qr_ref[...]

    def tile_scores(c, t):
        i0 = c * DEC_CHUNK_PAGES + t * DEC_SUB_PAGES
        k0 = i0 * PAGE_SIZE
        pg = pages[i0:i0 + DEC_SUB_PAGES]
        latb = jnp.concatenate([p[0:MLA_KV_LORA, :] for p in pg], axis=1).astype(BF16)
        kr = jnp.concatenate([p[MLA_KV_LORA:MLA_KV_LORA + MLA_ROPE, :] for p in pg], axis=1)
        latb_refs[c][:, t * sub:(t + 1) * sub] = latb
        kt = _dot(lhs, latb)
        ssn = jnp.concatenate(
            [jnp.sum(jnp.square(kt[h * 128:(h + 1) * 128, :]), axis=0, keepdims=True) for h in range(nh)],
            axis=0)
        sn = kt[nw:nw + nh, :]
        krg = kr * gr
        swapped = jnp.concatenate([krg[half:, :], krg[:half, :]], axis=0)
        rot = krg * cc_ref[:, k0:k0 + sub] + swapped * ss_ref[:, k0:k0 + sub]
        feat = jnp.concatenate([rot, kr * kr], axis=0).astype(BF16)
        rr = _dot(qr, feat)
        rs = lax.rsqrt((ssn + rr[nh:nh + 1]) * (1.0 / MLA_QK) + EPS)
        return rs * (sn + rr[:nh])

    def chunk_update(c, s, state):
        m_run, l_run, acc = state
        m_new = jnp.maximum(m_run, jnp.max(s, axis=1, keepdims=True))
        alpha = jnp.exp(m_run - m_new)
        p = jnp.exp(s - m_new)
        l_run = alpha * l_run + jnp.sum(p, axis=1, keepdims=True)
        acc = alpha * acc + _dot_nt(p.astype(BF16), latb_refs[c][...])
        return m_new, l_run, acc

    n_chunks = n_pages // DEC_CHUNK_PAGES
    n_tiles = DEC_CHUNK_PAGES // DEC_SUB_PAGES
    state = (jnp.full((nh, 1), -jnp.inf, F32), jnp.zeros((nh, 1), F32), jnp.zeros((nh, MLA_KV_LORA), F32))
    s_prev = None
    for c in range(n_chunks):
        s_cur = jnp.concatenate([tile_scores(c, t) for t in range(n_tiles)], axis=1)
        if s_prev is not None:
            state = chunk_update(c - 1, s_prev, state)
        s_prev = s_cur
    m_run, l_run, acc = chunk_update(n_chunks - 1, s_prev, state)

    s_new = jnp.sum(qf_ref[...] * kn_ref[...], axis=1, keepdims=True)
    m_fin = jnp.maximum(m_run, s_new)
    a_fin = jnp.exp(m_run - m_fin)
    p_new = jnp.exp(s_new - m_fin)
    l_fin = a_fin * l_run + p_new
    lat_new = latn_ref[...].astype(BF16).astype(F32)
    o_ref[...] = (a_fin * acc + p_new.astype(BF16).astype(F32) * lat_new) / l_fin


def _mla_decode(cache, layer, page_table, wt_nope, qp, qr, g_rope, cc_keys, ss_keys, qf, kn, lat_new):
    nsamp, n_pages = page_table.shape
    assert n_pages % DEC_CHUNK_PAGES == 0 and DEC_CHUNK_PAGES % DEC_SUB_PAGES == 0
    n_keys = n_pages * PAGE_SIZE
    kvd = cache.shape[-1]
    cache = jnp.transpose(cache, (0, 1, 3, 2))
    nh = MLA_HEADS
    grid_spec = pltpu.PrefetchScalarGridSpec(
        num_scalar_prefetch=1,
        grid=(nsamp,),
        in_specs=[
            pl.BlockSpec(memory_space=pl.ANY),
            pl.BlockSpec(wt_nope.shape, lambda b, pt: (0, 0)),
            pl.BlockSpec((None, DEC_QROWS, MLA_KV_LORA), lambda b, pt: (b, 0, 0)),
            pl.BlockSpec((None, DEC_QROWS, LANES), lambda b, pt: (b, 0, 0)),
            pl.BlockSpec((MLA_ROPE, 1), lambda b, pt: (0, 0)),
            pl.BlockSpec((MLA_ROPE, n_keys), lambda b, pt: (0, 0)),
            pl.BlockSpec((MLA_ROPE, n_keys), lambda b, pt: (0, 0)),
            pl.BlockSpec((None, nh, MLA_QKP), lambda b, pt: (b, 0, 0)),
            pl.BlockSpec((None, nh, MLA_QKP), lambda b, pt: (b, 0, 0)),
            pl.BlockSpec((None, 1, MLA_KV_LORA), lambda b, pt: (b, 0, 0)),
        ],
        out_specs=pl.BlockSpec((None, nh, MLA_KV_LORA), lambda b, pt: (b, 0, 0)),
        scratch_shapes=[pltpu.VMEM((2, n_pages, kvd, PAGE_SIZE), F32), pltpu.SemaphoreType.DMA((2,)),
                        pltpu.VMEM((nh * MLA_NOPE + DEC_QROWS, MLA_KV_LORA), BF16),
                        ] + [pltpu.VMEM((MLA_KV_LORA, DEC_CHUNK_PAGES * PAGE_SIZE), BF16)
                             for _ in range(n_pages // DEC_CHUNK_PAGES)],
    )
    return pl.pallas_call(
        functools.partial(_decode_kernel, n_pages=n_pages, nsamp=nsamp, layer=layer), name="mla_decode",
        grid_spec=grid_spec,
        out_shape=jax.ShapeDtypeStruct((nsamp, nh, MLA_KV_LORA), F32),
        compiler_params=_cparams("arbitrary"),
    )(page_table, cache, wt_nope, qp, qr, g_rope, cc_keys, ss_keys, qf, kn, lat_new)


def _ssd_gate_norm(y, xs, z, dskip, gnorm):
    y = (y + dskip * xs) * _silu(z)
    gw = SSD_D_INNER // SSD_GROUPS
    outs = []
    for g in range(SSD_GROUPS):
        yg = y[:, g * gw:(g + 1) * gw]
        outs.append(yg * lax.rsqrt(jnp.mean(yg * yg, axis=-1, keepdims=True) + EPS))
    return jnp.concatenate(outs, axis=1) * gnorm


def _ssd_prompt_kernel(z_ref, x_ref, bc_ref, dt_ref, bufx_ref, bufbc_ref, wx_ref, bx_ref, wbc_ref, bbc_ref,
                       dtb_ref, alog_ref, dskip_ref, gnorm_ref, h0_ref, tri_ref, exp_ref,
                       y_ref, hout_ref, extx_ref, extbc_ref, ht_ref, *, nc):
    c = pl.program_id(1)
    first = c == 0
    l = SSD_CHUNK
    xs = _silu(_conv_chunk(extx_ref, x_ref[...], bufx_ref, wx_ref, bx_ref, first))
    bcs = _silu(_conv_chunk(extbc_ref, bc_ref[...], bufbc_ref, wbc_ref, bbc_ref, first))
    gs = SSD_GROUPS * SSD_STATE
    bm, cm = bcs[:, :gs], bcs[:, gs:]

    @pl.when(first)
    def _():
        for r in range(SSD_D_INNER // LANES):
            ht_ref[:, r * LANES:(r + 1) * LANES] = h0_ref[r * LANES:(r + 1) * LANES, :].T

    dt = _softplus(dt_ref[...] + dtb_ref[...])
    a = -jnp.exp(alog_ref[...])
    acum = _dot_sel_left(tri_ref[...], dt * a)
    acum_t = acum.T
    dt_t = dt.T
    a_last = acum[l - 1:l, :]
    dend = jnp.exp(a_last - acum)
    sel = exp_ref[...]
    w_state = _dot_sel(dt * dend, sel)
    e_acc = _dot_sel(jnp.exp(acum), sel)
    e_last = e_acc[l - 1:l, :]

    rows = lax.broadcasted_iota(jnp.int32, (l, l), 0)
    cols = lax.broadcasted_iota(jnp.int32, (l, l), 1)
    causal = cols <= rows
    lane = lax.broadcasted_iota(jnp.int32, (l, LANES), 1)
    xs_b = xs.astype(BF16)
    hpg = SSD_HEADS // SSD_GROUPS
    gw = SSD_D_INNER // SSD_GROUPS
    y_groups = []
    for g in range(SSD_GROUPS):
        gcols = slice(g * gw, (g + 1) * gw)
        bg = bm[:, g * SSD_STATE:(g + 1) * SSD_STATE]
        cg_b = cm[:, g * SSD_STATE:(g + 1) * SSD_STATE].astype(BF16)
        cb = _dot_nt(cg_b, bg.astype(BF16))
        ht_g = ht_ref[:, gcols]
        y_off = _dot(cg_b, ht_g.astype(BF16))
        y_pairs = []
        for pr in range(hpg // 2):
            col = (g * hpg + pr * 2) * SSD_HEAD_DIM
            res = []
            for hh in range(2):
                h = g * hpg + pr * 2 + hh
                diff = acum[:, h:h + 1] - acum_t[h:h + 1, :]
                lmat = jnp.exp(jnp.where(causal, diff, -jnp.inf)) * dt_t[h:h + 1, :]
                res.append(_dot((cb * lmat).astype(BF16), xs_b[:, col:col + LANES]))
            y_pairs.append(jnp.where(lane < SSD_HEAD_DIM, res[0], res[1]))
        y_groups.append(jnp.concatenate(y_pairs, axis=1) + e_acc[:, gcols] * y_off)
        xw = (xs[:, gcols] * w_state[:, gcols]).astype(BF16)
        ht_ref[:, gcols] = e_last[:, gcols] * ht_g + _dot(bg.T.astype(BF16), xw)
    y = jnp.concatenate(y_groups, axis=1)
    y_ref[...] = _ssd_gate_norm(y, xs, z_ref[...], dskip_ref[...], gnorm_ref[...]).astype(y_ref.dtype)

    @pl.when(c == nc - 1)
    def _():
        for r in range(SSD_D_INNER // LANES):
            hout_ref[r * LANES:(r + 1) * LANES, :] = ht_ref[:, r * LANES:(r + 1) * LANES].T


def _dot_sel_left(sel, x):
    hi, mid, lo = _split3(x)
    return _dot(sel, hi) + _dot(sel, mid) + _dot(sel, lo)


def _ssd_consts():
    l = SSD_CHUNK
    tri = (jnp.arange(l)[:, None] >= jnp.arange(l)[None, :]).astype(BF16)
    sel = (jnp.arange(LANES)[:, None] == (jnp.arange(SSD_D_INNER)[None, :] // SSD_HEAD_DIM)).astype(BF16)
    return tri, sel


def _ssd_prompt(u, u_dt, bufx, bufbc, h0, prm, nb, t):
    l = SSD_CHUNK
    nc = t // l
    tri, sel = _ssd_consts()
    full = lambda a: pl.BlockSpec(a.shape, lambda b, c: (0,) * a.ndim)
    row = lambda width, colblk: pl.BlockSpec((l, width), lambda b, c: (b * nc + c, colblk))
    y, hout = pl.pallas_call(
        functools.partial(_ssd_prompt_kernel, nc=nc), name="ssd_prompt",
        grid=(nb, nc),
        in_specs=[row(SSD_D_INNER, 0), row(SSD_D_INNER, 1), row(SSD_BC, 4), row(LANES, 0),
                  pl.BlockSpec((None, CONV_WIDTH - 1, SSD_D_INNER), lambda b, c: (b, 0, 0)),
                  pl.BlockSpec((None, CONV_WIDTH - 1, SSD_BC), lambda b, c: (b, 0, 0)),
                  full(prm["w_conv_x"]), full(prm["b_conv_x"]), full(prm["w_conv_bc"]), full(prm["b_conv_bc"]),
                  full(prm["dt_bias"]), full(prm["a_log"]), full(prm["d_skip"]), full(prm["g_ssd_norm"]),
                  pl.BlockSpec((None, SSD_D_INNER, SSD_STATE), lambda b, c: (b, 0, 0)),
                  full(tri), full(sel)],
        out_specs=[pl.BlockSpec((l, SSD_D_INNER), lambda b, c: (b * nc + c, 0)),
                   pl.BlockSpec((None, SSD_D_INNER, SSD_STATE), lambda b, c: (b, 0, 0))],
        out_shape=[jax.ShapeDtypeStruct((nb * t, SSD_D_INNER), BF16),
                   jax.ShapeDtypeStruct((nb, SSD_D_INNER, SSD_STATE), F32)],
        scratch_shapes=[pltpu.VMEM((CONV_HALO + l, SSD_D_INNER), F32),
                        pltpu.VMEM((CONV_HALO + l, SSD_BC), F32),
                        pltpu.VMEM((SSD_STATE, SSD_D_INNER), F32)],
        compiler_params=_cparams("parallel", "arbitrary"),
    )(u, u, u, u_dt, bufx, bufbc, prm["w_conv_x"], prm["b_conv_x"], prm["w_conv_bc"], prm["b_conv_bc"],
      prm["dt_bias"], prm["a_log"], prm["d_skip"], prm["g_ssd_norm"], h0, tri, sel)
    return y, hout


def _ssd_step_prep_kernel(x_ref, bc_ref, dt_ref, bufx_ref, bufbc_ref, wx_ref, bx_ref, wbc_ref, bbc_ref,
                          dtb_ref, alog_ref, exp_ref, xs_ref, bcs_ref, dtx_ref, dae_ref):
    xs = _silu(_conv_step(bufx_ref, x_ref[...], wx_ref, bx_ref))
    xs_ref[...] = xs
    bcs_ref[...] = _silu(_conv_step(bufbc_ref, bc_ref[...], wbc_ref, bbc_ref))
    dt = _softplus(dt_ref[...] + dtb_ref[...])
    da = jnp.exp(dt * (-jnp.exp(alog_ref[...])))
    sel = exp_ref[...]
    dtx_ref[...] = _dot_sel(dt, sel) * xs
    dae_ref[...] = _dot_sel(da, sel)


def _ssd_step_prep(u, u_dt, bufx_t, bufbc_t, prm):
    m = u_dt.shape[0]
    _, sel = _ssd_consts()
    full = lambda a: pl.BlockSpec(a.shape, lambda i: (0,) * a.ndim)
    return pl.pallas_call(
        _ssd_step_prep_kernel, name="ssd_step_prep",
        grid=(1,),
        in_specs=[pl.BlockSpec((m, SSD_D_INNER), lambda i: (0, 1)),
                  pl.BlockSpec((m, SSD_BC), lambda i: (0, 4)),
                  full(u_dt), full(bufx_t), full(bufbc_t),
                  full(prm["w_conv_x"]), full(prm["b_conv_x"]), full(prm["w_conv_bc"]), full(prm["b_conv_bc"]),
                  full(prm["dt_bias"]), full(prm["a_log"]), full(sel)],
        out_specs=[pl.BlockSpec((m, SSD_D_INNER), lambda i: (0, 0)),
                   pl.BlockSpec((m, SSD_BC), lambda i: (0, 0)),
                   pl.BlockSpec((m, SSD_D_INNER), lambda i: (0, 0)),
                   pl.BlockSpec((m, SSD_D_INNER), lambda i: (0, 0))],
        out_shape=[jax.ShapeDtypeStruct((m, SSD_D_INNER), F32),
                   jax.ShapeDtypeStruct((m, SSD_BC), F32),
                   jax.ShapeDtypeStruct((m, SSD_D_INNER), F32),
                   jax.ShapeDtypeStruct((m, SSD_D_INNER), F32)],
        compiler_params=_cparams("arbitrary"),
    )(u, u, u_dt, bufx_t, bufbc_t, prm["w_conv_x"], prm["b_conv_x"], prm["w_conv_bc"], prm["b_conv_bc"],
      prm["dt_bias"], prm["a_log"], sel)


SSD_STEP_SAMPLES = 4


def _ssd_step_state_kernel(h0_ref, cols_ref, b_ref, c_ref, hn_ref, y_ref):
    nr = SSD_D_INNER // LANES
    rows_per_group = SSD_D_INNER // SSD_GROUPS // LANES
    gw = SSD_D_INNER // SSD_GROUPS
    for s in range(SSD_STEP_SAMPLES):
        pad = jnp.concatenate([cols_ref[s], jnp.zeros((LANES - 2 * nr, LANES), F32)], axis=0)
        ct = pad.T
        for r in range(nr):
            g = r // rows_per_group
            hn_ref[s, r * LANES:(r + 1) * LANES, :] = (
                ct[:, nr + r:nr + r + 1] * h0_ref[s, r * LANES:(r + 1) * LANES, :]
                + ct[:, r:r + 1] * b_ref[s, g:g + 1, :])
        res = _dot_nt(c_ref[s].astype(BF16), hn_ref[s].astype(BF16))
        y_ref[s] = jnp.concatenate([res[g:g + 1, g * gw:(g + 1) * gw] for g in range(SSD_GROUPS)], axis=1)


def _ssd_step_state(h0, cols, bmat, cmat):
    m = h0.shape[0]
    sps = SSD_STEP_SAMPLES
    assert m % sps == 0
    per = lambda shape: pl.BlockSpec((sps,) + shape, lambda i: (i, 0, 0))
    return pl.pallas_call(
        _ssd_step_state_kernel, name="ssd_step_state",
        grid=(m // sps,),
        in_specs=[per((SSD_D_INNER, SSD_STATE)), per((2 * SSD_D_INNER // LANES, LANES)),
                  per((8, SSD_STATE)), per((8, SSD_STATE))],
        out_specs=[per((SSD_D_INNER, SSD_STATE)), per((1, SSD_D_INNER))],
        out_shape=[jax.ShapeDtypeStruct((m, SSD_D_INNER, SSD_STATE), F32),
                   jax.ShapeDtypeStruct((m, 1, SSD_D_INNER), F32)],
        compiler_params=_cparams("parallel"),
    )(h0, cols, bmat, cmat)


def _ssd_step_post_kernel(y_ref, xs_ref, z_ref, dskip_ref, gnorm_ref, o_ref):
    o_ref[...] = _ssd_gate_norm(y_ref[...], xs_ref[...], z_ref[...], dskip_ref[...],
                                gnorm_ref[...]).astype(o_ref.dtype)


def _ssd_step_post(y, xs, u, prm):
    m = y.shape[0]
    full = lambda a: pl.BlockSpec(a.shape, lambda i: (0,) * a.ndim)
    return pl.pallas_call(
        _ssd_step_post_kernel, name="ssd_step_post",
        grid=(1,),
        in_specs=[full(y), full(xs), pl.BlockSpec((m, SSD_D_INNER), lambda i: (0, 0)),
                  full(prm["d_skip"]), full(prm["g_ssd_norm"])],
        out_specs=pl.BlockSpec((m, SSD_D_INNER), lambda i: (0, 0)),
        out_shape=jax.ShapeDtypeStruct((m, SSD_D_INNER), BF16),
        compiler_params=_cparams("arbitrary"),
    )(y, xs, u, prm["d_skip"], prm["g_ssd_norm"])


def _lru_gates(xc, wr_ref, br_ref, wi_ref, bi_ref, lam_ref, is_pos0):
    xb = xc.astype(BF16)
    rs, is_ = [], []
    for n in range(LRU_BLOCKS):
        blk = xb[:, n * LRU_BLOCK_DIM:(n + 1) * LRU_BLOCK_DIM]
        rs.append(_dot(blk, wr_ref[n]))
        is_.append(_dot(blk, wi_ref[n]))
    r = _sigmoid(jnp.concatenate(rs, axis=1) + br_ref[...])
    i = _sigmoid(jnp.concatenate(is_, axis=1) + bi_ref[...])
    log_a = -LRU_C * r * _softplus(-lam_ref[...])
    a = jnp.exp(log_a)
    th = jnp.tanh(log_a)
    mult = jnp.sqrt(-2.0 * th / (1.0 - th))
    if is_pos0 is not False:
        mult = jnp.where(is_pos0, 1.0, mult)
    return a, mult * i * xc


SUBLANES = 8


def _scan_rows(a, b, h_in):
    tc, w = a.shape
    ng = tc // SUBLANES
    a3 = a.reshape(ng, SUBLANES, w)
    b3 = b.reshape(ng, SUBLANES, w)
    sub = lax.broadcasted_iota(jnp.int32, a3.shape, 1)
    d = 1
    while d < SUBLANES:
        keep = sub >= d
        b3 = b3 + a3 * jnp.where(keep, pltpu.roll(b3, d, 1), 0.0)
        a3 = a3 * jnp.where(keep, pltpu.roll(a3, d, 1), 1.0)
        d *= 2
    carry = h_in
    outs = []
    for g in range(ng):
        hg = b3[g] + a3[g] * carry
        outs.append(hg)
        carry = hg[SUBLANES - 1:SUBLANES, :]
    return jnp.concatenate(outs, axis=0), carry


def _lru_prompt_kernel(lx_ref, lg_ref, buf_ref, wc_ref, bc_ref, wr_ref, br_ref, wi_ref, bi_ref, lam_ref, h0_ref,
                       y_ref, hout_ref, ext_ref, h_ref, *, tc, nc, pos0):
    c = pl.program_id(1)
    first = c == 0

    @pl.when(first)
    def _():
        h_ref[...] = h0_ref[...]

    xc = _conv_chunk(ext_ref, lx_ref[...], buf_ref, wc_ref, bc_ref, first)
    pos = pos0 + c * tc + lax.broadcasted_iota(jnp.int32, (tc, 1), 0)
    a, b = _lru_gates(xc, wr_ref, br_ref, wi_ref, bi_ref, lam_ref, pos == 0)
    h, h_last = _scan_rows(a, b, h_ref[...])
    h_ref[...] = h_last
    y_ref[...] = (h * _gelu_tanh(lg_ref[...])).astype(y_ref.dtype)

    @pl.when(c == nc - 1)
    def _():
        hout_ref[...] = h_last


def _lru_prompt(u, buf, h0, prm, nb, t, pos0):
    tc = 256
    nc = t // tc
    full = lambda a: pl.BlockSpec(a.shape, lambda b, c: (0,) * a.ndim)
    y, hout = pl.pallas_call(
        functools.partial(_lru_prompt_kernel, tc=tc, nc=nc, pos0=pos0), name="lru_prompt",
        grid=(nb, nc),
        in_specs=[pl.BlockSpec((tc, LRU_WIDTH), lambda b, c: (b * nc + c, 5)),
                  pl.BlockSpec((tc, LRU_WIDTH), lambda b, c: (b * nc + c, 6)),
                  pl.BlockSpec((None, CONV_WIDTH - 1, LRU_WIDTH), lambda b, c: (b, 0, 0)),
                  full(prm["w_conv_lru"]), full(prm["b_conv_lru"]), full(prm["w_lru_r"]), full(prm["b_lru_r"]),
                  full(prm["w_lru_i"]), full(prm["b_lru_i"]), full(prm["lru_lambda"]),
                  pl.BlockSpec((None, 1, LRU_WIDTH), lambda b, c: (b, 0, 0))],
        out_specs=[pl.BlockSpec((tc, LRU_WIDTH), lambda b, c: (b * nc + c, 0)),
                   pl.BlockSpec((None, 1, LRU_WIDTH), lambda b, c: (b, 0, 0))],
        out_shape=[jax.ShapeDtypeStruct((nb * t, LRU_WIDTH), BF16),
                   jax.ShapeDtypeStruct((nb, 1, LRU_WIDTH), F32)],
        scratch_shapes=[pltpu.VMEM((CONV_HALO + tc, LRU_WIDTH), F32), pltpu.VMEM((1, LRU_WIDTH), F32)],
        compiler_params=_cparams("parallel", "arbitrary"),
    )(u, u, buf, prm["w_conv_lru"], prm["b_conv_lru"], prm["w_lru_r"], prm["b_lru_r"], prm["w_lru_i"],
      prm["b_lru_i"], prm["lru_lambda"], h0)
    return y, hout


def _lru_step_kernel(lx_ref, lg_ref, buf_ref, wc_ref, bc_ref, wr_ref, br_ref, wi_ref, bi_ref, lam_ref, h0_ref,
                     y_ref, hout_ref, *, pos0):
    xc = _conv_step(buf_ref, lx_ref[...], wc_ref, bc_ref)
    a, b = _lru_gates(xc, wr_ref, br_ref, wi_ref, bi_ref, lam_ref, pos0 == 0)
    h = b + a * h0_ref[...]
    hout_ref[...] = h
    y_ref[...] = (h * _gelu_tanh(lg_ref[...])).astype(y_ref.dtype)


def _lru_step(u, buf_t, h0, prm, pos0):
    m = h0.shape[0]
    full = lambda a: pl.BlockSpec(a.shape, lambda i: (0,) * a.ndim)
    return pl.pallas_call(
        functools.partial(_lru_step_kernel, pos0=pos0), name="lru_step",
        grid=(1,),
        in_specs=[pl.BlockSpec((m, LRU_WIDTH), lambda i: (0, 5)),
                  pl.BlockSpec((m, LRU_WIDTH), lambda i: (0, 6)),
                  full(buf_t), full(prm["w_conv_lru"]), full(prm["b_conv_lru"]), full(prm["w_lru_r"]),
                  full(prm["b_lru_r"]), full(prm["w_lru_i"]), full(prm["b_lru_i"]), full(prm["lru_lambda"]),
                  full(h0)],
        out_specs=[pl.BlockSpec((m, LRU_WIDTH), lambda i: (0, 0)),
                   pl.BlockSpec((m, LRU_WIDTH), lambda i: (0, 0))],
        out_shape=[jax.ShapeDtypeStruct((m, LRU_WIDTH), BF16),
                   jax.ShapeDtypeStruct((m, LRU_WIDTH), F32)],
        compiler_params=_cparams("arbitrary"),
    )(u, u, buf_t, prm["w_conv_lru"], prm["b_conv_lru"], prm["w_lru_r"], prm["b_lru_r"], prm["w_lru_i"],
      prm["b_lru_i"], prm["lru_lambda"], h0)


def _rope_tables(n_pos):
    half = MLA_ROPE // 2
    inv = ROPE_THETA ** (-np.arange(half, dtype=np.float64) * (2.0 / MLA_ROPE))
    ang = np.arange(n_pos, dtype=np.float64)[:, None] * inv[None, :]
    cos, sin = np.cos(ang), np.sin(ang)
    cc = jnp.asarray(np.concatenate([cos, cos], axis=1), F32)
    ss = jnp.asarray(np.concatenate([-sin, sin], axis=1), F32)
    return cc, ss


def _even_params(e, w_in_e, w_pool, s_pool, g_qlat, w_qb, g_kvlat, w_kvb, g_q, g_k, w_out_e):
    nh = MLA_HEADS
    w_in = jnp.pad(jnp.transpose(w_in_e[e]).astype(BF16), ((0, EVEN_IN_PAD - w_in_e.shape[2]), (0, 0)))
    wq = w_qb[e].reshape(MLA_Q_LORA, nh, MLA_QK)
    wq_nope = wq[:, :, :MLA_NOPE].reshape(MLA_Q_LORA, nh * MLA_NOPE)
    wq_rope = jnp.pad(wq[:, :, MLA_NOPE:], ((0, 0), (0, 0), (0, LANES - MLA_ROPE))).reshape(MLA_Q_LORA, nh * LANES)
    wkv = w_kvb[e].reshape(MLA_KV_LORA, nh, MLA_NOPE + MLA_V)
    w_nope_t = jnp.transpose(wkv[:, :, :MLA_NOPE], (1, 2, 0))
    pad_g = lambda g: jnp.pad(g, (0, MLA_QKP - MLA_QK)).reshape(1, MLA_QKP)
    return dict(
        w_in=w_in,
        w_pool=w_pool[e].astype(BF16),
        s_pool=s_pool[e].reshape(1, POOL_WIDTH),
        g_qlat=g_qlat[e].reshape(1, MLA_Q_LORA),
        g_kvlat=g_kvlat[e].reshape(1, MLA_KV_LORA),
        w_qb=jnp.concatenate([wq_nope, wq_rope], axis=1).astype(BF16),
        w_kvb=w_kvb[e].astype(BF16),
        g_q=pad_g(g_q[e]), g_k=pad_g(g_k[e]),
        g_k_rope=g_k[e][MLA_NOPE:].reshape(MLA_ROPE, 1),
        w_nope_t=w_nope_t.astype(BF16),
        w_v=jnp.transpose(wkv[:, :, MLA_NOPE:], (1, 0, 2)).astype(BF16),
        w_out_pool=w_out_e[e][:POOL_WIDTH].astype(BF16),
        w_out_att=w_out_e[e][POOL_WIDTH:].astype(BF16),
    )


ODD_J2 = SSD_D_INNER + SSD_D_INNER + SSD_BC
ODD_J3 = ODD_J2 + SSD_HEADS
ODD_MAIN = ODD_J2 + 2 * LRU_WIDTH


def _prep_w_in_odd_kernel(w_ref, main_ref, dt_ref):
    kb = w_ref.shape[1]
    main_ref[0:ODD_J2, :] = w_ref[0:ODD_J2, :].astype(BF16)
    main_ref[ODD_J2:ODD_MAIN, :] = w_ref[ODD_J3:ODD_J3 + 2 * LRU_WIDTH, :].astype(BF16)
    dt_ref[...] = jnp.concatenate([w_ref[ODD_J2:ODD_J3, :], jnp.zeros((LANES - SSD_HEADS, kb), F32)],
                                  axis=0).astype(BF16)


def _prep_w_in_odd(w_in_o, o):
    _, k, n = w_in_o.shape
    kb = 256
    return pl.pallas_call(
        _prep_w_in_odd_kernel, name="prep_w_in_odd",
        grid=(k // kb,),
        in_specs=[pl.BlockSpec((None, n, kb), lambda i: (o, 0, i))],
        out_specs=[pl.BlockSpec((ODD_MAIN, kb), lambda i: (0, i)), pl.BlockSpec((LANES, kb), lambda i: (0, i))],
        out_shape=[jax.ShapeDtypeStruct((ODD_MAIN, k), BF16), jax.ShapeDtypeStruct((LANES, k), BF16)],
        compiler_params=_cparams("parallel"),
    )(jnp.transpose(w_in_o, (0, 2, 1)))


def _odd_params(o, w_in_o, w_conv_ssd, b_conv_ssd, dt_bias, a_log, d_skip, g_ssd_norm, w_conv_lru, b_conv_lru,
                w_lru_r, b_lru_r, w_lru_i, b_lru_i, lru_lambda, w_out_o):
    pad_heads = lambda v: jnp.pad(v, (0, LANES - SSD_HEADS)).reshape(1, LANES)
    w_in, w_in_dt = _prep_w_in_odd(w_in_o, o)
    return dict(
        w_in=w_in,
        w_in_dt=w_in_dt,
        w_conv_x=w_conv_ssd[o][:, :SSD_D_INNER], w_conv_bc=w_conv_ssd[o][:, SSD_D_INNER:],
        b_conv_x=b_conv_ssd[o][:SSD_D_INNER].reshape(1, -1), b_conv_bc=b_conv_ssd[o][SSD_D_INNER:].reshape(1, -1),
        dt_bias=pad_heads(dt_bias[o]), a_log=pad_heads(a_log[o]),
        d_skip=jnp.repeat(d_skip[o], SSD_HEAD_DIM).reshape(1, SSD_D_INNER),
        g_ssd_norm=g_ssd_norm[o].reshape(1, SSD_D_INNER),
        w_conv_lru=w_conv_lru[o], b_conv_lru=b_conv_lru[o].reshape(1, LRU_WIDTH),
        w_lru_r=w_lru_r[o].astype(BF16), b_lru_r=b_lru_r[o].reshape(1, LRU_WIDTH),
        w_lru_i=w_lru_i[o].astype(BF16), b_lru_i=b_lru_i[o].reshape(1, LRU_WIDTH),
        lru_lambda=lru_lambda[o].reshape(1, LRU_WIDTH),
        w_out_ssd=w_out_o[o][:SSD_D_INNER].astype(BF16),
        w_out_lru=w_out_o[o][SSD_D_INNER:].astype(BF16),
    )


def _pad_tab(tab):
    return jnp.pad(tab, ((0, 0), (0, LANES - tab.shape[1])))


def _even_layer_prompt(rows, x, g, mod, prm, nb, t):
    u = _norm_matmul_nt(rows, x, g, mod, 1, 0, prm["w_in"], name="in_even", tn=EVEN_IN_PAD)
    zero_buf = jnp.zeros((nb, POOL_BUF, POOL_WIDTH), F32)
    y_pool = _pool_prompt(u, zero_buf, prm["w_pool"], prm["s_pool"], nb, t, 0)
    cc, ss = _rope_tables(t)
    tm = 512
    tiles = t // tm
    q, k, v, lat = _mla_prep(u, rows.m, tm, lambda i: i % tiles, (_pad_tab(cc), _pad_tab(ss)), prm, False)
    y_att = _flash_attention(q, k, v, nb, t)
    u3 = u.reshape(nb, t, EVEN_IN_PAD)
    kr = u3[:, :, POOL_WIDTH + MLA_Q_LORA + MLA_KV_LORA:POOL_WIDTH + MLA_Q_LORA + MLA_KV_LORA + MLA_ROPE]
    mla_rows = jnp.concatenate([lat.reshape(nb, t, MLA_KV_LORA), kr], axis=-1)
    pool_new = u3[:, t - POOL_BUF:, :POOL_WIDTH]
    return [y_pool, y_att], [prm["w_out_pool"], prm["w_out_att"]], mla_rows, pool_new


def _even_layer_sample(rows, x, g, mod, prm, pool_buf, cache, layer, page_table, pos0):
    m = rows.m
    nh = MLA_HEADS
    u = _norm_matmul_nt(rows, x, g, mod, 1, 0, prm["w_in"], name="in_even", tn=640)
    y_pool = _pool_step(u, jnp.transpose(pool_buf, (1, 0, 2)), prm["w_pool"], prm["s_pool"], pos0)
    cc, ss = _rope_tables(pos0 + 1)
    cc_new = jnp.broadcast_to(_pad_tab(cc[pos0:]), (m, LANES))
    ss_new = jnp.broadcast_to(_pad_tab(ss[pos0:]), (m, LANES))
    q, k, v, lat, qg = _mla_prep(u, m, m, lambda i: 0, (cc_new, ss_new), prm, True)
    qp = _bmm_heads(qg, prm["w_nope_t"], BF16)
    qp = jnp.pad(jnp.transpose(qp, (1, 0, 2)), ((0, 0), (0, DEC_QROWS - nh), (0, 0)))
    q_m = jnp.transpose(q, (1, 0, 2))
    qr = jnp.pad(q_m[:, :, MLA_NOPE:MLA_NOPE + MLA_ROPE], ((0, 0), (0, 0), (0, LANES - MLA_ROPE)))
    ones_row = jnp.concatenate([jnp.zeros((m, 1, MLA_ROPE), BF16), jnp.ones((m, 1, LANES - MLA_ROPE), BF16)], axis=2)
    qr = jnp.concatenate([qr, ones_row, jnp.zeros((m, DEC_QROWS - nh - 1, LANES), BF16)], axis=1)
    o_lat = _mla_decode(cache, layer, page_table, prm["w_nope_t"].reshape(nh * MLA_NOPE, MLA_KV_LORA), qp, qr,
                        prm["g_k_rope"], cc[:pos0].T, ss[:pos0].T, q_m.astype(F32),
                        jnp.transpose(k, (1, 0, 2)).astype(F32), lat.reshape(m, 1, MLA_KV_LORA))
    y_att = _bmm_heads(jnp.transpose(o_lat, (1, 0, 2)).astype(BF16), prm["w_v"], BF16)
    y_att = jnp.transpose(y_att, (1, 0, 2)).reshape(m, nh * MLA_V)
    kr = u[:, POOL_WIDTH + MLA_Q_LORA + MLA_KV_LORA:POOL_WIDTH + MLA_Q_LORA + MLA_KV_LORA + MLA_ROPE]
    mla_rows = jnp.concatenate([lat, kr], axis=-1).reshape(m, 1, MLA_KV_LORA + MLA_ROPE)
    pool_new = jnp.concatenate([pool_buf[:, 1:], u[:, None, :POOL_WIDTH]], axis=1)
    return [y_pool, y_att], [prm["w_out_pool"], prm["w_out_att"]], mla_rows, pool_new


def _odd_layer_prompt(rows, hn, prm, nb, t):
    u = _matmul(rows, [hn], [prm["w_in"]], name="in_odd", tm=1024, tn=1792, w_t=True)
    u_dt = _matmul(rows, [hn], [prm["w_in_dt"]], name="in_odd_dt", tn=LANES, w_t=True)
    k1 = CONV_WIDTH - 1
    y_ssd, h_ssd = _ssd_prompt(u, u_dt, jnp.zeros((nb, k1, SSD_D_INNER), F32), jnp.zeros((nb, k1, SSD_BC), F32),
                               jnp.zeros((nb, SSD_D_INNER, SSD_STATE), F32), prm, nb, t)
    y_lru, h_lru = _lru_prompt(u, jnp.zeros((nb, k1, LRU_WIDTH), F32), jnp.zeros((nb, 1, LRU_WIDTH), F32),
                               prm, nb, t, 0)
    u3 = u.reshape(nb, t, -1)
    sconv = u3[:, t - k1:, SSD_D_INNER:2 * SSD_D_INNER + SSD_BC]
    lconv = u3[:, t - k1:, 2 * SSD_D_INNER + SSD_BC:2 * SSD_D_INNER + SSD_BC + LRU_WIDTH]
    return ([y_ssd, y_lru], [prm["w_out_ssd"], prm["w_out_lru"]], sconv,
            h_ssd.reshape(nb, SSD_HEADS, SSD_HEAD_DIM, SSD_STATE), lconv, h_lru.reshape(nb, LRU_WIDTH))


def _odd_layer_sample(rows, hn, prm, sconv_buf, ssd_state, lconv_buf, lru_state, pos0):
    m = rows.m
    u = _matmul(rows, [hn], [prm["w_in"]], name="in_odd", tn=1024, w_t=True)
    u_dt = _matmul(rows, [hn], [prm["w_in_dt"]], name="in_odd_dt", tn=LANES, w_t=True)
    sconv_t = jnp.transpose(sconv_buf, (1, 0, 2))
    xs, bcs, dtx, dae = _ssd_step_prep(u, u_dt, sconv_t[:, :, :SSD_D_INNER], sconv_t[:, :, SSD_D_INNER:], prm)
    nr = SSD_D_INNER // LANES
    cols = jnp.concatenate([dtx.reshape(m, nr, LANES), dae.reshape(m, nr, LANES)], axis=1)
    gs = SSD_GROUPS * SSD_STATE
    pad8 = lambda a: jnp.pad(a.reshape(m, SSD_GROUPS, SSD_STATE), ((0, 0), (0, 8 - SSD_GROUPS), (0, 0)))
    h_new, y = _ssd_step_state(ssd_state.reshape(m, SSD_D_INNER, SSD_STATE), cols,
                               pad8(bcs[:, :gs]), pad8(bcs[:, gs:]))
    y_ssd = _ssd_step_post(y.reshape(m, SSD_D_INNER), xs, u, prm)
    y_lru, h_lru = _lru_step(u, jnp.transpose(lconv_buf, (1, 0, 2)), lru_state, prm, pos0)
    u_xbc = u[:, SSD_D_INNER:2 * SSD_D_INNER + SSD_BC]
    u_lx = u[:, 2 * SSD_D_INNER + SSD_BC:2 * SSD_D_INNER + SSD_BC + LRU_WIDTH]
    sconv = jnp.concatenate([sconv_buf[:, 1:], u_xbc[:, None]], axis=1)
    lconv = jnp.concatenate([lconv_buf[:, 1:], u_lx[:, None]], axis=1)
    return ([y_ssd, y_lru], [prm["w_out_ssd"], prm["w_out_lru"]], sconv,
            h_new.reshape(m, SSD_HEADS, SSD_HEAD_DIM, SSD_STATE), lconv, h_lru)


def kernel(x_prompt, x_sample, cache_mla, state_pool, state_ssd_conv, state_ssd, state_lru_conv, state_lru, page_table, c_prompt, c_sample, g_norm1, g_norm2, w_mod, b_mod, w_mlp1, w_mlp2, w_in_e, w_pool, s_pool, g_qlat, w_qb, g_kvlat, w_kvb, g_q, g_k, w_out_e, w_in_o, w_conv_ssd, b_conv_ssd, dt_bias, a_log, d_skip, g_ssd_norm, w_conv_lru, b_conv_lru, w_lru_r, b_lru_r, w_lru_i, b_lru_i, lru_lambda, w_out_o):
    nb, t, _ = x_prompt.shape
    ns = x_sample.shape[0]
    assert x_sample.shape[1] == 1 and t >= POOL_BUF
    pos0_s = page_table.shape[1] * PAGE_SIZE

    pad_rows = (-(ns + nb)) % 8
    c_all = jnp.concatenate([c_sample, c_prompt, jnp.zeros((pad_rows, D_MODEL), F32)], axis=0)
    mod_all = _modulation(c_all, w_mod, b_mod)

    rows_p = _Rows(nb, t, 512)
    rows_po = _Rows(nb, t, 256)
    rows_s = _Rows(ns, 1, ns)
    xp = x_prompt.reshape(nb * t, D_MODEL)
    xs = x_sample.reshape(ns, D_MODEL)
    w2 = w_mlp2.astype(BF16)

    outs_p, outs_s = {}, {}
    for layer in range(DEPTH):
        mod_p = rows_p.mod_array(mod_all[layer, ns:ns + nb])
        mod_s = rows_s.mod_array(mod_all[layer, :ns])
        if layer % 2 == 0:
            e = layer // 2
            prm = _even_params(e, w_in_e, w_pool, s_pool, g_qlat, w_qb, g_kvlat, w_kvb, g_q, g_k, w_out_e)
            a_p, w_o, mla_p, pool_p = _even_layer_prompt(rows_p, xp, g_norm1[layer], mod_p, prm, nb, t)
            a_s, _, mla_s, pool_s = _even_layer_sample(rows_s, xs, g_norm1[layer], mod_s, prm, state_pool[e],
                                                       cache_mla, e, page_table, pos0_s)
            outs_p.setdefault("mla", []).append(mla_p)
            outs_p.setdefault("pool", []).append(pool_p)
            outs_s.setdefault("mla", []).append(mla_s)
            outs_s.setdefault("pool", []).append(pool_s)
        else:
            o = layer // 2
            prm = _odd_params(o, w_in_o, w_conv_ssd, b_conv_ssd, dt_bias, a_log, d_skip, g_ssd_norm, w_conv_lru,
                              b_conv_lru, w_lru_r, b_lru_r, w_lru_i, b_lru_i, lru_lambda, w_out_o)
            hn_p = _norm_mod(rows_p, xp, g_norm1[layer], mod_p, 1, 0)
            hn_s = _norm_mod(rows_s, xs, g_norm1[layer], mod_s, 1, 0)
            a_p, w_o, sconv_p, ssd_p, lconv_p, lru_p = _odd_layer_prompt(rows_p, hn_p, prm, nb, t)
            a_s, _, sconv_s, ssd_s, lconv_s, lru_s = _odd_layer_sample(
                rows_s, hn_s, prm, state_ssd_conv[o], state_ssd[o], state_lru_conv[o], state_lru[o], pos0_s)
            for d, vals in ((outs_p, (sconv_p, ssd_p, lconv_p, lru_p)), (outs_s, (sconv_s, ssd_s, lconv_s, lru_s))):
                for name, val in zip(("sconv", "ssd", "lconv", "lru"), vals):
                    d.setdefault(name, []).append(val)
        xp, hn2_p = _matmul(rows_po, a_p, w_o, name="out_proj", tn=D_MODEL, res=xp, mod=mod_p, gate_chunk=2,
                            norm=(g_norm2[layer], 4, 3))
        xs, hn2_s = _matmul(rows_s, a_s, w_o, name="out_proj_s", tn=D_MODEL, res=xs, mod=mod_s, gate_chunk=2,
                            norm=(g_norm2[layer], 4, 3))
        act_p = _matmul_ws(hn2_p, w_mlp1, layer, name="mlp1", tm=1024, tn=1024, act="relu2", out_dtype=BF16)
        act_s = _matmul(rows_s, [hn2_s], [(w_mlp1, layer)], name="mlp1_s", tn=1024, act="relu2", out_dtype=BF16)
        xp = _matmul(rows_p, [act_p], [(w2, layer)], name="mlp2", tn=512, res=xp, mod=mod_p, gate_chunk=5)
        xs = _matmul(rows_s, [act_s], [(w_mlp2, layer)], name="mlp2_s", tn=1024, tk=2048, res=xs, mod=mod_s,
                     gate_chunk=5)

    st = lambda d, name: jnp.stack(d[name])
    return (xp.reshape(nb, t, D_MODEL), xs.reshape(ns, 1, D_MODEL),
            st(outs_p, "mla"), st(outs_s, "mla"), st(outs_p, "pool"), st(outs_s, "pool"),
            st(outs_p, "sconv"), st(outs_s, "sconv"), st(outs_p, "ssd"), st(outs_s, "ssd"),
            st(outs_p, "lconv"), st(outs_s, "lconv"), st(outs_p, "lru"), st(outs_s, "lru"))
```
